```python
import math
import jax
import jax.numpy as jnp
from jax import lax
import numpy as np


D_MODEL = 2048
BATCH = 1
SEQ = 8192
DEPTH = 4

CHUNK = 64
QBLK = 128
EPS = 1e-6
N_MIXERS = 3
ROPE_BASE = 10000.0
PLE_DIM = 256
FFN_HIDDEN = ((8 * D_MODEL // 3 + 255) // 256) * 256

A_HEADS = 16
A_KV_HEADS = 4
A_GROUP = A_HEADS // A_KV_HEADS
A_HEAD_DIM = 128
IDX_HEADS = 16
IDX_DIM = 128
IDX_ROPE = 64
IDX_TOPK_MAX = 256
A_Q = A_HEADS * A_HEAD_DIM
A_KV = A_KV_HEADS * A_HEAD_DIM
A_IDXQ = IDX_HEADS * IDX_DIM
A_SPLITS = (A_Q, A_Q + A_KV, A_Q + 2 * A_KV, A_Q + 2 * A_KV + A_IDXQ, A_Q + 2 * A_KV + A_IDXQ + IDX_DIM)
A_IN = A_SPLITS[-1] + IDX_HEADS

T5_BUCKETS = 32
T5_MAX_DISTANCE = 1024

B_HEADS = 32
B_HEAD_DIM = 64
B_PREV_CHUNKS = 8
B_BAND = (B_PREV_CHUNKS + 1) * CHUNK
B_PAD = B_PREV_CHUNKS * CHUNK
B_REL_CLIP = 128

C_HEADS = 16
C_Q_LORA = 512
C_KV_LORA = 512
C_NOPE = 128
C_ROPE = 64
C_V = 128
C_DOWN = C_Q_LORA + C_KV_LORA + C_ROPE

N_A = (DEPTH + 2) // 3
N_B = (DEPTH + 1) // 3
N_C = DEPTH // 3

kernel_name = "hybrid_dsa_chunkband_mla_trunk"


def rms_norm(x, g):
    xf = x.astype(jnp.float32)
    y = xf * lax.rsqrt(jnp.mean(xf * xf, axis=-1, keepdims=True) + EPS)
    return (y * g.astype(jnp.float32)).astype(x.dtype)


def rope(x, pos):
    half = x.shape[-1] // 2
    inv = ROPE_BASE ** (-jnp.arange(half, dtype=jnp.float32) * 2.0 / x.shape[-1])
    ang = pos.astype(jnp.float32)[..., None] * inv
    ang = ang.reshape(ang.shape[:2] + (1,) * (x.ndim - 3) + (half,))
    cos, sin = jnp.cos(ang), jnp.sin(ang)
    xf = x.astype(jnp.float32)
    x1, x2 = xf[..., :half], xf[..., half:]
    return jnp.concatenate([x1 * cos - x2 * sin, x1 * sin + x2 * cos], axis=-1).astype(x.dtype)


def t5_bucket(rel):
    nb = T5_BUCKETS // 2
    max_exact = nb // 2
    ret = jnp.where(rel > 0, nb, 0)
    n = jnp.abs(rel)
    nf = jnp.maximum(n, 1).astype(jnp.float32)
    large = max_exact + (jnp.log(nf / max_exact) / math.log(T5_MAX_DISTANCE / max_exact) * (nb - max_exact)).astype(jnp.int32)
    large = jnp.minimum(large, nb - 1)
    return ret + jnp.where(n < max_exact, n, large)


def to_blocks(a, size):
    b, s = a.shape[:2]
    return jnp.moveaxis(a.reshape((b, s // size, size) + a.shape[2:]), 1, 0)


def from_blocks(o):
    o = jnp.moveaxis(o, 0, 1)
    return o.reshape(o.shape[0], o.shape[1] * o.shape[2], o.shape[3])


def swiglu(h, w_in, w_out):
    gu = h @ w_in
    g, u = gu[..., :FFN_HIDDEN], gu[..., FFN_HIDDEN:]
    return (jax.nn.silu(g) * u) @ w_out


def dsa_sparse_attention(h, pos, w_in, w_out, t5_table):
    b, s, _ = h.shape
    topk = min(IDX_TOPK_MAX, s // 4)
    nb = s // QBLK
    q, k, v, qi, ki, wi = jnp.split(h @ w_in, A_SPLITS, axis=-1)
    q = q.reshape(b, s, A_KV_HEADS, A_GROUP, A_HEAD_DIM)
    k = k.reshape(b, s, A_KV_HEADS, A_HEAD_DIM)
    v = v.reshape(b, s, A_KV_HEADS, A_HEAD_DIM)
    qi = qi.reshape(b, s, IDX_HEADS, IDX_DIM)
    qi = jnp.concatenate([rope(qi[..., :IDX_ROPE], pos), qi[..., IDX_ROPE:]], axis=-1)
    ki = jnp.concatenate([rope(ki[..., :IDX_ROPE], pos), ki[..., IDX_ROPE:]], axis=-1)
    wi = wi * IDX_HEADS ** -0.5
    cid = jnp.arange(s) // CHUNK
    take = jax.vmap(lambda a, i: a[i])

    def block(args):
        qb, qib, wib, pb, bi = args
        qc = (bi * QBLK + jnp.arange(QBLK)) // CHUNK
        sc = jnp.einsum('bqhd,bsd->bqhs', qib, ki).astype(jnp.float32) * IDX_DIM ** -0.5
        score = jnp.einsum('bqhs,bqh->bqs', jax.nn.relu(sc), wib.astype(jnp.float32))
        score = jnp.where(cid[None, None, :] <= qc[None, :, None], score, -jnp.inf)
        _, idx = lax.top_k(score, topk)
        valid = cid[idx] <= qc[None, :, None]
        kg = take(k, idx)
        vg = take(v, idx)
        pg = take(pos, idx)
        bias = t5_table[t5_bucket(pg - pb[:, :, None])]
        bias = jnp.moveaxis(bias.reshape(b, QBLK, topk, A_KV_HEADS, A_GROUP), 2, -1)
        logits = jnp.einsum('bqkgd,bqskd->bqkgs', qb, kg).astype(jnp.float32) * A_HEAD_DIM ** -0.5 + bias.astype(jnp.float32)
        logits = jnp.where(valid[:, :, None, None, :], logits, -jnp.inf)
        pr = jax.nn.softmax(logits, axis=-1).astype(vg.dtype)
        o = jnp.einsum('bqkgs,bqskd->bqkgd', pr, vg)
        return o.reshape(b, QBLK, A_HEADS * A_HEAD_DIM)

    o = lax.map(block, (to_blocks(q, QBLK), to_blocks(qi, QBLK), to_blocks(wi, QBLK), to_blocks(pos, QBLK), jnp.arange(nb)))
    return from_blocks(o) @ w_out


def chunk_band_attention(h, pos, w_in, rel_table, w_out):
    b, s, _ = h.shape
    nc = s // CHUNK
    q, k, v = jnp.split(h @ w_in, 3, axis=-1)
    q = q.reshape(b, s, B_HEADS, B_HEAD_DIM)
    k = jnp.pad(k.reshape(b, s, B_HEADS, B_HEAD_DIM), ((0, 0), (B_PAD, 0), (0, 0), (0, 0)))
    v = jnp.pad(v.reshape(b, s, B_HEADS, B_HEAD_DIM), ((0, 0), (B_PAD, 0), (0, 0), (0, 0)))
    pp = jnp.pad(pos, ((0, 0), (B_PAD, 0)))

    def per_chunk(c):
        s0 = c * CHUNK
        qc = lax.dynamic_slice_in_dim(q, s0, CHUNK, axis=1)
        pq = lax.dynamic_slice_in_dim(pos, s0, CHUNK, axis=1)
        kb = lax.dynamic_slice_in_dim(k, s0, B_BAND, axis=1)
        vb = lax.dynamic_slice_in_dim(v, s0, B_BAND, axis=1)
        pb = lax.dynamic_slice_in_dim(pp, s0, B_BAND, axis=1)
        valid = (s0 - B_PAD + jnp.arange(B_BAND)) >= 0
        rel = jnp.clip(pq[:, :, None] - pb[:, None, :], -B_REL_CLIP, B_REL_CLIP) + B_REL_CLIP
        bias = jnp.moveaxis(rel_table[rel], -1, 1)
        logits = jnp.einsum('bqhd,bkhd->bhqk', qc, kb).astype(jnp.float32) * B_HEAD_DIM ** -0.5 + bias.astype(jnp.float32)
        logits = jnp.where(valid, logits, -jnp.inf)
        pr = jax.nn.softmax(logits, axis=-1).astype(vb.dtype)
        o = jnp.einsum('bhqk,bkhd->bqhd', pr, vb)
        return o.reshape(b, CHUNK, B_HEADS * B_HEAD_DIM)

    o = lax.map(per_chunk, jnp.arange(nc))
    return from_blocks(o) @ w_out


def mla_attention(h, pos, w_down, g_q, g_kv, w_uq, w_ukv, w_out):
    b, s, _ = h.shape
    nb = s // QBLK
    cq, ckv, kr = jnp.split(h @ w_down, (C_Q_LORA, C_Q_LORA + C_KV_LORA), axis=-1)
    cq = rms_norm(cq, g_q)
    ckv = rms_norm(ckv, g_kv)
    q = (cq @ w_uq).reshape(b, s, C_HEADS, C_NOPE + C_ROPE)
    qn, qr = q[..., :C_NOPE], rope(q[..., C_NOPE:], pos)
    kv = (ckv @ w_ukv).reshape(b, s, C_HEADS, C_NOPE + C_V)
    kn, v = kv[..., :C_NOPE], kv[..., C_NOPE:]
    kr = rope(kr, pos)
    cid = jnp.arange(s) // CHUNK
    scale = (C_NOPE + C_ROPE) ** -0.5

    def block(args):
        qnb, qrb, bi = args
        qc = (bi * QBLK + jnp.arange(QBLK)) // CHUNK
        logits = (jnp.einsum('bqhd,bkhd->bhqk', qnb, kn) + jnp.einsum('bqhd,bkd->bhqk', qrb, kr)).astype(jnp.float32) * scale
        logits = jnp.where(cid[None, :] <= qc[:, None], logits, -jnp.inf)
        pr = jax.nn.softmax(logits, axis=-1).astype(v.dtype)
        o = jnp.einsum('bhqk,bkhd->bqhd', pr, v)
        return o.reshape(b, QBLK, C_HEADS * C_V)

    o = lax.map(block, (to_blocks(qn, QBLK), to_blocks(qr, QBLK), jnp.arange(nb)))
    return from_blocks(o) @ w_out


def setup_inputs(seed: int = 0) -> dict:
    key = jax.random.key(seed)
    ks = jax.random.split(key, 24)
    f32 = jnp.float32

    def dense(k, shape):
        return jax.random.normal(k, shape, f32) * shape[-2] ** -0.5

    def gain(k, shape):
        return 1.0 + 0.05 * jax.random.normal(k, shape, f32)

    x = jax.random.normal(ks[0], (BATCH, SEQ, D_MODEL), f32)
    p = jax.random.normal(ks[1], (DEPTH, BATCH, SEQ, PLE_DIM), f32)
    offset = jax.random.randint(ks[2], (BATCH, 1), 0, 1024, jnp.int32)
    positions = offset + jnp.arange(SEQ, dtype=jnp.int32)[None, :]
    return {
        "x": x,
        "p": p,
        "positions": positions,
        "t5_table": 0.5 * jax.random.normal(ks[3], (T5_BUCKETS, A_HEADS), f32),
        "a_w_in": dense(ks[4], (N_A, D_MODEL, A_IN)),
        "a_w_out": dense(ks[5], (N_A, A_HEADS * A_HEAD_DIM, D_MODEL)),
        "b_w_in": dense(ks[6], (N_B, D_MODEL, 3 * B_HEADS * B_HEAD_DIM)),
        "b_rel_table": 0.5 * jax.random.normal(ks[7], (N_B, 2 * B_REL_CLIP + 1, B_HEADS), f32),
        "b_w_out": dense(ks[8], (N_B, B_HEADS * B_HEAD_DIM, D_MODEL)),
        "c_w_down": dense(ks[9], (N_C, D_MODEL, C_DOWN)),
        "c_q_norm": gain(ks[10], (N_C, C_Q_LORA)),
        "c_kv_norm": gain(ks[11], (N_C, C_KV_LORA)),
        "c_w_uq": dense(ks[12], (N_C, C_Q_LORA, C_HEADS * (C_NOPE + C_ROPE))),
        "c_w_ukv": dense(ks[13], (N_C, C_KV_LORA, C_HEADS * (C_NOPE + C_V))),
        "c_w_out": dense(ks[14], (N_C, C_HEADS * C_V, D_MODEL)),
        "attn_norm": gain(ks[15], (DEPTH, D_MODEL)),
        "ffn_norm": gain(ks[16], (DEPTH, D_MODEL)),
        "ffn_w_in": dense(ks[17], (DEPTH, D_MODEL, 2 * FFN_HIDDEN)),
        "ffn_w_out": dense(ks[18], (DEPTH, FFN_HIDDEN, D_MODEL)),
        "ple_norm": gain(ks[19], (DEPTH, D_MODEL)),
        "ple_w_gate": dense(ks[20], (DEPTH, D_MODEL, D_MODEL)),
        "ple_w_proj": dense(ks[21], (DEPTH, PLE_DIM, D_MODEL)),
        "final_norm": gain(ks[22], (D_MODEL,)),
    }


def reference(x, p, positions, t5_table, a_w_in, a_w_out, b_w_in, b_rel_table, b_w_out,
              c_w_down, c_q_norm, c_kv_norm, c_w_uq, c_w_ukv, c_w_out,
              attn_norm, ffn_norm, ffn_w_in, ffn_w_out, ple_norm, ple_w_gate, ple_w_proj, final_norm):
    h = x
    for i in range(DEPTH):
        j = i // N_MIXERS
        kind = i % N_MIXERS
        hn = rms_norm(h, attn_norm[i])
        if kind == 0:
            mix = dsa_sparse_attention(hn, positions, a_w_in[j], a_w_out[j], t5_table)
        elif kind == 1:
            mix = chunk_band_attention(hn, positions, b_w_in[j], b_rel_table[j], b_w_out[j])
        else:
            mix = mla_attention(hn, positions, c_w_down[j], c_q_norm[j], c_kv_norm[j], c_w_uq[j], c_w_ukv[j], c_w_out[j])
        h = h + mix
        h = h + swiglu(rms_norm(h, ffn_norm[i]), ffn_w_in[i], ffn_w_out[i])
        gate = jax.nn.sigmoid(rms_norm(h, ple_norm[i]) @ ple_w_gate[i])
        h = h + gate * (p[i] @ ple_w_proj[i])
    return rms_norm(h, final_norm)
```

```python
import functools
import math

import jax
import jax.numpy as jnp
from jax import lax
from jax.experimental import pallas as pl
from jax.experimental.pallas import tpu as pltpu

LANE = 128
VMEM_LIMIT_BYTES = 56 * 1024 * 1024

CHUNK = 64
CHUNK_SHIFT = 6
EPS = 1e-6
ROPE_BASE = 10000.0
ROPE_DIM = 64
A_HEADS = 16
A_KV_HEADS = 4
A_GROUP = 4
A_HEAD_DIM = 128
IDX_HEADS = 16
IDX_DIM = 128
IDX_TOPK_MAX = 256
T5_BUCKETS = 32
T5_MAX_DISTANCE = 1024
T5_FAR = 640
B_HEADS = 32
B_HEAD_DIM = 64
B_PREV_CHUNKS = 8
B_REL_CLIP = 128
C_HEADS = 16
C_NOPE = 128
C_ROPE = 64
C_V = 128
NEG_BIG = -1e30
INT_MIN = -(2 ** 31)

F32 = jnp.float32
BF16 = jnp.bfloat16
NT_DIMS = (((1,), (1,)), ((), ()))


def _params(sem):
    return pltpu.CompilerParams(dimension_semantics=sem, vmem_limit_bytes=VMEM_LIMIT_BYTES)


def _rms(x, g):
    ms = jnp.mean(x * x, axis=-1, keepdims=True)
    return (x * lax.rsqrt(ms + EPS)) * g


def _rope_group(y, c, s1, s2):
    return y * c + pltpu.roll(y, 96, 1) * s1 + pltpu.roll(y, 32, 1) * s2


def _norm_mm_kernel(*refs, rope):
    if rope is None:
        x_ref, g_ref, w_ref, o_ref, xn_ref = refs
    else:
        x_ref, g_ref, w_ref, c_ref, s1_ref, s2_ref, o_ref, xn_ref = refs

    @pl.when(pl.program_id(1) == 0)
    def _():
        xn_ref[...] = _rms(x_ref[...], g_ref[...]).astype(BF16)

    y = jnp.dot(xn_ref[...], w_ref[...], preferred_element_type=F32)
    if rope is None:
        o_ref[...] = y.astype(o_ref.dtype)
    else:
        c, s1, s2 = c_ref[...], s1_ref[...], s2_ref[...]
        for gi, on in enumerate(rope):
            sl = slice(gi * LANE, (gi + 1) * LANE)
            yg = y[:, sl]
            if on:
                yg = _rope_group(yg, c, s1, s2)
            o_ref[:, sl] = yg.astype(o_ref.dtype)


def norm_mm(x, x_col, kx, gain, w, *, tn, out_dtype, name, tm=512, rope=None, tables=None):
    s = x.shape[0]
    n = w.shape[1]
    tm = min(tm, s)
    in_specs = [
        pl.BlockSpec((tm, kx), lambda i, j: (i, x_col)),
        pl.BlockSpec((1, kx), lambda i, j: (0, 0)),
        pl.BlockSpec((kx, tn), lambda i, j: (0, j)),
    ]
    args = [x, gain.reshape(1, kx), w]
    if rope is not None:
        assert len(rope) == tn // LANE
        in_specs += [pl.BlockSpec((tm, LANE), lambda i, j: (i, 0))] * 3
        args += list(tables)
    return pl.pallas_call(
        functools.partial(_norm_mm_kernel, rope=rope),
        grid=(s // tm, n // tn),
        in_specs=in_specs,
        out_specs=pl.BlockSpec((tm, tn), lambda i, j: (i, j)),
        out_shape=jax.ShapeDtypeStruct((s, n), out_dtype),
        scratch_shapes=[pltpu.VMEM((tm, kx), BF16)],
        compiler_params=_params(("parallel", "arbitrary")),
        name=name,
    )(*args)


def _mm_res_kernel(x_ref, w_ref, r_ref, o_ref):
    o_ref[...] = r_ref[...] + jnp.dot(x_ref[...], w_ref[...], preferred_element_type=F32)


def mm_residual(x, w, res, *, tm=512, tn=512):
    s, k = x.shape
    n = w.shape[1]
    tm = min(tm, s)
    return pl.pallas_call(
        _mm_res_kernel,
        name="mm_residual",
        grid=(s // tm, n // tn),
        in_specs=[
            pl.BlockSpec((tm, k), lambda i, j: (i, 0)),
            pl.BlockSpec((k, tn), lambda i, j: (0, j)),
            pl.BlockSpec((tm, tn), lambda i, j: (i, j)),
        ],
        out_specs=pl.BlockSpec((tm, tn), lambda i, j: (i, j)),
        out_shape=jax.ShapeDtypeStruct((s, n), F32),
        compiler_params=_params(("parallel", "arbitrary")),
    )(x, w, res)


def _ffn_in_kernel(x_ref, g_ref, wg_ref, wu_ref, o_ref, xn_ref):
    @pl.when(pl.program_id(1) == 0)
    def _():
        xn_ref[...] = _rms(x_ref[...], g_ref[...]).astype(BF16)

    xn = xn_ref[...]
    a = jnp.dot(xn, wg_ref[...], preferred_element_type=F32)
    u = jnp.dot(xn, wu_ref[...], preferred_element_type=F32)
    o_ref[...] = (a * jax.nn.sigmoid(a) * u).astype(o_ref.dtype)


def ffn_in(h, gain, w_in, *, tm=1024, tn=512):
    s, d = h.shape
    f = w_in.shape[1] // 2
    tm = min(tm, s)
    nj = f // tn
    return pl.pallas_call(
        _ffn_in_kernel,
        name="ffn_in",
        grid=(s // tm, nj),
        in_specs=[
            pl.BlockSpec((tm, d), lambda i, j: (i, 0)),
            pl.BlockSpec((1, d), lambda i, j: (0, 0)),
            pl.BlockSpec((d, tn), lambda i, j: (0, j)),
            pl.BlockSpec((d, tn), lambda i, j: (0, j + nj)),
        ],
        out_specs=pl.BlockSpec((tm, tn), lambda i, j: (i, j)),
        out_shape=jax.ShapeDtypeStruct((s, f), BF16),
        scratch_shapes=[pltpu.VMEM((tm, d), BF16)],
        compiler_params=_params(("parallel", "arbitrary")),
    )(h, gain.reshape(1, d), w_in, w_in)


def _ple_kernel(x_ref, g_ref, wg_ref, p_ref, wp_ref, h_ref, o_ref, xn_ref, pb_ref):
    @pl.when(pl.program_id(1) == 0)
    def _():
        xn_ref[...] = _rms(x_ref[...], g_ref[...]).astype(BF16)
        pb_ref[...] = p_ref[...].astype(BF16)

    gate = jax.nn.sigmoid(jnp.dot(xn_ref[...], wg_ref[...], preferred_element_type=F32))
    proj = jnp.dot(pb_ref[...], wp_ref[...], preferred_element_type=F32)
    o_ref[...] = h_ref[...] + gate * proj


def ple(h, gain, w_gate, p, w_proj, *, tm=512, tn=1024):
    s, d = h.shape
    pd = p.shape[1]
    tm = min(tm, s)
    return pl.pallas_call(
        _ple_kernel,
        name="ple",
        grid=(s // tm, d // tn),
        in_specs=[
            pl.BlockSpec((tm, d), lambda i, j: (i, 0)),
            pl.BlockSpec((1, d), lambda i, j: (0, 0)),
            pl.BlockSpec((d, tn), lambda i, j: (0, j)),
            pl.BlockSpec((tm, pd), lambda i, j: (i, 0)),
            pl.BlockSpec((pd, tn), lambda i, j: (0, j)),
            pl.BlockSpec((tm, tn), lambda i, j: (i, j)),
        ],
        out_specs=pl.BlockSpec((tm, tn), lambda i, j: (i, j)),
        out_shape=jax.ShapeDtypeStruct((s, d), F32),
        scratch_shapes=[pltpu.VMEM((tm, d), BF16), pltpu.VMEM((tm, pd), BF16)],
        compiler_params=_params(("parallel", "arbitrary")),
    )(h, gain.reshape(1, d), w_gate, p, w_proj, h)


def _final_norm_kernel(x_ref, g_ref, o_ref):
    o_ref[...] = _rms(x_ref[...], g_ref[...])


def final_rms_norm(h, gain, *, tm=512):
    s, d = h.shape
    tm = min(tm, s)
    return pl.pallas_call(
        _final_norm_kernel,
        name="final_norm",
        grid=(s // tm,),
        in_specs=[pl.BlockSpec((tm, d), lambda i: (i, 0)), pl.BlockSpec((1, d), lambda i: (0, 0))],
        out_specs=pl.BlockSpec((tm, d), lambda i: (i, 0)),
        out_shape=jax.ShapeDtypeStruct((s, d), F32),
        compiler_params=_params(("parallel",)),
    )(h, gain.reshape(1, d))


IDX_TQ = 256
IDX_KC = 256


def _orderable(x):
    b = pltpu.bitcast(x, jnp.int32)
    return jnp.where(b < 0, b ^ jnp.int32(0x7FFFFFFF), b)


def _indexer_kernel(qi_ref, ki_ref, wi_ref, mask_ref, keys_ref, *, topk, n_kblocks):
    tq, kc = IDX_TQ, IDX_KC
    i = pl.program_id(0)
    nkc = (i + 1) * (tq // kc)
    w_t = (wi_ref[...] * (IDX_HEADS ** -0.5 * IDX_DIM ** -0.5)).T
    qchunk = (i * tq + lax.broadcasted_iota(jnp.int32, (kc, tq), 1)) >> CHUNK_SHIFT
    krow = lax.broadcasted_iota(jnp.int32, (kc, tq), 0)

    def valid_of(c):
        return ((c * kc + krow) >> CHUNK_SHIFT) <= qchunk

    def score_body(c, carry):
        k0 = pl.multiple_of(c * kc, kc)
        kblk = ki_ref[pl.ds(k0, kc), :]
        acc = jnp.zeros((kc, tq), F32)
        for h in range(IDX_HEADS):
            s = lax.dot_general(kblk, qi_ref[:, h * IDX_DIM:(h + 1) * IDX_DIM], NT_DIMS,
                                preferred_element_type=F32)
            acc = acc + jnp.maximum(s, 0.0) * w_t[h:h + 1, :]
        keys_ref[pl.ds(k0, kc), :] = jnp.where(valid_of(c), _orderable(acc), jnp.int32(INT_MIN))
        return carry

    lax.fori_loop(0, nkc, score_body, 0)

    def count_ge(cand):
        def body(c, cnt):
            k0 = pl.multiple_of(c * kc, kc)
            ge = (keys_ref[pl.ds(k0, kc), :] >= cand).astype(jnp.int32)
            return cnt + ge.reshape(kc // 8, 8, tq).sum(axis=0)
        cnt = lax.fori_loop(0, nkc, body, jnp.zeros((8, tq), jnp.int32))
        return cnt.sum(axis=0, keepdims=True)

    def bit_body(b, t):
        cand = t + lax.shift_left(jnp.int32(1), 31 - b)
        return jnp.where(count_ge(cand) >= topk, cand, t)

    thr = lax.fori_loop(0, 32, bit_body, jnp.full((1, tq), INT_MIN, jnp.int32))

    def mask_body(c, carry):
        k0 = pl.multiple_of(c * kc, kc)
        sel = (keys_ref[pl.ds(k0, kc), :] >= thr) & valid_of(c)
        mask_ref[c] = jnp.where(sel, 0.0, -jnp.inf).astype(F32).T
        return carry

    lax.fori_loop(0, nkc, mask_body, 0)

    def fill_body(c, carry):
        mask_ref[c] = jnp.full((tq, kc), -jnp.inf, F32)
        return carry

    lax.fori_loop(nkc, n_kblocks, fill_body, 0)


def dsa_indexer(qiki, wi, *, topk):
    s = qiki.shape[0]
    tq, kc = IDX_TQ, IDX_KC
    nkb = s // kc
    return pl.pallas_call(
        functools.partial(_indexer_kernel, topk=topk, n_kblocks=nkb),
        name="dsa_indexer",
        grid=(s // tq,),
        in_specs=[
            pl.BlockSpec((tq, IDX_HEADS * IDX_DIM), lambda i: (i, 0)),
            pl.BlockSpec((s, IDX_DIM), lambda i: (0, IDX_HEADS)),
            pl.BlockSpec((tq, LANE), lambda i: (i, 0)),
        ],
        out_specs=pl.BlockSpec((nkb, tq, kc), lambda i: (0, i, 0)),
        out_shape=jax.ShapeDtypeStruct((nkb, s, kc), F32),
        scratch_shapes=[pltpu.VMEM((s, tq), jnp.int32)],
        compiler_params=_params(("parallel",)),
    )(qiki, qiki, wi)


ATT_A_TQ = 256
ATT_A_TK = IDX_KC


def _t5_bucket(rel):
    nb = T5_BUCKETS // 2
    max_exact = nb // 2
    ret = jnp.where(rel > 0, nb, 0)
    n = jnp.abs(rel)
    nf = jnp.maximum(n, 1).astype(F32)
    large = max_exact + (jnp.log(nf / max_exact) / math.log(T5_MAX_DISTANCE / max_exact)
                         * (nb - max_exact)).astype(jnp.int32)
    large = jnp.minimum(large, nb - 1)
    return ret + jnp.where(n < max_exact, n, large)


def _softmax_step(s, m_ref, l_ref, acc_ref, rows, tk):
    m_prev = m_ref[rows, :]
    m_next = jnp.maximum(m_prev, jnp.max(s, axis=1, keepdims=True))
    p = jnp.exp(s - pltpu.repeat(m_next, tk // LANE, axis=1))
    alpha = jnp.exp(m_prev - m_next)
    l_ref[rows, :] = alpha * l_ref[rows, :] + jnp.sum(p, axis=1, keepdims=True)
    m_ref[rows, :] = m_next
    acc_ref[rows, :] = acc_ref[rows, :] * alpha
    return p


def _attn_a_kernel(qmin_ref, kmax_ref, t5s_ref, q_ref, k_ref, v_ref, mask_ref, posq_ref, posk_ref,
                   t5t_ref, o_ref, qs_ref, p_ref, m_ref, l_ref, acc_ref):
    tq, tk = ATT_A_TQ, ATT_A_TK
    i = pl.program_id(0)
    g = pl.program_id(1)
    for r in range(A_GROUP):
        qs_ref[r * tq:(r + 1) * tq, :] = q_ref[:, r * A_HEAD_DIM:(r + 1) * A_HEAD_DIM]
    m_ref[...] = jnp.full(m_ref.shape, NEG_BIG, F32)
    l_ref[...] = jnp.zeros(l_ref.shape, F32)
    acc_ref[...] = jnp.zeros(acc_ref.shape, F32)
    scale = A_HEAD_DIM ** -0.5
    nkt = (i + 1) * (tq // tk)
    qmin = qmin_ref[i]

    def body(kt, carry):
        k0 = pl.multiple_of(kt * tk, tk)
        s_all = lax.dot_general(qs_ref[...], k_ref[pl.ds(k0, tk), :], NT_DIMS,
                                preferred_element_type=F32)
        base = mask_ref[kt]
        far = (qmin - kmax_ref[kt]) >= T5_FAR

        def update(bias_of):
            for r in range(A_GROUP):
                rows = slice(r * tq, (r + 1) * tq)
                s = s_all[rows] * scale + (base + bias_of(r))
                p = _softmax_step(s, m_ref, l_ref, acc_ref, rows, tk)
                p_ref[rows, :] = p.astype(BF16)

        @pl.when(far)
        def _():
            update(lambda r: t5s_ref[T5_BUCKETS // 2 - 1, g * A_GROUP + r])

        @pl.when(jnp.logical_not(far))
        def _():
            bucket = _t5_bucket(posk_ref[kt] - posq_ref[...])

            def bias_of(r):
                row = t5t_ref[pl.ds(g * A_GROUP + r, 1), :]
                tbl = jnp.broadcast_to(row, (tq, LANE))
                return jnp.concatenate(
                    [jnp.take_along_axis(tbl, bucket[:, c * LANE:(c + 1) * LANE], axis=1)
                     for c in range(tk // LANE)], axis=1)

            update(bias_of)

        acc_ref[...] += jnp.dot(p_ref[...], v_ref[pl.ds(k0, tk), :], preferred_element_type=F32)
        return carry

    lax.fori_loop(0, nkt, body, 0)
    for r in range(A_GROUP):
        rows = slice(r * tq, (r + 1) * tq)
        o_ref[:, r * A_HEAD_DIM:(r + 1) * A_HEAD_DIM] = (acc_ref[rows, :] / l_ref[rows, :]).astype(o_ref.dtype)


def dsa_attention(qkv, mask, pos, t5_table):
    s = qkv.shape[0]
    tq, tk = ATT_A_TQ, ATT_A_TK
    gw = A_GROUP * A_HEAD_DIM
    qmin = pos.reshape(s // tq, tq).min(axis=1)
    kmax = pos.reshape(s // tk, tk).max(axis=1)
    t5t = jnp.zeros((A_HEADS, LANE), F32).at[:, :T5_BUCKETS].set(t5_table.T)
    kblk0 = (A_HEADS * A_HEAD_DIM) // A_HEAD_DIM
    smem = pl.BlockSpec(memory_space=pltpu.SMEM)
    return pl.pallas_call(
        _attn_a_kernel,
        name="dsa_attention",
        grid=(s // tq, A_KV_HEADS),
        in_specs=[
            smem, smem, smem,
            pl.BlockSpec((tq, gw), lambda i, g: (i, g)),
            pl.BlockSpec((s, A_HEAD_DIM), lambda i, g: (0, kblk0 + g)),
            pl.BlockSpec((s, A_HEAD_DIM), lambda i, g: (0, kblk0 + A_KV_HEADS + g)),
            pl.BlockSpec((s // tk, tq, tk), lambda i, g: (0, i, 0)),
            pl.BlockSpec((tq, 1), lambda i, g: (i, 0)),
            pl.BlockSpec((s // tk, 1, tk), lambda i, g: (0, 0, 0)),
            pl.BlockSpec((A_HEADS, LANE), lambda i, g: (0, 0)),
        ],
        out_specs=pl.BlockSpec((tq, gw), lambda i, g: (i, g)),
        out_shape=jax.ShapeDtypeStruct((s, A_HEADS * A_HEAD_DIM), BF16),
        scratch_shapes=[
            pltpu.VMEM((A_GROUP * tq, A_HEAD_DIM), BF16),
            pltpu.VMEM((A_GROUP * tq, tk), BF16),
            pltpu.VMEM((A_GROUP * tq, LANE), F32),
            pltpu.VMEM((A_GROUP * tq, LANE), F32),
            pltpu.VMEM((A_GROUP * tq, A_HEAD_DIM), F32),
        ],
        compiler_params=_params(("parallel", "arbitrary")),
    )(qmin, kmax, t5_table, qkv, qkv, qkv, mask, pos.reshape(s, 1), pos.reshape(s // tk, 1, tk), t5t)


ATT_B_TQ = 256
ATT_B_NKB = 3


def _attn_b_kernel(t256_ref, q_ref, k0_ref, k1_ref, k2_ref, v0_ref, v1_ref, v2_ref, posq_ref, posk_ref,
                   relt_ref, o_ref, ridx_ref, madd_ref):
    tq = ATT_B_TQ
    i = pl.program_id(0)
    hp = pl.program_id(1)
    k_refs = (k0_ref, k1_ref, k2_ref)
    v_refs = (v0_ref, v1_ref, v2_ref)

    @pl.when(hp == 0)
    def _():
        pq = posq_ref[...]
        qchunk = (i * tq + lax.broadcasted_iota(jnp.int32, (tq, tq), 0)) >> CHUNK_SHIFT
        col = lax.broadcasted_iota(jnp.int32, (tq, tq), 1)
        for j in range(ATT_B_NKB):
            blk = i - (ATT_B_NKB - 1) + j
            pk = posk_ref[jnp.maximum(blk, 0)]
            cols = slice(j * tq, (j + 1) * tq)
            ridx_ref[:, cols] = jnp.clip(pq - pk, -B_REL_CLIP, B_REL_CLIP) + B_REL_CLIP
            kchunk = (blk * tq + col) >> CHUNK_SHIFT
            valid = (kchunk >= 0) & (kchunk >= qchunk - B_PREV_CHUNKS) & (kchunk <= qchunk)
            madd_ref[:, cols] = jnp.where(valid, 0.0, -jnp.inf).astype(F32)

    qp = q_ref[...]
    lane_head = lax.broadcasted_iota(jnp.int32, (1, LANE), 1) >> CHUNK_SHIFT
    scale = B_HEAD_DIM ** -0.5
    out = jnp.zeros((tq, LANE), F32)
    for hh in range(LANE // B_HEAD_DIM):
        mine = lane_head == hh
        h = hp * (LANE // B_HEAD_DIM) + hh
        qm = jnp.where(mine, qp, jnp.zeros_like(qp))
        row = relt_ref[pl.ds(h, 1), :]
        seg0 = jnp.broadcast_to(row[:, :LANE], (tq, LANE))
        seg1 = jnp.broadcast_to(row[:, LANE:2 * LANE], (tq, LANE))
        t256 = t256_ref[h]
        parts = []
        for j in range(ATT_B_NKB):
            sj = lax.dot_general(qm, k_refs[j][...], NT_DIMS, preferred_element_type=F32)
            for c in range(tq // LANE):
                cols = slice(j * tq + c * LANE, j * tq + (c + 1) * LANE)
                r = ridx_ref[:, cols]
                lo = r & (LANE - 1)
                bias = jnp.where(r < LANE, jnp.take_along_axis(seg0, lo, axis=1),
                                 jnp.where(r < 2 * LANE, jnp.take_along_axis(seg1, lo, axis=1), t256))
                parts.append(sj[:, c * LANE:(c + 1) * LANE] * scale + (bias + madd_ref[:, cols]))
        s = jnp.concatenate(parts, axis=1)
        m = jnp.max(s, axis=1, keepdims=True)
        p = jnp.exp(s - m)
        l = jnp.sum(p, axis=1, keepdims=True)
        pb = p.astype(BF16)
        acc = jnp.zeros((tq, LANE), F32)
        for j in range(ATT_B_NKB):
            vj = v_refs[j][...]
            vm = jnp.where(mine, vj, jnp.zeros_like(vj))
            acc = acc + jnp.dot(pb[:, j * tq:(j + 1) * tq], vm, preferred_element_type=F32)
        out = out + acc / l
    o_ref[...] = out.astype(o_ref.dtype)


def band_attention(qkv, pos, rel_table):
    s = qkv.shape[0]
    tq = ATT_B_TQ
    hw = B_HEADS * B_HEAD_DIM
    npair = hw // LANE
    nrel = 2 * B_REL_CLIP + 1
    relt = jnp.zeros((B_HEADS, 3 * LANE), F32).at[:, :nrel].set(rel_table.T)
    t256 = rel_table[nrel - 1]

    def kv_spec(j, base):
        return pl.BlockSpec((tq, LANE),
                            lambda i, hp: (jnp.maximum(i - (ATT_B_NKB - 1) + j, 0), base + hp))

    return pl.pallas_call(
        _attn_b_kernel,
        name="band_attention",
        grid=(s // tq, npair),
        in_specs=[
            pl.BlockSpec(memory_space=pltpu.SMEM),
            pl.BlockSpec((tq, LANE), lambda i, hp: (i, hp)),
            kv_spec(0, npair), kv_spec(1, npair), kv_spec(2, npair),
            kv_spec(0, 2 * npair), kv_spec(1, 2 * npair), kv_spec(2, 2 * npair),
            pl.BlockSpec((tq, 1), lambda i, hp: (i, 0)),
            pl.BlockSpec((s // tq, 1, tq), lambda i, hp: (0, 0, 0)),
            pl.BlockSpec((B_HEADS, 3 * LANE), lambda i, hp: (0, 0)),
        ],
        out_specs=pl.BlockSpec((tq, LANE), lambda i, hp: (i, hp)),
        out_shape=jax.ShapeDtypeStruct((s, hw), BF16),
        scratch_shapes=[pltpu.VMEM((tq, ATT_B_NKB * tq), jnp.int32),
                        pltpu.VMEM((tq, ATT_B_NKB * tq), F32)],
        compiler_params=_params(("parallel", "arbitrary")),
    )(t256, qkv, qkv, qkv, qkv, qkv, qkv, qkv, pos.reshape(s, 1), pos.reshape(s // tq, 1, tq), relt)


ATT_C_T = 512


def _attn_c_kernel(q_ref, kn_ref, kr_ref, v_ref, o_ref, m_ref, l_ref, acc_ref):
    t = ATT_C_T
    iq = pl.program_id(1)
    m_ref[...] = jnp.full(m_ref.shape, NEG_BIG, F32)
    l_ref[...] = jnp.zeros(l_ref.shape, F32)
    acc_ref[...] = jnp.zeros(acc_ref.shape, F32)
    scale = (C_NOPE + C_ROPE) ** -0.5
    q = q_ref[...]
    rows = slice(0, t)

    def tile(kt, diag):
        k0 = pl.multiple_of(kt * t, t)
        kcat = jnp.concatenate([kn_ref[pl.ds(k0, t), :], kr_ref[pl.ds(k0, t), :]], axis=1)
        s = lax.dot_general(q, kcat, NT_DIMS, preferred_element_type=F32) * scale
        if diag:
            qc = lax.broadcasted_iota(jnp.int32, (t, t), 0) >> CHUNK_SHIFT
            kc = lax.broadcasted_iota(jnp.int32, (t, t), 1) >> CHUNK_SHIFT
            s = jnp.where(kc <= qc, s, -jnp.inf)
        p = _softmax_step(s, m_ref, l_ref, acc_ref, rows, t)
        acc_ref[...] += jnp.dot(p.astype(BF16), v_ref[pl.ds(k0, t), :], preferred_element_type=F32)

    def body(kt, carry):
        tile(kt, False)
        return carry

    lax.fori_loop(0, iq, body, 0)
    tile(iq, True)
    o_ref[...] = (acc_ref[...] / l_ref[...]).astype(o_ref.dtype)


def mla_attention(qcat, kv, kr):
    s = qcat.shape[0]
    t = min(ATT_C_T, s)
    assert t == ATT_C_T
    return pl.pallas_call(
        _attn_c_kernel,
        name="mla_attention",
        grid=(C_HEADS, s // t),
        in_specs=[
            pl.BlockSpec((t, 2 * LANE), lambda h, i: (i, h)),
            pl.BlockSpec((s, C_NOPE), lambda h, i: (0, h)),
            pl.BlockSpec((s, LANE), lambda h, i: (0, 0)),
            pl.BlockSpec((s, C_V), lambda h, i: (0, C_HEADS + h)),
        ],
        out_specs=pl.BlockSpec((t, C_V), lambda h, i: (i, h)),
        out_shape=jax.ShapeDtypeStruct((s, C_HEADS * C_V), BF16),
        scratch_shapes=[pltpu.VMEM((t, LANE), F32), pltpu.VMEM((t, LANE), F32), pltpu.VMEM((t, C_V), F32)],
        compiler_params=_params(("parallel", "arbitrary")),
    )(qcat, kv, kr, kv)


def _rope_tables(pos):
    half = ROPE_DIM // 2
    inv = ROPE_BASE ** (-jnp.arange(half, dtype=F32) * 2.0 / ROPE_DIM)
    ang = pos.astype(F32)[:, None] * inv
    cos, sin = jnp.cos(ang), jnp.sin(ang)
    s = pos.shape[0]
    z = jnp.zeros((s, half), F32)
    c = jnp.concatenate([cos, cos, jnp.ones((s, LANE - ROPE_DIM), F32)], axis=1)
    s1 = jnp.concatenate([-sin, z, z, z], axis=1)
    s2 = jnp.concatenate([z, sin, z, z], axis=1)
    return c, s1, s2


def _mixer_a(h, gain, pos, tables, w_in, w_out, t5_table):
    s = h.shape[0]
    d = h.shape[1]
    nq = A_HEADS * A_HEAD_DIM
    nkv = A_KV_HEADS * A_HEAD_DIM
    nqi = IDX_HEADS * IDX_DIM
    o_qkv = nq + 2 * nkv
    o_ki = o_qkv + nqi + IDX_DIM
    w = w_in.astype(BF16)
    w_wi = jnp.zeros((d, LANE), BF16).at[:, :IDX_HEADS].set(w[:, o_ki:o_ki + IDX_HEADS])
    qkv = norm_mm(h, 0, d, gain, w[:, :o_qkv], tn=1024, out_dtype=BF16, name="a_qkv_proj")
    n_qiki = nqi + IDX_DIM
    qiki = norm_mm(h, 0, d, gain, w[:, o_qkv:o_ki], tn=n_qiki, out_dtype=BF16,
                   rope=(True,) * (n_qiki // LANE), tables=tables, name="a_idx_proj")
    wi = norm_mm(h, 0, d, gain, w_wi, tn=LANE, out_dtype=F32, name="a_idxw_proj")
    mask = dsa_indexer(qiki, wi, topk=min(IDX_TOPK_MAX, s // 4))
    o = dsa_attention(qkv, mask, pos, t5_table)
    return mm_residual(o, w_out.astype(BF16), h)


def _mixer_b(h, gain, pos, w_in, rel_table, w_out):
    d = h.shape[1]
    qkv = norm_mm(h, 0, d, gain, w_in.astype(BF16), tn=1024, out_dtype=BF16, name="b_qkv_proj")
    o = band_attention(qkv, pos, rel_table)
    return mm_residual(o, w_out.astype(BF16), h)


def _mixer_c(h, gain, tables, w_down, g_q, g_kv, w_uq, w_ukv, w_out):
    d = h.shape[1]
    lq = g_q.shape[0]
    lkv = g_kv.shape[0]
    n_down = lq + lkv + LANE
    wd = jnp.zeros((d, n_down), BF16).at[:, :w_down.shape[1]].set(w_down.astype(BF16))
    down = norm_mm(h, 0, d, gain, wd, tn=n_down, out_dtype=F32,
                   rope=(False,) * ((lq + lkv) // LANE) + (True,), tables=tables, name="c_down_proj")
    wq = w_uq.astype(BF16).reshape(lq, C_HEADS, C_NOPE + C_ROPE)
    wq = jnp.pad(wq, ((0, 0), (0, 0), (0, 2 * LANE - C_NOPE - C_ROPE))).reshape(lq, C_HEADS * 2 * LANE)
    qcat = norm_mm(down, 0, lq, g_q, wq, tn=1024, out_dtype=BF16, rope=(False, True) * 4, tables=tables,
                   name="c_uq_proj")
    wkv = w_ukv.astype(BF16).reshape(lkv, C_HEADS, 2, C_NOPE).transpose(0, 2, 1, 3).reshape(lkv, -1)
    assert lq == lkv
    kv = norm_mm(down, 1, lkv, g_kv, wkv, tn=1024, out_dtype=BF16, name="c_ukv_proj")
    kr = down[:, lq + lkv:].astype(BF16)
    o = mla_attention(qcat, kv, kr)
    return mm_residual(o, w_out.astype(BF16), h)


def kernel(x, p, positions, t5_table, a_w_in, a_w_out, b_w_in, b_rel_table, b_w_out, c_w_down, c_q_norm,
           c_kv_norm, c_w_uq, c_w_ukv, c_w_out, attn_norm, ffn_norm, ffn_w_in, ffn_w_out, ple_norm,
           ple_w_gate, ple_w_proj, final_norm):
    assert x.shape[0] == 1
    depth = attn_norm.shape[0]
    h = x[0]
    pos = positions[0]
    tables = _rope_tables(pos)
    for i in range(depth):
        j, kind = divmod(i, 3)
        if kind == 0:
            h = _mixer_a(h, attn_norm[i], pos, tables, a_w_in[j], a_w_out[j], t5_table)
        elif kind == 1:
            h = _mixer_b(h, attn_norm[i], pos, b_w_in[j], b_rel_table[j], b_w_out[j])
        else:
            h = _mixer_c(h, attn_norm[i], tables, c_w_down[j], c_q_norm[j], c_kv_norm[j], c_w_uq[j],
                         c_w_ukv[j], c_w_out[j])
        act = ffn_in(h, ffn_norm[i], ffn_w_in[i].astype(BF16))
        h = mm_residual(act, ffn_w_out[i].astype(BF16), h)
        h = ple(h, ple_norm[i], ple_w_gate[i].astype(BF16), p[i, 0], ple_w_proj[i].astype(BF16))
    return final_rms_norm(h, final_norm)[None]
```

```python
import functools
import math

import jax
import jax.numpy as jnp
from jax import lax
from jax.experimental import pallas as pl
from jax.experimental.pallas import tpu as pltpu

LANE = 128
VMEM_LIMIT_BYTES = 56 * 1024 * 1024

CHUNK = 64
CHUNK_SHIFT = 6
EPS = 1e-6
ROPE_BASE = 10000.0
ROPE_DIM = 64
A_HEADS = 16
A_KV_HEADS = 4
A_GROUP = 4
A_HEAD_DIM = 128
IDX_HEADS = 16
IDX_DIM = 128
IDX_TOPK_MAX = 256
T5_BUCKETS = 32
T5_MAX_DISTANCE = 1024
T5_FAR = 640
B_HEADS = 32
B_HEAD_DIM = 64
B_PREV_CHUNKS = 8
B_REL_CLIP = 128
C_HEADS = 16
C_NOPE = 128
C_ROPE = 64
C_V = 128
NEG_BIG = -1e30
INT_MIN = -(2 ** 31)
LOG2E = math.log2(math.e)

F32 = jnp.float32
BF16 = jnp.bfloat16
NT_DIMS = (((1,), (1,)), ((), ()))


def _params(sem):
    return pltpu.CompilerParams(dimension_semantics=sem, vmem_limit_bytes=VMEM_LIMIT_BYTES)


def _rms(x, g):
    ms = jnp.mean(x * x, axis=-1, keepdims=True)
    return (x * lax.rsqrt(ms + EPS)) * g


def _lane_gather(table, idx):
    return jnp.take_along_axis(table, idx, axis=1, mode="promise_in_bounds")


def _rope_group(y, c, s1, s2):
    return y * c + pltpu.roll(y, 96, 1) * s1 + pltpu.roll(y, 32, 1) * s2


def _norm_mm_kernel(*refs, rope, scaled):
    x_ref, g_ref, w_ref = refs[:3]
    rest = list(refs[3:])
    cs_ref = rest.pop(0) if scaled else None
    c_ref, s1_ref, s2_ref = (rest.pop(0), rest.pop(0), rest.pop(0)) if rope is not None else (None,) * 3
    o_ref, xn_ref = rest

    @pl.when(pl.program_id(1) == 0)
    def _():
        xn_ref[...] = _rms(x_ref[...], g_ref[...]).astype(BF16)

    y = jnp.dot(xn_ref[...], w_ref[...], preferred_element_type=F32)
    if scaled:
        y = y * cs_ref[...]
    if rope is None:
        o_ref[...] = y.astype(o_ref.dtype)
    else:
        c, s1, s2 = c_ref[...], s1_ref[...], s2_ref[...]
        for gi, on in enumerate(rope):
            sl = slice(gi * LANE, (gi + 1) * LANE)
            yg = y[:, sl]
            if on:
                yg = _rope_group(yg, c, s1, s2)
            o_ref[:, sl] = yg.astype(o_ref.dtype)


def norm_mm(x, x_col, kx, gain, w, *, tn, out_dtype, name, tm=512, rope=None, tables=None, col_scale=None):
    s = x.shape[0]
    n = w.shape[1]
    tm = min(tm, s)
    in_specs = [
        pl.BlockSpec((tm, kx), lambda i, j: (i, x_col)),
        pl.BlockSpec((1, kx), lambda i, j: (0, 0)),
        pl.BlockSpec((kx, tn), lambda i, j: (0, j)),
    ]
    args = [x, gain.reshape(1, kx), w]
    if col_scale is not None:
        in_specs.append(pl.BlockSpec((1, tn), lambda i, j: (0, j)))
        args.append(col_scale.reshape(1, n))
    if rope is not None:
        assert len(rope) == tn // LANE
        in_specs += [pl.BlockSpec((tm, LANE), lambda i, j: (i, 0))] * 3
        args += list(tables)
    return pl.pallas_call(
        functools.partial(_norm_mm_kernel, rope=rope, scaled=col_scale is not None),
        grid=(s // tm, n // tn),
        in_specs=in_specs,
        out_specs=pl.BlockSpec((tm, tn), lambda i, j: (i, j)),
        out_shape=jax.ShapeDtypeStruct((s, n), out_dtype),
        scratch_shapes=[pltpu.VMEM((tm, kx), BF16)],
        compiler_params=_params(("parallel", "arbitrary")),
        name=name,
    )(*args)


def _mm_res_kernel(x_ref, w_ref, r_ref, o_ref):
    o_ref[...] = r_ref[...] + jnp.dot(x_ref[...], w_ref[...], preferred_element_type=F32)


def mm_residual(x, w, res, *, tm=512, tn=512):
    s, k = x.shape
    n = w.shape[1]
    tm = min(tm, s)
    return pl.pallas_call(
        _mm_res_kernel,
        name="mm_residual",
        grid=(s // tm, n // tn),
        in_specs=[
            pl.BlockSpec((tm, k), lambda i, j: (i, 0)),
            pl.BlockSpec((k, tn), lambda i, j: (0, j)),
            pl.BlockSpec((tm, tn), lambda i, j: (i, j)),
        ],
        out_specs=pl.BlockSpec((tm, tn), lambda i, j: (i, j)),
        out_shape=jax.ShapeDtypeStruct((s, n), F32),
        compiler_params=_params(("parallel", "arbitrary")),
    )(x, w, res)


def _ffn_in_kernel(x_ref, g_ref, wg_ref, wu_ref, o_ref, xn_ref):
    @pl.when(pl.program_id(1) == 0)
    def _():
        xn_ref[...] = _rms(x_ref[...], g_ref[...]).astype(BF16)

    xn = xn_ref[...]
    a = jnp.dot(xn, wg_ref[...], preferred_element_type=F32)
    u = jnp.dot(xn, wu_ref[...], preferred_element_type=F32)
    o_ref[...] = (a * jax.nn.sigmoid(a) * u).astype(o_ref.dtype)


def ffn_in(h, gain, w_in, *, tm=1024, tn=512):
    s, d = h.shape
    f = w_in.shape[1] // 2
    tm = min(tm, s)
    nj = f // tn
    return pl.pallas_call(
        _ffn_in_kernel,
        name="ffn_in",
        grid=(s // tm, nj),
        in_specs=[
            pl.BlockSpec((tm, d), lambda i, j: (i, 0)),
            pl.BlockSpec((1, d), lambda i, j: (0, 0)),
            pl.BlockSpec((d, tn), lambda i, j: (0, j)),
            pl.BlockSpec((d, tn), lambda i, j: (0, j + nj)),
        ],
        out_specs=pl.BlockSpec((tm, tn), lambda i, j: (i, j)),
        out_shape=jax.ShapeDtypeStruct((s, f), BF16),
        scratch_shapes=[pltpu.VMEM((tm, d), BF16)],
        compiler_params=_params(("parallel", "arbitrary")),
    )(h, gain.reshape(1, d), w_in, w_in)


def _ple_kernel(x_ref, g_ref, wg_ref, p_ref, wp_ref, h_ref, o_ref, xn_ref, pb_ref):
    @pl.when(pl.program_id(1) == 0)
    def _():
        xn_ref[...] = _rms(x_ref[...], g_ref[...]).astype(BF16)
        pb_ref[...] = p_ref[...].astype(BF16)

    gate = jax.nn.sigmoid(jnp.dot(xn_ref[...], wg_ref[...], preferred_element_type=F32))
    proj = jnp.dot(pb_ref[...], wp_ref[...], preferred_element_type=F32)
    o_ref[...] = h_ref[...] + gate * proj


def ple(h, gain, w_gate, p, w_proj, *, tm=512, tn=1024):
    s, d = h.shape
    pd = p.shape[1]
    tm = min(tm, s)
    return pl.pallas_call(
        _ple_kernel,
        name="ple",
        grid=(s // tm, d // tn),
        in_specs=[
            pl.BlockSpec((tm, d), lambda i, j: (i, 0)),
            pl.BlockSpec((1, d), lambda i, j: (0, 0)),
            pl.BlockSpec((d, tn), lambda i, j: (0, j)),
            pl.BlockSpec((tm, pd), lambda i, j: (i, 0)),
            pl.BlockSpec((pd, tn), lambda i, j: (0, j)),
            pl.BlockSpec((tm, tn), lambda i, j: (i, j)),
        ],
        out_specs=pl.BlockSpec((tm, tn), lambda i, j: (i, j)),
        out_shape=jax.ShapeDtypeStruct((s, d), F32),
        scratch_shapes=[pltpu.VMEM((tm, d), BF16), pltpu.VMEM((tm, pd), BF16)],
        compiler_params=_params(("parallel", "arbitrary")),
    )(h, gain.reshape(1, d), w_gate, p, w_proj, h)


def _final_norm_kernel(x_ref, g_ref, o_ref):
    o_ref[...] = _rms(x_ref[...], g_ref[...])


def final_rms_norm(h, gain, *, tm=512):
    s, d = h.shape
    tm = min(tm, s)
    return pl.pallas_call(
        _final_norm_kernel,
        name="final_norm",
        grid=(s // tm,),
        in_specs=[pl.BlockSpec((tm, d), lambda i: (i, 0)), pl.BlockSpec((1, d), lambda i: (0, 0))],
        out_specs=pl.BlockSpec((tm, d), lambda i: (i, 0)),
        out_shape=jax.ShapeDtypeStruct((s, d), F32),
        compiler_params=_params(("parallel",)),
    )(h, gain.reshape(1, d))


IDX_TQ = 256
IDX_KC = 256


def _orderable(x):
    b = pltpu.bitcast(x, jnp.int32)
    return jnp.where(b < 0, b ^ jnp.int32(0x7FFFFFFF), b)


def _indexer_kernel(qi_ref, ki_ref, wi_ref, mask_ref, keys_ref, *, topk, n_kblocks):
    tq, kc = IDX_TQ, IDX_KC
    i = pl.program_id(0)
    nkc = (i + 1) * (tq // kc)
    w_t = (wi_ref[...] * (IDX_HEADS ** -0.5 * IDX_DIM ** -0.5)).T
    qchunk = (i * tq + lax.broadcasted_iota(jnp.int32, (kc, tq), 1)) >> CHUNK_SHIFT
    krow = lax.broadcasted_iota(jnp.int32, (kc, tq), 0)

    def valid_of(c):
        return ((c * kc + krow) >> CHUNK_SHIFT) <= qchunk

    def score_body(c, carry):
        k0 = pl.multiple_of(c * kc, kc)
        kblk = ki_ref[pl.ds(k0, kc), :]
        acc = jnp.zeros((kc, tq), F32)
        for h in range(IDX_HEADS):
            s = lax.dot_general(kblk, qi_ref[:, h * IDX_DIM:(h + 1) * IDX_DIM], NT_DIMS,
                                preferred_element_type=F32)
            acc = acc + jnp.maximum(s, 0.0) * w_t[h:h + 1, :]
        keys_ref[pl.ds(k0, kc), :] = jnp.where(valid_of(c), _orderable(acc), jnp.int32(INT_MIN))
        return carry

    lax.fori_loop(0, nkc, score_body, 0)

    def count_ge(cand):
        def body(c, cnt):
            k0 = pl.multiple_of(c * kc, kc)
            ge = (keys_ref[pl.ds(k0, kc), :] >= cand).astype(jnp.int32)
            return cnt + ge.reshape(kc // 8, 8, tq).sum(axis=0)
        cnt = lax.fori_loop(0, nkc, body, jnp.zeros((8, tq), jnp.int32))
        return cnt.sum(axis=0, keepdims=True)

    def bit_body(b, t):
        cand = t + lax.shift_left(jnp.int32(1), 31 - b)
        return jnp.where(count_ge(cand) >= topk, cand, t)

    thr = lax.fori_loop(0, 32, bit_body, jnp.full((1, tq), INT_MIN, jnp.int32))

    def mask_body(c, carry):
        k0 = pl.multiple_of(c * kc, kc)
        sel = (keys_ref[pl.ds(k0, kc), :] >= thr) & valid_of(c)
        mask_ref[c] = jnp.where(sel, 0.0, -jnp.inf).astype(F32).T
        return carry

    lax.fori_loop(0, nkc, mask_body, 0)

    def fill_body(c, carry):
        mask_ref[c] = jnp.full((tq, kc), -jnp.inf, F32)
        return carry

    lax.fori_loop(nkc, n_kblocks, fill_body, 0)


def dsa_indexer(qiki, wi, *, topk):
    s = qiki.shape[0]
    tq, kc = IDX_TQ, IDX_KC
    nkb = s // kc
    return pl.pallas_call(
        functools.partial(_indexer_kernel, topk=topk, n_kblocks=nkb),
        name="dsa_indexer",
        grid=(s // tq,),
        in_specs=[
            pl.BlockSpec((tq, IDX_HEADS * IDX_DIM), lambda i: (i, 0)),
            pl.BlockSpec((s, IDX_DIM), lambda i: (0, IDX_HEADS)),
            pl.BlockSpec((tq, LANE), lambda i: (i, 0)),
        ],
        out_specs=pl.BlockSpec((nkb, tq, kc), lambda i: (0, i, 0)),
        out_shape=jax.ShapeDtypeStruct((nkb, s, kc), F32),
        scratch_shapes=[pltpu.VMEM((s, tq), jnp.int32)],
        compiler_params=_params(("parallel",)),
    )(qiki, qiki, wi)


ATT_A_TQ = 256
ATT_A_KB = 2


def _t5_bucket(rel):
    nb = T5_BUCKETS // 2
    max_exact = nb // 2
    ret = jnp.where(rel > 0, nb, 0)
    n = jnp.abs(rel)
    nf = jnp.maximum(n, 1).astype(F32)
    large = max_exact + (jnp.log(nf / max_exact) / math.log(T5_MAX_DISTANCE / max_exact)
                         * (nb - max_exact)).astype(jnp.int32)
    large = jnp.minimum(large, nb - 1)
    return ret + jnp.where(n < max_exact, n, large)


def _softmax_step(s, v, m_ref, l_ref, acc_ref, rows, tk, shift=None):
    d = v.shape[1]
    m_prev = m_ref[rows, :]
    mx = jnp.max(s, axis=1, keepdims=True)
    if shift is not None:
        mx = mx + shift
    m_next = jnp.maximum(m_prev, mx)
    pivot = m_next if shift is None else m_next - shift
    p = jnp.exp2(s - pltpu.repeat(pivot, tk // LANE, axis=1)).astype(BF16)
    alpha = jnp.exp2(m_prev - m_next)
    v_ones = jnp.concatenate([v, jnp.ones((tk, LANE), BF16)], axis=1)
    pv = jnp.dot(p, v_ones, preferred_element_type=F32)
    m_ref[rows, :] = m_next
    l_ref[rows, :] = alpha * l_ref[rows, :] + pv[:, d:]
    acc_ref[rows, :] = acc_ref[rows, :] * alpha + pv[:, :d]


def _attn_a_kernel(qmin_ref, kmax_ref, t5s_ref, q_ref, k_ref, v_ref, mask_ref, posq_ref, posk_ref,
                   t5t_ref, o_ref, m_ref, l_ref, acc_ref):
    tq, kc, nb = ATT_A_TQ, IDX_KC, ATT_A_KB
    tk = nb * kc
    i = pl.program_id(0)
    g = pl.program_id(1)
    m_ref[...] = jnp.full(m_ref.shape, NEG_BIG, F32)
    l_ref[...] = jnp.zeros(l_ref.shape, F32)
    acc_ref[...] = jnp.zeros(acc_ref.shape, F32)
    nkt = ((i + 1) * (tq // kc) + nb - 1) // nb
    qmin = qmin_ref[i]

    def is_far(kt):
        ktc = jnp.minimum(kt, nkt - 1) * nb
        kmax = kmax_ref[ktc]
        for b in range(1, nb):
            kmax = jnp.maximum(kmax, kmax_ref[ktc + b])
        return (kt < nkt) & ((qmin - kmax) >= T5_FAR)

    n_far = lax.while_loop(is_far, lambda kt: kt + 1, jnp.int32(0))

    def tile(kt, bias_of, shift_of):
        k0 = pl.multiple_of(kt * tk, tk)
        k_t = k_ref[pl.ds(k0, tk), :]
        v_t = v_ref[pl.ds(k0, tk), :]
        base = jnp.concatenate([mask_ref[kt * nb + b] for b in range(nb)], axis=1)
        for r in range(A_GROUP):
            rows = slice(r * tq, (r + 1) * tq)
            q_r = q_ref[:, r * A_HEAD_DIM:(r + 1) * A_HEAD_DIM]
            s = lax.dot_general(q_r, k_t, NT_DIMS, preferred_element_type=F32) + bias_of(r, base)
            _softmax_step(s, v_t, m_ref, l_ref, acc_ref, rows, tk, shift_of(r))

    def far_body(kt, carry):
        tile(kt, lambda r, base: base,
             lambda r: t5s_ref[T5_BUCKETS // 2 - 1, g * A_GROUP + r] * LOG2E)
        return carry

    def near_body(kt, carry):
        pk = jnp.concatenate([posk_ref[kt * nb + b] for b in range(nb)], axis=1)
        bucket = _t5_bucket(pk - posq_ref[...])

        def bias_of(r, base):
            row = t5t_ref[pl.ds(g * A_GROUP + r, 1), :] * LOG2E
            tbl = jnp.broadcast_to(row, (tq, LANE))
            return base + jnp.concatenate(
                [_lane_gather(tbl, bucket[:, c * LANE:(c + 1) * LANE])
                 for c in range(tk // LANE)], axis=1)

        tile(kt, bias_of, lambda r: None)
        return carry

    lax.fori_loop(0, n_far, far_body, 0)
    lax.fori_loop(n_far, nkt, near_body, 0)
    for r in range(A_GROUP):
        rows = slice(r * tq, (r + 1) * tq)
        o_ref[:, r * A_HEAD_DIM:(r + 1) * A_HEAD_DIM] = (acc_ref[rows, :] / l_ref[rows, :]).astype(o_ref.dtype)


def dsa_attention(qkv, mask, pos, t5_table):
    s = qkv.shape[0]
    tq, tk = ATT_A_TQ, IDX_KC
    assert (s // tk) % ATT_A_KB == 0
    gw = A_GROUP * A_HEAD_DIM
    qmin = pos.reshape(s // tq, tq).min(axis=1)
    kmax = pos.reshape(s // tk, tk).max(axis=1)
    t5t = jnp.zeros((A_HEADS, LANE), F32).at[:, :T5_BUCKETS].set(t5_table.T)
    kblk0 = (A_HEADS * A_HEAD_DIM) // A_HEAD_DIM
    smem = pl.BlockSpec(memory_space=pltpu.SMEM)
    return pl.pallas_call(
        _attn_a_kernel,
        name="dsa_attention",
        grid=(s // tq, A_KV_HEADS),
        in_specs=[
            smem, smem, smem,
            pl.BlockSpec((tq, gw), lambda i, g: (i, g)),
            pl.BlockSpec((s, A_HEAD_DIM), lambda i, g: (0, kblk0 + g)),
            pl.BlockSpec((s, A_HEAD_DIM), lambda i, g: (0, kblk0 + A_KV_HEADS + g)),
            pl.BlockSpec((s // tk, tq, tk), lambda i, g: (0, i, 0)),
            pl.BlockSpec((tq, 1), lambda i, g: (i, 0)),
            pl.BlockSpec((s // tk, 1, tk), lambda i, g: (0, 0, 0)),
            pl.BlockSpec((A_HEADS, LANE), lambda i, g: (0, 0)),
        ],
        out_specs=pl.BlockSpec((tq, gw), lambda i, g: (i, g)),
        out_shape=jax.ShapeDtypeStruct((s, A_HEADS * A_HEAD_DIM), BF16),
        scratch_shapes=[
            pltpu.VMEM((A_GROUP * tq, LANE), F32),
            pltpu.VMEM((A_GROUP * tq, LANE), F32),
            pltpu.VMEM((A_GROUP * tq, A_HEAD_DIM), F32),
        ],
        compiler_params=_params(("parallel", "arbitrary")),
    )(qmin, kmax, t5_table, qkv, qkv, qkv, mask, pos.reshape(s, 1), pos.reshape(s // tk, 1, tk), t5t)


ATT_B_TQ = 256
ATT_B_NKB = 3


def _attn_b_kernel(t256_ref, q_ref, k0_ref, k1_ref, k2_ref, v0_ref, v1_ref, v2_ref, posq_ref, posk_ref,
                   relt_ref, o_ref, ridx_ref, madd_ref):
    tq = ATT_B_TQ
    i = pl.program_id(0)
    hp = pl.program_id(1)
    k_refs = (k0_ref, k1_ref, k2_ref)
    v_refs = (v0_ref, v1_ref, v2_ref)

    @pl.when(hp == 0)
    def _():
        pq = posq_ref[...]
        qchunk = (i * tq + lax.broadcasted_iota(jnp.int32, (tq, tq), 0)) >> CHUNK_SHIFT
        col = lax.broadcasted_iota(jnp.int32, (tq, tq), 1)
        for j in range(ATT_B_NKB):
            blk = i - (ATT_B_NKB - 1) + j
            pk = posk_ref[jnp.maximum(blk, 0)]
            cols = slice(j * tq, (j + 1) * tq)
            ridx_ref[:, cols] = jnp.clip(pq - pk, -B_REL_CLIP, B_REL_CLIP) + B_REL_CLIP
            kchunk = (blk * tq + col) >> CHUNK_SHIFT
            valid = (kchunk >= 0) & (kchunk >= qchunk - B_PREV_CHUNKS) & (kchunk <= qchunk)
            madd_ref[:, cols] = jnp.where(valid, 0.0, -jnp.inf).astype(F32)

    qp = q_ref[...]
    lane_head = lax.broadcasted_iota(jnp.int32, (1, LANE), 1) >> CHUNK_SHIFT
    out = jnp.zeros((tq, LANE), F32)
    for hh in range(LANE // B_HEAD_DIM):
        mine = lane_head == hh
        h = hp * (LANE // B_HEAD_DIM) + hh
        qm = jnp.where(mine, qp, jnp.zeros_like(qp))
        row = relt_ref[pl.ds(h, 1), :] * LOG2E
        seg0 = jnp.broadcast_to(row[:, :LANE], (tq, LANE))
        seg1 = jnp.broadcast_to(row[:, LANE:2 * LANE], (tq, LANE))
        t256 = t256_ref[h] * LOG2E
        parts = []
        for j in range(ATT_B_NKB):
            sj = lax.dot_general(qm, k_refs[j][...], NT_DIMS, preferred_element_type=F32)
            for c in range(tq // LANE):
                cols = slice(j * tq + c * LANE, j * tq + (c + 1) * LANE)
                r = ridx_ref[:, cols]
                lo = r & (LANE - 1)
                bias = jnp.where(r < LANE, _lane_gather(seg0, lo),
                                 jnp.where(r < 2 * LANE, _lane_gather(seg1, lo), t256))
                parts.append(sj[:, c * LANE:(c + 1) * LANE] + (bias + madd_ref[:, cols]))
        s = jnp.concatenate(parts, axis=1)
        m = jnp.max(s, axis=1, keepdims=True)
        p = jnp.exp2(s - m)
        l = jnp.sum(p, axis=1, keepdims=True)
        pb = p.astype(BF16)
        acc = jnp.zeros((tq, LANE), F32)
        for j in range(ATT_B_NKB):
            vj = v_refs[j][...]
            vm = jnp.where(mine, vj, jnp.zeros_like(vj))
            acc = acc + jnp.dot(pb[:, j * tq:(j + 1) * tq], vm, preferred_element_type=F32)
        out = out + acc / l
    o_ref[...] = out.astype(o_ref.dtype)


def band_attention(qkv, pos, rel_table):
    s = qkv.shape[0]
    tq = ATT_B_TQ
    hw = B_HEADS * B_HEAD_DIM
    npair = hw // LANE
    nrel = 2 * B_REL_CLIP + 1
    relt = jnp.zeros((B_HEADS, 3 * LANE), F32).at[:, :nrel].set(rel_table.T)
    t256 = rel_table[nrel - 1]

    def kv_spec(j, base):
        return pl.BlockSpec((tq, LANE),
                            lambda i, hp: (jnp.maximum(i - (ATT_B_NKB - 1) + j, 0), base + hp))

    return pl.pallas_call(
        _attn_b_kernel,
        name="band_attention",
        grid=(s // tq, npair),
        in_specs=[
            pl.BlockSpec(memory_space=pltpu.SMEM),
            pl.BlockSpec((tq, LANE), lambda i, hp: (i, hp)),
            kv_spec(0, npair), kv_spec(1, npair), kv_spec(2, npair),
            kv_spec(0, 2 * npair), kv_spec(1, 2 * npair), kv_spec(2, 2 * npair),
            pl.BlockSpec((tq, 1), lambda i, hp: (i, 0)),
            pl.BlockSpec((s // tq, 1, tq), lambda i, hp: (0, 0, 0)),
            pl.BlockSpec((B_HEADS, 3 * LANE), lambda i, hp: (0, 0)),
        ],
        out_specs=pl.BlockSpec((tq, LANE), lambda i, hp: (i, hp)),
        out_shape=jax.ShapeDtypeStruct((s, hw), BF16),
        scratch_shapes=[pltpu.VMEM((tq, ATT_B_NKB * tq), jnp.int32),
                        pltpu.VMEM((tq, ATT_B_NKB * tq), F32)],
        compiler_params=_params(("parallel", "arbitrary")),
    )(t256, qkv, qkv, qkv, qkv, qkv, qkv, qkv, pos.reshape(s, 1), pos.reshape(s // tq, 1, tq), relt)


ATT_C_T = 512


def _attn_c_kernel(q_ref, kn_ref, kr_ref, v_ref, o_ref, m_ref, l_ref, acc_ref):
    t = ATT_C_T
    iq = pl.program_id(1)
    m_ref[...] = jnp.full(m_ref.shape, NEG_BIG, F32)
    l_ref[...] = jnp.zeros(l_ref.shape, F32)
    acc_ref[...] = jnp.zeros(acc_ref.shape, F32)
    q = q_ref[...]
    rows = slice(0, t)

    def tile(kt, diag):
        k0 = pl.multiple_of(kt * t, t)
        kcat = jnp.concatenate([kn_ref[pl.ds(k0, t), :], kr_ref[pl.ds(k0, t), :]], axis=1)
        s = lax.dot_general(q, kcat, NT_DIMS, preferred_element_type=F32)
        if diag:
            qc = lax.broadcasted_iota(jnp.int32, (t, t), 0) >> CHUNK_SHIFT
            kc = lax.broadcasted_iota(jnp.int32, (t, t), 1) >> CHUNK_SHIFT
            s = jnp.where(kc <= qc, s, -jnp.inf)
        _softmax_step(s, v_ref[pl.ds(k0, t), :], m_ref, l_ref, acc_ref, rows, t)

    def body(kt, carry):
        tile(kt, False)
        return carry

    lax.fori_loop(0, iq, body, 0)
    tile(iq, True)
    o_ref[...] = (acc_ref[...] / l_ref[...]).astype(o_ref.dtype)


def mla_attention(qcat, kv, kr):
    s = qcat.shape[0]
    t = min(ATT_C_T, s)
    assert t == ATT_C_T
    return pl.pallas_call(
        _attn_c_kernel,
        name="mla_attention",
        grid=(C_HEADS, s // t),
        in_specs=[
            pl.BlockSpec((t, 2 * LANE), lambda h, i: (i, h)),
            pl.BlockSpec((s, C_NOPE), lambda h, i: (0, h)),
            pl.BlockSpec((s, LANE), lambda h, i: (0, 0)),
            pl.BlockSpec((s, C_V), lambda h, i: (0, C_HEADS + h)),
        ],
        out_specs=pl.BlockSpec((t, C_V), lambda h, i: (i, h)),
        out_shape=jax.ShapeDtypeStruct((s, C_HEADS * C_V), BF16),
        scratch_shapes=[pltpu.VMEM((t, LANE), F32), pltpu.VMEM((t, LANE), F32), pltpu.VMEM((t, C_V), F32)],
        compiler_params=_params(("parallel", "arbitrary")),
    )(qcat, kv, kr, kv)


def _rope_tables(pos):
    half = ROPE_DIM // 2
    inv = ROPE_BASE ** (-jnp.arange(half, dtype=F32) * 2.0 / ROPE_DIM)
    ang = pos.astype(F32)[:, None] * inv
    cos, sin = jnp.cos(ang), jnp.sin(ang)
    s = pos.shape[0]
    z = jnp.zeros((s, half), F32)
    c = jnp.concatenate([cos, cos, jnp.ones((s, LANE - ROPE_DIM), F32)], axis=1)
    s1 = jnp.concatenate([-sin, z, z, z], axis=1)
    s2 = jnp.concatenate([z, sin, z, z], axis=1)
    return c, s1, s2


def _query_scale(n_query, n_total, scale):
    return jnp.concatenate([jnp.full((n_query,), scale * LOG2E, F32), jnp.ones((n_total - n_query,), F32)])


def _mixer_a(h, gain, pos, tables, w_in, w_out, t5_table):
    s = h.shape[0]
    d = h.shape[1]
    nq = A_HEADS * A_HEAD_DIM
    nkv = A_KV_HEADS * A_HEAD_DIM
    nqi = IDX_HEADS * IDX_DIM
    o_qkv = nq + 2 * nkv
    o_ki = o_qkv + nqi + IDX_DIM
    w = w_in.astype(BF16)
    w_wi = jnp.zeros((d, LANE), BF16).at[:, :IDX_HEADS].set(w[:, o_ki:o_ki + IDX_HEADS])
    qkv = norm_mm(h, 0, d, gain, w[:, :o_qkv], tn=1024, out_dtype=BF16, name="a_qkv_proj",
                  col_scale=_query_scale(nq, o_qkv, A_HEAD_DIM ** -0.5))
    n_qiki = nqi + IDX_DIM
    qiki = norm_mm(h, 0, d, gain, w[:, o_qkv:o_ki], tn=n_qiki, out_dtype=BF16,
                   rope=(True,) * (n_qiki // LANE), tables=tables, name="a_idx_proj")
    wi = norm_mm(h, 0, d, gain, w_wi, tn=LANE, out_dtype=F32, name="a_idxw_proj")
    mask = dsa_indexer(qiki, wi, topk=min(IDX_TOPK_MAX, s // 4))
    o = dsa_attention(qkv, mask, pos, t5_table)
    return mm_residual(o, w_out.astype(BF16), h)


def _mixer_b(h, gain, pos, w_in, rel_table, w_out):
    d = h.shape[1]
    n = w_in.shape[1]
    qkv = norm_mm(h, 0, d, gain, w_in.astype(BF16), tn=1024, out_dtype=BF16, name="b_qkv_proj",
                  col_scale=_query_scale(n // 3, n, B_HEAD_DIM ** -0.5))
    o = band_attention(qkv, pos, rel_table)
    return mm_residual(o, w_out.astype(BF16), h)


def _mixer_c(h, gain, tables, w_down, g_q, g_kv, w_uq, w_ukv, w_out):
    d = h.shape[1]
    lq = g_q.shape[0]
    lkv = g_kv.shape[0]
    n_down = lq + lkv + LANE
    wd = jnp.zeros((d, n_down), BF16).at[:, :w_down.shape[1]].set(w_down.astype(BF16))
    down = norm_mm(h, 0, d, gain, wd, tn=n_down, out_dtype=F32,
                   rope=(False,) * ((lq + lkv) // LANE) + (True,), tables=tables, name="c_down_proj")
    wq = w_uq.astype(BF16).reshape(lq, C_HEADS, C_NOPE + C_ROPE)
    wq = jnp.pad(wq, ((0, 0), (0, 0), (0, 2 * LANE - C_NOPE - C_ROPE))).reshape(lq, C_HEADS * 2 * LANE)
    nqc = wq.shape[1]
    qcat = norm_mm(down, 0, lq, g_q, wq, tn=1024, out_dtype=BF16, rope=(False, True) * 4, tables=tables,
                   name="c_uq_proj", col_scale=_query_scale(nqc, nqc, (C_NOPE + C_ROPE) ** -0.5))
    wkv = w_ukv.astype(BF16).reshape(lkv, C_HEADS, 2, C_NOPE).transpose(0, 2, 1, 3).reshape(lkv, -1)
    assert lq == lkv
    kv = norm_mm(down, 1, lkv, g_kv, wkv, tn=1024, out_dtype=BF16, name="c_ukv_proj")
    kr = down[:, lq + lkv:].astype(BF16)
    o = mla_attention(qcat, kv, kr)
    return mm_residual(o, w_out.astype(BF16), h)


def kernel(x, p, positions, t5_table, a_w_in, a_w_out, b_w_in, b_rel_table, b_w_out, c_w_down, c_q_norm,
           c_kv_norm, c_w_uq, c_w_ukv, c_w_out, attn_norm, ffn_norm, ffn_w_in, ffn_w_out, ple_norm,
           ple_w_gate, ple_w_proj, final_norm):
    assert x.shape[0] == 1
    depth = attn_norm.shape[0]
    h = x[0]
    pos = positions[0]
    tables = _rope_tables(pos)
    for i in range(depth):
        j, kind = divmod(i, 3)
        if kind == 0:
            h = _mixer_a(h, attn_norm[i], pos, tables, a_w_in[j], a_w_out[j], t5_table)
        elif kind == 1:
            h = _mixer_b(h, attn_norm[i], pos, b_w_in[j], b_rel_table[j], b_w_out[j])
        else:
            h = _mixer_c(h, attn_norm[i], tables, c_w_down[j], c_q_norm[j], c_kv_norm[j], c_w_uq[j],
                         c_w_ukv[j], c_w_out[j])
        act = ffn_in(h, ffn_norm[i], ffn_w_in[i].astype(BF16))
        h = mm_residual(act, ffn_w_out[i].astype(BF16), h)
        h = ple(h, ple_norm[i], ple_w_gate[i].astype(BF16), p[i, 0], ple_w_proj[i].astype(BF16))
    return final_rms_norm(h, final_norm)[None]
```

```python
import functools
import math

import jax
import jax.numpy as jnp
from jax import lax
from jax.experimental import pallas as pl
from jax.experimental.pallas import tpu as pltpu

LANE = 128
VMEM_LIMIT_BYTES = 56 * 1024 * 1024

CHUNK = 64
CHUNK_SHIFT = 6
EPS = 1e-6
ROPE_BASE = 10000.0
ROPE_DIM = 64
A_HEADS = 16
A_KV_HEADS = 4
A_GROUP = 4
A_HEAD_DIM = 128
IDX_HEADS = 16
IDX_DIM = 128
IDX_TOPK_MAX = 256
T5_BUCKETS = 32
T5_MAX_DISTANCE = 1024
T5_FAR = 640
B_HEADS = 32
B_HEAD_DIM = 64
B_PREV_CHUNKS = 8
B_REL_CLIP = 128
C_HEADS = 16
C_NOPE = 128
C_ROPE = 64
C_V = 128
NEG_BIG = -1e30
INT_MIN = -(2 ** 31)
LOG2E = math.log2(math.e)

F32 = jnp.float32
BF16 = jnp.bfloat16
NT_DIMS = (((1,), (1,)), ((), ()))


def _params(sem):
    return pltpu.CompilerParams(dimension_semantics=sem, vmem_limit_bytes=VMEM_LIMIT_BYTES)


def _rms(x, g):
    ms = jnp.mean(x * x, axis=-1, keepdims=True)
    return (x * lax.rsqrt(ms + EPS)) * g


def _lane_gather(table, idx):
    return jnp.take_along_axis(table, idx, axis=1, mode="promise_in_bounds")


def _rope_group(y, c, s1, s2):
    return y * c + pltpu.roll(y, 96, 1) * s1 + pltpu.roll(y, 32, 1) * s2


def _norm_mm_kernel(*refs, rope, scaled):
    x_ref, g_ref, w_ref = refs[:3]
    rest = list(refs[3:])
    cs_ref = rest.pop(0) if scaled else None
    c_ref, s1_ref, s2_ref = (rest.pop(0), rest.pop(0), rest.pop(0)) if rope is not None else (None,) * 3
    o_ref, xn_ref = rest

    @pl.when(pl.program_id(1) == 0)
    def _():
        xn_ref[...] = _rms(x_ref[...], g_ref[...]).astype(BF16)

    y = jnp.dot(xn_ref[...], w_ref[...], preferred_element_type=F32)
    if scaled:
        y = y * cs_ref[...]
    if rope is None:
        o_ref[...] = y.astype(o_ref.dtype)
    else:
        c, s1, s2 = c_ref[...], s1_ref[...], s2_ref[...]
        for gi, on in enumerate(rope):
            sl = slice(gi * LANE, (gi + 1) * LANE)
            yg = y[:, sl]
            if on:
                yg = _rope_group(yg, c, s1, s2)
            o_ref[:, sl] = yg.astype(o_ref.dtype)


def norm_mm(x, x_col, kx, gain, w, *, tn, out_dtype, name, tm=512, rope=None, tables=None, col_scale=None):
    s = x.shape[0]
    n = w.shape[1]
    tm = min(tm, s)
    in_specs = [
        pl.BlockSpec((tm, kx), lambda i, j: (i, x_col)),
        pl.BlockSpec((1, kx), lambda i, j: (0, 0)),
        pl.BlockSpec((kx, tn), lambda i, j: (0, j)),
    ]
    args = [x, gain.reshape(1, kx), w]
    if col_scale is not None:
        in_specs.append(pl.BlockSpec((1, tn), lambda i, j: (0, j)))
        args.append(col_scale.reshape(1, n))
    if rope is not None:
        assert len(rope) == tn // LANE
        in_specs += [pl.BlockSpec((tm, LANE), lambda i, j: (i, 0))] * 3
        args += list(tables)
    return pl.pallas_call(
        functools.partial(_norm_mm_kernel, rope=rope, scaled=col_scale is not None),
        grid=(s // tm, n // tn),
        in_specs=in_specs,
        out_specs=pl.BlockSpec((tm, tn), lambda i, j: (i, j)),
        out_shape=jax.ShapeDtypeStruct((s, n), out_dtype),
        scratch_shapes=[pltpu.VMEM((tm, kx), BF16)],
        compiler_params=_params(("parallel", "arbitrary")),
        name=name,
    )(*args)


def _mm_res_kernel(x_ref, w_ref, r_ref, o_ref):
    o_ref[...] = r_ref[...] + jnp.dot(x_ref[...], w_ref[...], preferred_element_type=F32)


def mm_residual(x, w, res, *, tm=512, tn=512):
    s, k = x.shape
    n = w.shape[1]
    tm = min(tm, s)
    return pl.pallas_call(
        _mm_res_kernel,
        name="mm_residual",
        grid=(s // tm, n // tn),
        in_specs=[
            pl.BlockSpec((tm, k), lambda i, j: (i, 0)),
            pl.BlockSpec((k, tn), lambda i, j: (0, j)),
            pl.BlockSpec((tm, tn), lambda i, j: (i, j)),
        ],
        out_specs=pl.BlockSpec((tm, tn), lambda i, j: (i, j)),
        out_shape=jax.ShapeDtypeStruct((s, n), F32),
        compiler_params=_params(("parallel", "arbitrary")),
    )(x, w, res)


def _ffn_in_kernel(x_ref, g_ref, wg_ref, wu_ref, o_ref, xn_ref):
    @pl.when(pl.program_id(1) == 0)
    def _():
        xn_ref[...] = _rms(x_ref[...], g_ref[...]).astype(BF16)

    xn = xn_ref[...]
    a = jnp.dot(xn, wg_ref[...], preferred_element_type=F32)
    u = jnp.dot(xn, wu_ref[...], preferred_element_type=F32)
    o_ref[...] = (a * jax.nn.sigmoid(a) * u).astype(o_ref.dtype)


def ffn_in(h, gain, w_in, *, tm=1024, tn=512):
    s, d = h.shape
    f = w_in.shape[1] // 2
    tm = min(tm, s)
    nj = f // tn
    return pl.pallas_call(
        _ffn_in_kernel,
        name="ffn_in",
        grid=(s // tm, nj),
        in_specs=[
            pl.BlockSpec((tm, d), lambda i, j: (i, 0)),
            pl.BlockSpec((1, d), lambda i, j: (0, 0)),
            pl.BlockSpec((d, tn), lambda i, j: (0, j)),
            pl.BlockSpec((d, tn), lambda i, j: (0, j + nj)),
        ],
        out_specs=pl.BlockSpec((tm, tn), lambda i, j: (i, j)),
        out_shape=jax.ShapeDtypeStruct((s, f), BF16),
        scratch_shapes=[pltpu.VMEM((tm, d), BF16)],
        compiler_params=_params(("parallel", "arbitrary")),
    )(h, gain.reshape(1, d), w_in, w_in)


def _ple_kernel(x_ref, g_ref, wg_ref, p_ref, wp_ref, h_ref, o_ref, xn_ref, pb_ref):
    @pl.when(pl.program_id(1) == 0)
    def _():
        xn_ref[...] = _rms(x_ref[...], g_ref[...]).astype(BF16)
        pb_ref[...] = p_ref[...].astype(BF16)

    gate = jax.nn.sigmoid(jnp.dot(xn_ref[...], wg_ref[...], preferred_element_type=F32))
    proj = jnp.dot(pb_ref[...], wp_ref[...], preferred_element_type=F32)
    o_ref[...] = h_ref[...] + gate * proj


def ple(h, gain, w_gate, p, w_proj, *, tm=512, tn=1024):
    s, d = h.shape
    pd = p.shape[1]
    tm = min(tm, s)
    return pl.pallas_call(
        _ple_kernel,
        name="ple",
        grid=(s // tm, d // tn),
        in_specs=[
            pl.BlockSpec((tm, d), lambda i, j: (i, 0)),
            pl.BlockSpec((1, d), lambda i, j: (0, 0)),
            pl.BlockSpec((d, tn), lambda i, j: (0, j)),
            pl.BlockSpec((tm, pd), lambda i, j: (i, 0)),
            pl.BlockSpec((pd, tn), lambda i, j: (0, j)),
            pl.BlockSpec((tm, tn), lambda i, j: (i, j)),
        ],
        out_specs=pl.BlockSpec((tm, tn), lambda i, j: (i, j)),
        out_shape=jax.ShapeDtypeStruct((s, d), F32),
        scratch_shapes=[pltpu.VMEM((tm, d), BF16), pltpu.VMEM((tm, pd), BF16)],
        compiler_params=_params(("parallel", "arbitrary")),
    )(h, gain.reshape(1, d), w_gate, p, w_proj, h)


def _final_norm_kernel(x_ref, g_ref, o_ref):
    o_ref[...] = _rms(x_ref[...], g_ref[...])


def final_rms_norm(h, gain, *, tm=512):
    s, d = h.shape
    tm = min(tm, s)
    return pl.pallas_call(
        _final_norm_kernel,
        name="final_norm",
        grid=(s // tm,),
        in_specs=[pl.BlockSpec((tm, d), lambda i: (i, 0)), pl.BlockSpec((1, d), lambda i: (0, 0))],
        out_specs=pl.BlockSpec((tm, d), lambda i: (i, 0)),
        out_shape=jax.ShapeDtypeStruct((s, d), F32),
        compiler_params=_params(("parallel",)),
    )(h, gain.reshape(1, d))


IDX_TQ = 256
IDX_KC = 256


def _orderable(x):
    b = pltpu.bitcast(x, jnp.int32)
    return jnp.where(b < 0, b ^ jnp.int32(0x7FFFFFFF), b)


def _indexer_kernel(qi_ref, ki_ref, wi_ref, mask_ref, keys_ref, *, topk, n_kblocks):
    tq, kc = IDX_TQ, IDX_KC
    i = pl.program_id(0)
    nkc = (i + 1) * (tq // kc)
    w_t = (wi_ref[...] * (IDX_HEADS ** -0.5 * IDX_DIM ** -0.5)).T
    qchunk = (i * tq + lax.broadcasted_iota(jnp.int32, (kc, tq), 1)) >> CHUNK_SHIFT
    krow = lax.broadcasted_iota(jnp.int32, (kc, tq), 0)

    def valid_of(c):
        return ((c * kc + krow) >> CHUNK_SHIFT) <= qchunk

    def score_body(c, carry):
        k0 = pl.multiple_of(c * kc, kc)
        kblk = ki_ref[pl.ds(k0, kc), :]
        acc = jnp.zeros((kc, tq), F32)
        for h in range(IDX_HEADS):
            s = lax.dot_general(kblk, qi_ref[:, h * IDX_DIM:(h + 1) * IDX_DIM], NT_DIMS,
                                preferred_element_type=F32)
            acc = acc + jnp.maximum(s, 0.0) * w_t[h:h + 1, :]
        keys_ref[pl.ds(k0, kc), :] = jnp.where(valid_of(c), _orderable(acc), jnp.int32(INT_MIN))
        return carry

    lax.fori_loop(0, nkc, score_body, 0)

    def count_ge(cand):
        def body(c, cnt):
            k0 = pl.multiple_of(c * kc, kc)
            ge = (keys_ref[pl.ds(k0, kc), :] >= cand).astype(jnp.int32)
            return cnt + ge.reshape(kc // 8, 8, tq).sum(axis=0)
        cnt = lax.fori_loop(0, nkc, body, jnp.zeros((8, tq), jnp.int32))
        return cnt.sum(axis=0, keepdims=True)

    def bit_body(b, t):
        cand = t + lax.shift_left(jnp.int32(1), 31 - b)
        return jnp.where(count_ge(cand) >= topk, cand, t)

    thr = lax.fori_loop(0, 32, bit_body, jnp.full((1, tq), INT_MIN, jnp.int32))

    def mask_body(c, carry):
        k0 = pl.multiple_of(c * kc, kc)
        sel = (keys_ref[pl.ds(k0, kc), :] >= thr) & valid_of(c)
        mask_ref[c] = jnp.where(sel, 0.0, -jnp.inf).astype(F32).T
        return carry

    lax.fori_loop(0, nkc, mask_body, 0)

    def fill_body(c, carry):
        mask_ref[c] = jnp.full((tq, kc), -jnp.inf, F32)
        return carry

    lax.fori_loop(nkc, n_kblocks, fill_body, 0)


def dsa_indexer(qiki, wi, *, topk):
    s = qiki.shape[0]
    tq, kc = IDX_TQ, IDX_KC
    nkb = s // kc
    return pl.pallas_call(
        functools.partial(_indexer_kernel, topk=topk, n_kblocks=nkb),
        name="dsa_indexer",
        grid=(s // tq,),
        in_specs=[
            pl.BlockSpec((tq, IDX_HEADS * IDX_DIM), lambda i: (i, 0)),
            pl.BlockSpec((s, IDX_DIM), lambda i: (0, IDX_HEADS)),
            pl.BlockSpec((tq, LANE), lambda i: (i, 0)),
        ],
        out_specs=pl.BlockSpec((nkb, tq, kc), lambda i: (0, i, 0)),
        out_shape=jax.ShapeDtypeStruct((nkb, s, kc), F32),
        scratch_shapes=[pltpu.VMEM((s, tq), jnp.int32)],
        compiler_params=_params(("parallel",)),
    )(qiki, qiki, wi)


ATT_A_TQ = 256
ATT_A_KB = 2
ATT_A_NEAR = (T5_FAR + ATT_A_KB * IDX_KC - 2) // IDX_KC + 1


def _t5_bucket(rel):
    nb = T5_BUCKETS // 2
    max_exact = nb // 2
    ret = jnp.where(rel > 0, nb, 0)
    n = jnp.abs(rel)
    nf = jnp.maximum(n, 1).astype(F32)
    large = max_exact + (jnp.log(nf / max_exact) / math.log(T5_MAX_DISTANCE / max_exact)
                         * (nb - max_exact)).astype(jnp.int32)
    large = jnp.minimum(large, nb - 1)
    return ret + jnp.where(n < max_exact, n, large)


def _softmax_step(s, v, m_ref, l_ref, acc_ref, rows, tk, shift=None):
    d = v.shape[1]
    m_prev = m_ref[rows, :]
    mx = jnp.max(s, axis=1, keepdims=True)
    if shift is not None:
        mx = mx + shift
    m_next = jnp.maximum(m_prev, mx)
    pivot = m_next if shift is None else m_next - shift
    p = jnp.exp2(s - jnp.concatenate([pivot] * (tk // LANE), axis=1)).astype(BF16)
    alpha = jnp.exp2(m_prev - m_next)
    v_ones = jnp.concatenate([v, jnp.ones((tk, LANE), BF16)], axis=1)
    pv = jnp.dot(p, v_ones, preferred_element_type=F32)
    m_ref[rows, :] = m_next
    l_ref[rows, :] = alpha * l_ref[rows, :] + pv[:, d:]
    acc_ref[rows, :] = acc_ref[rows, :] * alpha + pv[:, :d]


def _attn_a_kernel(qmin_ref, kmax_ref, t5s_ref, q_ref, k_ref, v_ref, mask_ref, posq_ref, posk_ref,
                   t5t_ref, o_ref, m_ref, l_ref, acc_ref, nbias_ref, *, consecutive):
    tq, kc, nb = ATT_A_TQ, IDX_KC, ATT_A_KB
    tk = nb * kc
    g = pl.program_id(0)
    i = pl.program_id(1)

    def head_bias(r, bucket):
        row = t5t_ref[pl.ds(g * A_GROUP + r, 1), :] * LOG2E
        tbl = jnp.broadcast_to(row, (tq, LANE))
        return jnp.concatenate([_lane_gather(tbl, bucket[:, c * LANE:(c + 1) * LANE])
                                for c in range(tk // LANE)], axis=1)

    if consecutive:
        @pl.when(i == 0)
        def _():
            rel0 = (lax.broadcasted_iota(jnp.int32, (tq, tk), 1)
                    - lax.broadcasted_iota(jnp.int32, (tq, tk), 0))
            for d in range(ATT_A_NEAR):
                bucket = _t5_bucket(rel0 + (d - ATT_A_NEAR + 1) * kc)
                for r in range(A_GROUP):
                    nbias_ref[d, r] = head_bias(r, bucket)

    m_ref[...] = jnp.full(m_ref.shape, NEG_BIG, F32)
    l_ref[...] = jnp.zeros(l_ref.shape, F32)
    acc_ref[...] = jnp.zeros(acc_ref.shape, F32)
    nkt = ((i + 1) * (tq // kc) + nb - 1) // nb
    qmin = qmin_ref[i]

    def is_far(kt):
        ktc = jnp.minimum(kt, nkt - 1) * nb
        kmax = kmax_ref[ktc]
        for b in range(1, nb):
            kmax = jnp.maximum(kmax, kmax_ref[ktc + b])
        return (kt < nkt) & ((qmin - kmax) >= T5_FAR)

    n_far = lax.while_loop(is_far, lambda kt: kt + 1, jnp.int32(0))

    def tile(kt, bias_of, shift_of):
        k0 = pl.multiple_of(kt * tk, tk)
        k_t = k_ref[pl.ds(k0, tk), :]
        v_t = v_ref[pl.ds(k0, tk), :]
        base = jnp.concatenate([mask_ref[kt * nb + b] for b in range(nb)], axis=1)
        for r in range(A_GROUP):
            rows = slice(r * tq, (r + 1) * tq)
            q_r = q_ref[:, r * A_HEAD_DIM:(r + 1) * A_HEAD_DIM]
            s = lax.dot_general(q_r, k_t, NT_DIMS, preferred_element_type=F32) + bias_of(r, base)
            _softmax_step(s, v_t, m_ref, l_ref, acc_ref, rows, tk, shift_of(r))

    def far_body(kt, carry):
        tile(kt, lambda r, base: base,
             lambda r: t5s_ref[T5_BUCKETS // 2 - 1, g * A_GROUP + r] * LOG2E)
        return carry

    def near_body(kt, carry):
        if consecutive:
            d = kt * nb - i * (tq // kc) + (ATT_A_NEAR - 1)
            tile(kt, lambda r, base: base + nbias_ref[d, r], lambda r: None)
        else:
            pk = jnp.concatenate([posk_ref[kt * nb + b] for b in range(nb)], axis=1)
            bucket = _t5_bucket(pk - posq_ref[...])
            tile(kt, lambda r, base: base + head_bias(r, bucket), lambda r: None)
        return carry

    lax.fori_loop(0, n_far, far_body, 0)
    lax.fori_loop(n_far, nkt, near_body, 0)
    for r in range(A_GROUP):
        rows = slice(r * tq, (r + 1) * tq)
        o_ref[:, r * A_HEAD_DIM:(r + 1) * A_HEAD_DIM] = (acc_ref[rows, :] / l_ref[rows, :]).astype(o_ref.dtype)


def dsa_attention(qkv, mask, pos, t5_table, *, consecutive):
    s = qkv.shape[0]
    tq, tk = ATT_A_TQ, IDX_KC
    assert (s // tk) % ATT_A_KB == 0
    gw = A_GROUP * A_HEAD_DIM
    qmin = pos.reshape(s // tq, tq).min(axis=1)
    kmax = pos.reshape(s // tk, tk).max(axis=1)
    t5t = jnp.zeros((A_HEADS, LANE), F32).at[:, :T5_BUCKETS].set(t5_table.T)
    kblk0 = (A_HEADS * A_HEAD_DIM) // A_HEAD_DIM
    smem = pl.BlockSpec(memory_space=pltpu.SMEM)
    nbias_shape = (ATT_A_NEAR, A_GROUP, tq, ATT_A_KB * tk) if consecutive else (1, 1, 8, LANE)
    return pl.pallas_call(
        functools.partial(_attn_a_kernel, consecutive=consecutive),
        name="dsa_attention",
        grid=(A_KV_HEADS, s // tq),
        in_specs=[
            smem, smem, smem,
            pl.BlockSpec((tq, gw), lambda g, i: (i, g)),
            pl.BlockSpec((s, A_HEAD_DIM), lambda g, i: (0, kblk0 + g)),
            pl.BlockSpec((s, A_HEAD_DIM), lambda g, i: (0, kblk0 + A_KV_HEADS + g)),
            pl.BlockSpec((s // tk, tq, tk), lambda g, i: (0, i, 0)),
            pl.BlockSpec((tq, 1), lambda g, i: (i, 0)),
            pl.BlockSpec((s // tk, 1, tk), lambda g, i: (0, 0, 0)),
            pl.BlockSpec((A_HEADS, LANE), lambda g, i: (0, 0)),
        ],
        out_specs=pl.BlockSpec((tq, gw), lambda g, i: (i, g)),
        out_shape=jax.ShapeDtypeStruct((s, A_HEADS * A_HEAD_DIM), BF16),
        scratch_shapes=[
            pltpu.VMEM((A_GROUP * tq, LANE), F32),
            pltpu.VMEM((A_GROUP * tq, LANE), F32),
            pltpu.VMEM((A_GROUP * tq, A_HEAD_DIM), F32),
            pltpu.VMEM(nbias_shape, F32),
        ],
        compiler_params=_params(("parallel", "arbitrary")),
    )(qmin, kmax, t5_table, qkv, qkv, qkv, mask, pos.reshape(s, 1), pos.reshape(s // tk, 1, tk), t5t)


ATT_B_TQ = 256
ATT_B_NKB = 3


def _attn_b_kernel(t256_ref, q_ref, k0_ref, k1_ref, k2_ref, v0_ref, v1_ref, v2_ref, posq_ref, posk_ref,
                   relt_ref, o_ref, bias_ref, *, consecutive):
    tq = ATT_B_TQ
    nh = LANE // B_HEAD_DIM
    hp = pl.program_id(0)
    i = pl.program_id(1)
    k_refs = (k0_ref, k1_ref, k2_ref)
    v_refs = (v0_ref, v1_ref, v2_ref)

    def build_bias():
        row_i = lax.broadcasted_iota(jnp.int32, (tq, tq), 0)
        col_i = lax.broadcasted_iota(jnp.int32, (tq, tq), 1)
        for j in range(ATT_B_NKB):
            back = (ATT_B_NKB - 1 - j) * tq
            if consecutive:
                rel = row_i - (col_i - back)
            else:
                rel = posq_ref[...] - posk_ref[jnp.maximum(i - (ATT_B_NKB - 1) + j, 0)]
            r = jnp.clip(rel, -B_REL_CLIP, B_REL_CLIP) + B_REL_CLIP
            dchunk = (row_i >> CHUNK_SHIFT) - ((col_i - back) >> CHUNK_SHIFT)
            band = (dchunk >= 0) & (dchunk <= B_PREV_CHUNKS)
            for hh in range(nh):
                h = hp * nh + hh
                row = relt_ref[pl.ds(h, 1), :] * LOG2E
                seg0 = jnp.broadcast_to(row[:, :LANE], (tq, LANE))
                seg1 = jnp.broadcast_to(row[:, LANE:2 * LANE], (tq, LANE))
                t256 = t256_ref[h] * LOG2E
                for c in range(tq // LANE):
                    cs = slice(c * LANE, (c + 1) * LANE)
                    rc = r[:, cs]
                    lo = rc & (LANE - 1)
                    bias = jnp.where(rc < LANE, _lane_gather(seg0, lo),
                                     jnp.where(rc < 2 * LANE, _lane_gather(seg1, lo), t256))
                    bias_ref[hh, :, j * tq + c * LANE:j * tq + (c + 1) * LANE] = jnp.where(band[:, cs], bias, -jnp.inf)

    if consecutive:
        pl.when(i == 0)(build_bias)
    else:
        build_bias()

    qp = q_ref[...]
    lane_head = lax.broadcasted_iota(jnp.int32, (1, LANE), 1) >> CHUNK_SHIFT
    out = jnp.zeros((tq, LANE), F32)
    for hh in range(nh):
        mine = lane_head == hh
        qm = jnp.where(mine, qp, jnp.zeros_like(qp))
        parts = []
        for j in range(ATT_B_NKB):
            sj = lax.dot_general(qm, k_refs[j][...], NT_DIMS, preferred_element_type=F32)
            sj = sj + bias_ref[hh, :, j * tq:(j + 1) * tq]
            if j < ATT_B_NKB - 1:
                sj = sj + jnp.where(i - (ATT_B_NKB - 1) + j >= 0, 0.0, -jnp.inf)
            parts.append(sj)
        s = jnp.concatenate(parts, axis=1)
        p = jnp.exp2(s - jnp.max(s, axis=1, keepdims=True)).astype(BF16)
        acc = jnp.zeros((tq, LANE), F32)
        for j in range(ATT_B_NKB):
            vj = v_refs[j][...]
            vm = jnp.where(mine, vj, jnp.ones_like(vj))
            acc = acc + jnp.dot(p[:, j * tq:(j + 1) * tq], vm, preferred_element_type=F32)
        rowsum = pltpu.roll(acc, B_HEAD_DIM, 1)
        out = out + jnp.where(mine, acc / rowsum, 0.0)
    o_ref[...] = out.astype(o_ref.dtype)


def band_attention(qkv, pos, rel_table, *, consecutive):
    s = qkv.shape[0]
    tq = ATT_B_TQ
    hw = B_HEADS * B_HEAD_DIM
    npair = hw // LANE
    assert LANE // B_HEAD_DIM == 2
    nrel = 2 * B_REL_CLIP + 1
    relt = jnp.zeros((B_HEADS, 3 * LANE), F32).at[:, :nrel].set(rel_table.T)
    t256 = rel_table[nrel - 1]

    def kv_spec(j, base):
        return pl.BlockSpec((tq, LANE),
                            lambda hp, i: (jnp.maximum(i - (ATT_B_NKB - 1) + j, 0), base + hp))

    return pl.pallas_call(
        functools.partial(_attn_b_kernel, consecutive=consecutive),
        name="band_attention",
        grid=(npair, s // tq),
        in_specs=[
            pl.BlockSpec(memory_space=pltpu.SMEM),
            pl.BlockSpec((tq, LANE), lambda hp, i: (i, hp)),
            kv_spec(0, npair), kv_spec(1, npair), kv_spec(2, npair),
            kv_spec(0, 2 * npair), kv_spec(1, 2 * npair), kv_spec(2, 2 * npair),
            pl.BlockSpec((tq, 1), lambda hp, i: (i, 0)),
            pl.BlockSpec((s // tq, 1, tq), lambda hp, i: (0, 0, 0)),
            pl.BlockSpec((B_HEADS, 3 * LANE), lambda hp, i: (0, 0)),
        ],
        out_specs=pl.BlockSpec((tq, LANE), lambda hp, i: (i, hp)),
        out_shape=jax.ShapeDtypeStruct((s, hw), BF16),
        scratch_shapes=[pltpu.VMEM((LANE // B_HEAD_DIM, tq, ATT_B_NKB * tq), F32)],
        compiler_params=_params(("parallel", "arbitrary")),
    )(t256, qkv, qkv, qkv, qkv, qkv, qkv, qkv, pos.reshape(s, 1), pos.reshape(s // tq, 1, tq), relt)


ATT_C_T = 512
ATT_C_HEADS = 2


def _attn_c_kernel(q_ref, kn_ref, kr_ref, v_ref, o_ref, m_ref, l_ref, acc_ref):
    t, nh = ATT_C_T, ATT_C_HEADS
    iq = pl.program_id(1)
    m_ref[...] = jnp.full(m_ref.shape, NEG_BIG, F32)
    l_ref[...] = jnp.zeros(l_ref.shape, F32)
    acc_ref[...] = jnp.zeros(acc_ref.shape, F32)

    def tile(kt, diag):
        k0 = pl.multiple_of(kt * t, t)
        kr_t = kr_ref[pl.ds(k0, t), :]
        for hh in range(nh):
            kcat = jnp.concatenate([kn_ref[pl.ds(k0, t), hh * C_NOPE:(hh + 1) * C_NOPE], kr_t], axis=1)
            s = lax.dot_general(q_ref[:, hh * 2 * LANE:(hh + 1) * 2 * LANE], kcat, NT_DIMS,
                                preferred_element_type=F32)
            if diag:
                qc = lax.broadcasted_iota(jnp.int32, (t, t), 0) >> CHUNK_SHIFT
                kc = lax.broadcasted_iota(jnp.int32, (t, t), 1) >> CHUNK_SHIFT
                s = jnp.where(kc <= qc, s, -jnp.inf)
            _softmax_step(s, v_ref[pl.ds(k0, t), hh * C_V:(hh + 1) * C_V], m_ref, l_ref, acc_ref,
                          slice(hh * t, (hh + 1) * t), t)

    def body(kt, carry):
        tile(kt, False)
        return carry

    lax.fori_loop(0, iq, body, 0)
    tile(iq, True)
    for hh in range(nh):
        rows = slice(hh * t, (hh + 1) * t)
        o_ref[:, hh * C_V:(hh + 1) * C_V] = (acc_ref[rows, :] / l_ref[rows, :]).astype(o_ref.dtype)


def mla_attention(qcat, kv, kr):
    s = qcat.shape[0]
    t, nh = ATT_C_T, ATT_C_HEADS
    assert s % t == 0
    return pl.pallas_call(
        _attn_c_kernel,
        name="mla_attention",
        grid=(C_HEADS // nh, s // t),
        in_specs=[
            pl.BlockSpec((t, nh * 2 * LANE), lambda h, i: (i, h)),
            pl.BlockSpec((s, nh * C_NOPE), lambda h, i: (0, h)),
            pl.BlockSpec((s, LANE), lambda h, i: (0, 0)),
            pl.BlockSpec((s, nh * C_V), lambda h, i: (0, C_HEADS // nh + h)),
        ],
        out_specs=pl.BlockSpec((t, nh * C_V), lambda h, i: (i, h)),
        out_shape=jax.ShapeDtypeStruct((s, C_HEADS * C_V), BF16),
        scratch_shapes=[pltpu.VMEM((nh * t, LANE), F32), pltpu.VMEM((nh * t, LANE), F32),
                        pltpu.VMEM((nh * t, C_V), F32)],
        compiler_params=_params(("parallel", "arbitrary")),
    )(qcat, kv, kr, kv)


def _rope_tables(pos):
    half = ROPE_DIM // 2
    inv = ROPE_BASE ** (-jnp.arange(half, dtype=F32) * 2.0 / ROPE_DIM)
    ang = pos.astype(F32)[:, None] * inv
    cos, sin = jnp.cos(ang), jnp.sin(ang)
    s = pos.shape[0]
    z = jnp.zeros((s, half), F32)
    c = jnp.concatenate([cos, cos, jnp.ones((s, LANE - ROPE_DIM), F32)], axis=1)
    s1 = jnp.concatenate([-sin, z, z, z], axis=1)
    s2 = jnp.concatenate([z, sin, z, z], axis=1)
    return c, s1, s2


def _query_scale(n_query, n_total, scale):
    return jnp.concatenate([jnp.full((n_query,), scale * LOG2E, F32), jnp.ones((n_total - n_query,), F32)])


def _by_position_layout(consecutive, fn, *args):
    return lax.cond(consecutive, functools.partial(fn, consecutive=True),
                    functools.partial(fn, consecutive=False), *args)


def _mixer_a(h, gain, pos, consecutive, tables, w_in, w_out, t5_table):
    s = h.shape[0]
    d = h.shape[1]
    nq = A_HEADS * A_HEAD_DIM
    nkv = A_KV_HEADS * A_HEAD_DIM
    nqi = IDX_HEADS * IDX_DIM
    o_qkv = nq + 2 * nkv
    o_ki = o_qkv + nqi + IDX_DIM
    w = w_in.astype(BF16)
    w_wi = jnp.zeros((d, LANE), BF16).at[:, :IDX_HEADS].set(w[:, o_ki:o_ki + IDX_HEADS])
    qkv = norm_mm(h, 0, d, gain, w[:, :o_qkv], tn=1024, out_dtype=BF16, name="a_qkv_proj",
                  col_scale=_query_scale(nq, o_qkv, A_HEAD_DIM ** -0.5))
    n_qiki = nqi + IDX_DIM
    qiki = norm_mm(h, 0, d, gain, w[:, o_qkv:o_ki], tn=n_qiki, out_dtype=BF16,
                   rope=(True,) * (n_qiki // LANE), tables=tables, name="a_idx_proj")
    wi = norm_mm(h, 0, d, gain, w_wi, tn=LANE, out_dtype=F32, name="a_idxw_proj")
    mask = dsa_indexer(qiki, wi, topk=min(IDX_TOPK_MAX, s // 4))
    o = _by_position_layout(consecutive, dsa_attention, qkv, mask, pos, t5_table)
    return mm_residual(o, w_out.astype(BF16), h)


def _mixer_b(h, gain, pos, consecutive, w_in, rel_table, w_out):
    d = h.shape[1]
    n = w_in.shape[1]
    qkv = norm_mm(h, 0, d, gain, w_in.astype(BF16), tn=1024, out_dtype=BF16, name="b_qkv_proj",
                  col_scale=_query_scale(n // 3, n, B_HEAD_DIM ** -0.5))
    o = _by_position_layout(consecutive, band_attention, qkv, pos, rel_table)
    return mm_residual(o, w_out.astype(BF16), h)


def _mixer_c(h, gain, tables, w_down, g_q, g_kv, w_uq, w_ukv, w_out):
    d = h.shape[1]
    lq = g_q.shape[0]
    lkv = g_kv.shape[0]
    n_down = lq + lkv + LANE
    wd = jnp.zeros((d, n_down), BF16).at[:, :w_down.shape[1]].set(w_down.astype(BF16))
    down = norm_mm(h, 0, d, gain, wd, tn=n_down, out_dtype=F32,
                   rope=(False,) * ((lq + lkv) // LANE) + (True,), tables=tables, name="c_down_proj")
    wq = w_uq.astype(BF16).reshape(lq, C_HEADS, C_NOPE + C_ROPE)
    wq = jnp.pad(wq, ((0, 0), (0, 0), (0, 2 * LANE - C_NOPE - C_ROPE))).reshape(lq, C_HEADS * 2 * LANE)
    nqc = wq.shape[1]
    qcat = norm_mm(down, 0, lq, g_q, wq, tn=1024, out_dtype=BF16, rope=(False, True) * 4, tables=tables,
                   name="c_uq_proj", col_scale=_query_scale(nqc, nqc, (C_NOPE + C_ROPE) ** -0.5))
    wkv = w_ukv.astype(BF16).reshape(lkv, C_HEADS, 2, C_NOPE).transpose(0, 2, 1, 3).reshape(lkv, -1)
    assert lq == lkv
    kv = norm_mm(down, 1, lkv, g_kv, wkv, tn=1024, out_dtype=BF16, name="c_ukv_proj")
    kr = down[:, lq + lkv:].astype(BF16)
    o = mla_attention(qcat, kv, kr)
    return mm_residual(o, w_out.astype(BF16), h)


def kernel(x, p, positions, t5_table, a_w_in, a_w_out, b_w_in, b_rel_table, b_w_out, c_w_down, c_q_norm,
           c_kv_norm, c_w_uq, c_w_ukv, c_w_out, attn_norm, ffn_norm, ffn_w_in, ffn_w_out, ple_norm,
           ple_w_gate, ple_w_proj, final_norm):
    assert x.shape[0] == 1
    depth = attn_norm.shape[0]
    h = x[0]
    pos = positions[0]
    tables = _rope_tables(pos)
    consecutive = jnp.all(pos[1:] - pos[:-1] == 1)
    for i in range(depth):
        j, kind = divmod(i, 3)
        if kind == 0:
            h = _mixer_a(h, attn_norm[i], pos, consecutive, tables, a_w_in[j], a_w_out[j], t5_table)
        elif kind == 1:
            h = _mixer_b(h, attn_norm[i], pos, consecutive, b_w_in[j], b_rel_table[j], b_w_out[j])
        else:
            h = _mixer_c(h, attn_norm[i], tables, c_w_down[j], c_q_norm[j], c_kv_norm[j], c_w_uq[j],
                         c_w_ukv[j], c_w_out[j])
        act = ffn_in(h, ffn_norm[i], ffn_w_in[i].astype(BF16))
        h = mm_residual(act, ffn_w_out[i].astype(BF16), h)
        h = ple(h, ple_norm[i], ple_w_gate[i].astype(BF16), p[i, 0], ple_w_proj[i].astype(BF16))
    return final_rms_norm(h, final_norm)[None]
```

```python
import functools
import math

import jax
import jax.numpy as jnp
from jax import lax
from jax.experimental import pallas as pl
from jax.experimental.pallas import tpu as pltpu

LANE = 128
VMEM_LIMIT_BYTES = 56 * 1024 * 1024

CHUNK = 64
CHUNK_SHIFT = 6
EPS = 1e-6
ROPE_BASE = 10000.0
ROPE_DIM = 64
A_HEADS = 16
A_KV_HEADS = 4
A_GROUP = 4
A_HEAD_DIM = 128
IDX_HEADS = 16
IDX_DIM = 128
IDX_TOPK_MAX = 256
T5_BUCKETS = 32
T5_MAX_DISTANCE = 1024
T5_FAR = 640
B_HEADS = 32
B_HEAD_DIM = 64
B_PREV_CHUNKS = 8
B_REL_CLIP = 128
C_HEADS = 16
C_NOPE = 128
C_ROPE = 64
C_V = 128
NEG_BIG = -1e30
INT_MIN = -(2 ** 31)
LOG2E = math.log2(math.e)

F32 = jnp.float32
BF16 = jnp.bfloat16
NT_DIMS = (((1,), (1,)), ((), ()))


def _params(sem):
    return pltpu.CompilerParams(dimension_semantics=sem, vmem_limit_bytes=VMEM_LIMIT_BYTES)


def _rms(x, g):
    ms = jnp.mean(x * x, axis=-1, keepdims=True)
    return (x * lax.rsqrt(ms + EPS)) * g


def _lane_gather(table, idx):
    return jnp.take_along_axis(table, idx, axis=1, mode="promise_in_bounds")


def _rope_group(y, c, s1, s2):
    return y * c + pltpu.roll(y, 96, 1) * s1 + pltpu.roll(y, 32, 1) * s2


def _norm_mm_kernel(*refs, rope, scaled):
    x_ref, g_ref, w_ref = refs[:3]
    rest = list(refs[3:])
    cs_ref = rest.pop(0) if scaled else None
    c_ref, s1_ref, s2_ref = (rest.pop(0), rest.pop(0), rest.pop(0)) if rope is not None else (None,) * 3
    o_ref, xn_ref = rest

    @pl.when(pl.program_id(1) == 0)
    def _():
        xn_ref[...] = _rms(x_ref[...], g_ref[...]).astype(BF16)

    y = jnp.dot(xn_ref[...], w_ref[...], preferred_element_type=F32)
    if scaled:
        y = y * cs_ref[...]
    if rope is None:
        o_ref[...] = y.astype(o_ref.dtype)
    else:
        c, s1, s2 = c_ref[...], s1_ref[...], s2_ref[...]
        for gi, on in enumerate(rope):
            sl = slice(gi * LANE, (gi + 1) * LANE)
            yg = y[:, sl]
            if on:
                yg = _rope_group(yg, c, s1, s2)
            o_ref[:, sl] = yg.astype(o_ref.dtype)


def norm_mm(x, x_col, kx, gain, w, *, tn, out_dtype, name, tm=512, rope=None, tables=None, col_scale=None):
    s = x.shape[0]
    n = w.shape[1]
    tm = min(tm, s)
    in_specs = [
        pl.BlockSpec((tm, kx), lambda i, j: (i, x_col)),
        pl.BlockSpec((1, kx), lambda i, j: (0, 0)),
        pl.BlockSpec((kx, tn), lambda i, j: (0, j)),
    ]
    args = [x, gain.reshape(1, kx), w]
    if col_scale is not None:
        in_specs.append(pl.BlockSpec((1, tn), lambda i, j: (0, j)))
        args.append(col_scale.reshape(1, n))
    if rope is not None:
        assert len(rope) == tn // LANE
        in_specs += [pl.BlockSpec((tm, LANE), lambda i, j: (i, 0))] * 3
        args += list(tables)
    return pl.pallas_call(
        functools.partial(_norm_mm_kernel, rope=rope, scaled=col_scale is not None),
        grid=(s // tm, n // tn),
        in_specs=in_specs,
        out_specs=pl.BlockSpec((tm, tn), lambda i, j: (i, j)),
        out_shape=jax.ShapeDtypeStruct((s, n), out_dtype),
        scratch_shapes=[pltpu.VMEM((tm, kx), BF16)],
        compiler_params=_params(("parallel", "arbitrary")),
        name=name,
    )(*args)


def _mm_res_kernel(x_ref, w_ref, r_ref, o_ref):
    o_ref[...] = r_ref[...] + jnp.dot(x_ref[...], w_ref[...], preferred_element_type=F32)


def mm_residual(x, w, res, *, tm=512, tn=512):
    s, k = x.shape
    n = w.shape[1]
    tm = min(tm, s)
    return pl.pallas_call(
        _mm_res_kernel,
        name="mm_residual",
        grid=(s // tm, n // tn),
        in_specs=[
            pl.BlockSpec((tm, k), lambda i, j: (i, 0)),
            pl.BlockSpec((k, tn), lambda i, j: (0, j)),
            pl.BlockSpec((tm, tn), lambda i, j: (i, j)),
        ],
        out_specs=pl.BlockSpec((tm, tn), lambda i, j: (i, j)),
        out_shape=jax.ShapeDtypeStruct((s, n), F32),
        compiler_params=_params(("parallel", "arbitrary")),
    )(x, w, res)


def _ffn_in_kernel(x_ref, g_ref, wg_ref, wu_ref, o_ref, xn_ref):
    @pl.when(pl.program_id(1) == 0)
    def _():
        xn_ref[...] = _rms(x_ref[...], g_ref[...]).astype(BF16)

    xn = xn_ref[...]
    a = jnp.dot(xn, wg_ref[...], preferred_element_type=F32)
    u = jnp.dot(xn, wu_ref[...], preferred_element_type=F32)
    o_ref[...] = (a * jax.nn.sigmoid(a) * u).astype(o_ref.dtype)


def ffn_in(h, gain, w_in, *, tm=1024, tn=512):
    s, d = h.shape
    f = w_in.shape[1] // 2
    tm = min(tm, s)
    nj = f // tn
    return pl.pallas_call(
        _ffn_in_kernel,
        name="ffn_in",
        grid=(s // tm, nj),
        in_specs=[
            pl.BlockSpec((tm, d), lambda i, j: (i, 0)),
            pl.BlockSpec((1, d), lambda i, j: (0, 0)),
            pl.BlockSpec((d, tn), lambda i, j: (0, j)),
            pl.BlockSpec((d, tn), lambda i, j: (0, j + nj)),
        ],
        out_specs=pl.BlockSpec((tm, tn), lambda i, j: (i, j)),
        out_shape=jax.ShapeDtypeStruct((s, f), BF16),
        scratch_shapes=[pltpu.VMEM((tm, d), BF16)],
        compiler_params=_params(("parallel", "arbitrary")),
    )(h, gain.reshape(1, d), w_in, w_in)


def _ple_kernel(x_ref, g_ref, wg_ref, p_ref, wp_ref, h_ref, o_ref, xn_ref, pb_ref):
    @pl.when(pl.program_id(1) == 0)
    def _():
        xn_ref[...] = _rms(x_ref[...], g_ref[...]).astype(BF16)
        pb_ref[...] = p_ref[...].astype(BF16)

    gate = jax.nn.sigmoid(jnp.dot(xn_ref[...], wg_ref[...], preferred_element_type=F32))
    proj = jnp.dot(pb_ref[...], wp_ref[...], preferred_element_type=F32)
    o_ref[...] = h_ref[...] + gate * proj


def ple(h, gain, w_gate, p, w_proj, *, tm=512, tn=2048):
    s, d = h.shape
    pd = p.shape[1]
    tm = min(tm, s)
    return pl.pallas_call(
        _ple_kernel,
        name="ple",
        grid=(s // tm, d // tn),
        in_specs=[
            pl.BlockSpec((tm, d), lambda i, j: (i, 0)),
            pl.BlockSpec((1, d), lambda i, j: (0, 0)),
            pl.BlockSpec((d, tn), lambda i, j: (0, j)),
            pl.BlockSpec((tm, pd), lambda i, j: (i, 0)),
            pl.BlockSpec((pd, tn), lambda i, j: (0, j)),
            pl.BlockSpec((tm, tn), lambda i, j: (i, j)),
        ],
        out_specs=pl.BlockSpec((tm, tn), lambda i, j: (i, j)),
        out_shape=jax.ShapeDtypeStruct((s, d), F32),
        scratch_shapes=[pltpu.VMEM((tm, d), BF16), pltpu.VMEM((tm, pd), BF16)],
        compiler_params=_params(("parallel", "arbitrary")),
    )(h, gain.reshape(1, d), w_gate, p, w_proj, h)


def _final_norm_kernel(x_ref, g_ref, o_ref):
    o_ref[...] = _rms(x_ref[...], g_ref[...])


def final_rms_norm(h, gain, *, tm=512):
    s, d = h.shape
    tm = min(tm, s)
    return pl.pallas_call(
        _final_norm_kernel,
        name="final_norm",
        grid=(s // tm,),
        in_specs=[pl.BlockSpec((tm, d), lambda i: (i, 0)), pl.BlockSpec((1, d), lambda i: (0, 0))],
        out_specs=pl.BlockSpec((tm, d), lambda i: (i, 0)),
        out_shape=jax.ShapeDtypeStruct((s, d), F32),
        compiler_params=_params(("parallel",)),
    )(h, gain.reshape(1, d))


IDX_TQ = 256
IDX_KC = 256


def _orderable(x):
    b = pltpu.bitcast(x, jnp.int32)
    return jnp.where(b < 0, b ^ jnp.int32(0x7FFFFFFF), b)


def _indexer_kernel(qi_ref, ki_ref, wi_ref, mask_ref, keys_ref, *, topk, n_kblocks):
    tq, kc = IDX_TQ, IDX_KC
    i = pl.program_id(0)
    nkc = (i + 1) * (tq // kc)
    n_idx_bits = (n_kblocks * kc - 1).bit_length()
    w_t = (wi_ref[...] * (IDX_HEADS ** -0.5 * IDX_DIM ** -0.5)).T
    qchunk = (i * tq + lax.broadcasted_iota(jnp.int32, (kc, tq), 1)) >> CHUNK_SHIFT
    krow = lax.broadcasted_iota(jnp.int32, (kc, tq), 0)

    def valid_of(c):
        return ((c * kc + krow) >> CHUNK_SHIFT) <= qchunk

    def score_body(c, carry):
        k0 = pl.multiple_of(c * kc, kc)
        kblk = ki_ref[pl.ds(k0, kc), :]
        acc = jnp.zeros((kc, tq), F32)
        for h in range(IDX_HEADS):
            s = lax.dot_general(kblk, qi_ref[:, h * IDX_DIM:(h + 1) * IDX_DIM], NT_DIMS,
                                preferred_element_type=F32)
            acc = acc + jnp.maximum(s, 0.0) * w_t[h:h + 1, :]
        keys_ref[pl.ds(k0, kc), :] = jnp.where(valid_of(c), _orderable(acc), jnp.int32(INT_MIN))
        return carry

    lax.fori_loop(0, nkc, score_body, 0)

    def count(pred_of):
        def body(c, cnt):
            k0 = pl.multiple_of(c * kc, kc)
            hit = pred_of(c, keys_ref[pl.ds(k0, kc), :]).astype(jnp.int32)
            return cnt + hit.reshape(kc // 8, 8, tq).sum(axis=0)
        cnt = lax.fori_loop(0, nkc, body, jnp.zeros((8, tq), jnp.int32))
        return cnt.sum(axis=0, keepdims=True)

    def bit_step(b, state):
        thr, cnt = state
        cand = thr + lax.shift_left(jnp.int32(1), 31 - b)
        c_cand = count(lambda c, keys: keys >= cand)
        take = c_cand >= topk
        return jnp.where(take, cand, thr), jnp.where(take, c_cand, cnt)

    thr, cnt = lax.fori_loop(
        0, 32, bit_step,
        (jnp.full((1, tq), INT_MIN, jnp.int32), jnp.full((1, tq), nkc * kc, jnp.int32)))

    def tie_limit():
        need = topk - count(lambda c, keys: keys > thr)
        def bit_body(b, x):
            cand = x + lax.shift_left(jnp.int32(1), n_idx_bits - 1 - b)
            below = count(lambda c, keys: (keys == thr) & ((c * kc + krow) < cand))
            return jnp.where(below < need, cand, x)
        return lax.fori_loop(0, n_idx_bits, bit_body, jnp.zeros((1, tq), jnp.int32))

    def write_masks(selected):
        def mask_body(c, carry):
            k0 = pl.multiple_of(c * kc, kc)
            sel = selected(c, keys_ref[pl.ds(k0, kc), :]) & valid_of(c)
            mask_ref[c] = jnp.where(sel, 0.0, -jnp.inf).astype(F32).T
            return carry
        lax.fori_loop(0, nkc, mask_body, 0)

    has_ties = jnp.max(jnp.where((cnt > topk) & (thr > INT_MIN), 1, 0)) > 0

    @pl.when(has_ties)
    def _():
        tie_last = tie_limit()
        write_masks(lambda c, keys: (keys > thr) | ((keys == thr) & ((c * kc + krow) <= tie_last)))

    @pl.when(jnp.logical_not(has_ties))
    def _():
        write_masks(lambda c, keys: keys >= thr)

    def fill_body(c, carry):
        mask_ref[c] = jnp.full((tq, kc), -jnp.inf, F32)
        return carry

    lax.fori_loop(nkc, n_kblocks, fill_body, 0)


def dsa_indexer(qiki, wi, *, topk):
    s = qiki.shape[0]
    tq, kc = IDX_TQ, IDX_KC
    nkb = s // kc
    return pl.pallas_call(
        functools.partial(_indexer_kernel, topk=topk, n_kblocks=nkb),
        name="dsa_indexer",
        grid=(s // tq,),
        in_specs=[
            pl.BlockSpec((tq, IDX_HEADS * IDX_DIM), lambda i: (i, 0)),
            pl.BlockSpec((s, IDX_DIM), lambda i: (0, IDX_HEADS)),
            pl.BlockSpec((tq, LANE), lambda i: (i, 0)),
        ],
        out_specs=pl.BlockSpec((nkb, tq, kc), lambda i: (0, i, 0)),
        out_shape=jax.ShapeDtypeStruct((nkb, s, kc), F32),
        scratch_shapes=[pltpu.VMEM((s, tq), jnp.int32)],
        compiler_params=_params(("parallel",)),
    )(qiki, qiki, wi)


ATT_A_TQ = 256
ATT_A_KB = 2
ATT_A_NEAR = (T5_FAR + ATT_A_KB * IDX_KC - 2) // IDX_KC + 1


def _t5_bucket(rel):
    nb = T5_BUCKETS // 2
    max_exact = nb // 2
    ret = jnp.where(rel > 0, nb, 0)
    n = jnp.abs(rel)
    nf = jnp.maximum(n, 1).astype(F32)
    large = max_exact + (jnp.log(nf / max_exact) / math.log(T5_MAX_DISTANCE / max_exact)
                         * (nb - max_exact)).astype(jnp.int32)
    large = jnp.minimum(large, nb - 1)
    return ret + jnp.where(n < max_exact, n, large)


def _softmax_step(s, v, m_ref, l_ref, acc_ref, rows, tk, shift=None):
    d = v.shape[1]
    m_prev = m_ref[rows, :]
    mx = jnp.max(s, axis=1, keepdims=True)
    if shift is not None:
        mx = mx + shift
    m_next = jnp.maximum(m_prev, mx)
    pivot = m_next if shift is None else m_next - shift
    p = jnp.exp2(s - jnp.concatenate([pivot] * (tk // LANE), axis=1)).astype(BF16)
    alpha = jnp.exp2(m_prev - m_next)
    v_ones = jnp.concatenate([v, jnp.ones((tk, LANE), BF16)], axis=1)
    pv = jnp.dot(p, v_ones, preferred_element_type=F32)
    m_ref[rows, :] = m_next
    l_ref[rows, :] = alpha * l_ref[rows, :] + pv[:, d:]
    acc_ref[rows, :] = acc_ref[rows, :] * alpha + pv[:, :d]


def _attn_a_kernel(qmin_ref, kmax_ref, t5s_ref, q_ref, k_ref, v_ref, mask_ref, posq_ref, posk_ref,
                   t5t_ref, o_ref, m_ref, l_ref, acc_ref, nbias_ref, *, consecutive):
    tq, kc, nb = ATT_A_TQ, IDX_KC, ATT_A_KB
    tk = nb * kc
    g = pl.program_id(0)
    i = pl.program_id(1)

    def head_bias(r, bucket):
        row = t5t_ref[pl.ds(g * A_GROUP + r, 1), :] * LOG2E
        tbl = jnp.broadcast_to(row, (tq, LANE))
        return jnp.concatenate([_lane_gather(tbl, bucket[:, c * LANE:(c + 1) * LANE])
                                for c in range(tk // LANE)], axis=1)

    if consecutive:
        @pl.when(i == 0)
        def _():
            rel0 = (lax.broadcasted_iota(jnp.int32, (tq, tk), 1)
                    - lax.broadcasted_iota(jnp.int32, (tq, tk), 0))
            for d in range(ATT_A_NEAR):
                bucket = _t5_bucket(rel0 + (d - ATT_A_NEAR + 1) * kc)
                for r in range(A_GROUP):
                    nbias_ref[d, r] = head_bias(r, bucket)

    m_ref[...] = jnp.full(m_ref.shape, NEG_BIG, F32)
    l_ref[...] = jnp.zeros(l_ref.shape, F32)
    acc_ref[...] = jnp.zeros(acc_ref.shape, F32)
    nkt = ((i + 1) * (tq // kc) + nb - 1) // nb
    qmin = qmin_ref[i]

    def is_far(kt):
        ktc = jnp.minimum(kt, nkt - 1) * nb
        kmax = kmax_ref[ktc]
        for b in range(1, nb):
            kmax = jnp.maximum(kmax, kmax_ref[ktc + b])
        return (kt < nkt) & ((qmin - kmax) >= T5_FAR)

    n_far = lax.while_loop(is_far, lambda kt: kt + 1, jnp.int32(0))

    def tile(kt, bias_of, shift_of):
        k0 = pl.multiple_of(kt * tk, tk)
        k_t = k_ref[pl.ds(k0, tk), :]
        v_t = v_ref[pl.ds(k0, tk), :]
        base = jnp.concatenate([mask_ref[kt * nb + b] for b in range(nb)], axis=1)
        for r in range(A_GROUP):
            rows = slice(r * tq, (r + 1) * tq)
            q_r = q_ref[:, r * A_HEAD_DIM:(r + 1) * A_HEAD_DIM]
            s = lax.dot_general(q_r, k_t, NT_DIMS, preferred_element_type=F32) + bias_of(r, base)
            _softmax_step(s, v_t, m_ref, l_ref, acc_ref, rows, tk, shift_of(r))

    def far_body(kt, carry):
        tile(kt, lambda r, base: base,
             lambda r: t5s_ref[T5_BUCKETS // 2 - 1, g * A_GROUP + r] * LOG2E)
        return carry

    def near_body(kt, carry):
        if consecutive:
            d = kt * nb - i * (tq // kc) + (ATT_A_NEAR - 1)
            tile(kt, lambda r, base: base + nbias_ref[d, r], lambda r: None)
        else:
            pk = jnp.concatenate([posk_ref[kt * nb + b] for b in range(nb)], axis=1)
            bucket = _t5_bucket(pk - posq_ref[...])
            tile(kt, lambda r, base: base + head_bias(r, bucket), lambda r: None)
        return carry

    lax.fori_loop(0, n_far, far_body, 0)
    lax.fori_loop(n_far, nkt, near_body, 0)
    for r in range(A_GROUP):
        rows = slice(r * tq, (r + 1) * tq)
        o_ref[:, r * A_HEAD_DIM:(r + 1) * A_HEAD_DIM] = (acc_ref[rows, :] / l_ref[rows, :]).astype(o_ref.dtype)


def dsa_attention(qkv, mask, pos, t5_table, *, consecutive):
    s = qkv.shape[0]
    tq, tk = ATT_A_TQ, IDX_KC
    assert (s // tk) % ATT_A_KB == 0
    gw = A_GROUP * A_HEAD_DIM
    qmin = pos.reshape(s // tq, tq).min(axis=1)
    kmax = pos.reshape(s // tk, tk).max(axis=1)
    t5t = jnp.zeros((A_HEADS, LANE), F32).at[:, :T5_BUCKETS].set(t5_table.T)
    kblk0 = (A_HEADS * A_HEAD_DIM) // A_HEAD_DIM
    smem = pl.BlockSpec(memory_space=pltpu.SMEM)
    nbias_shape = (ATT_A_NEAR, A_GROUP, tq, ATT_A_KB * tk) if consecutive else (1, 1, 8, LANE)
    return pl.pallas_call(
        functools.partial(_attn_a_kernel, consecutive=consecutive),
        name="dsa_attention",
        grid=(A_KV_HEADS, s // tq),
        in_specs=[
            smem, smem, smem,
            pl.BlockSpec((tq, gw), lambda g, i: (i, g)),
            pl.BlockSpec((s, A_HEAD_DIM), lambda g, i: (0, kblk0 + g)),
            pl.BlockSpec((s, A_HEAD_DIM), lambda g, i: (0, kblk0 + A_KV_HEADS + g)),
            pl.BlockSpec((s // tk, tq, tk), lambda g, i: (0, i, 0)),
            pl.BlockSpec((tq, 1), lambda g, i: (i, 0)),
            pl.BlockSpec((s // tk, 1, tk), lambda g, i: (0, 0, 0)),
            pl.BlockSpec((A_HEADS, LANE), lambda g, i: (0, 0)),
        ],
        out_specs=pl.BlockSpec((tq, gw), lambda g, i: (i, g)),
        out_shape=jax.ShapeDtypeStruct((s, A_HEADS * A_HEAD_DIM), BF16),
        scratch_shapes=[
            pltpu.VMEM((A_GROUP * tq, LANE), F32),
            pltpu.VMEM((A_GROUP * tq, LANE), F32),
            pltpu.VMEM((A_GROUP * tq, A_HEAD_DIM), F32),
            pltpu.VMEM(nbias_shape, F32),
        ],
        compiler_params=_params(("parallel", "arbitrary")),
    )(qmin, kmax, t5_table, qkv, qkv, qkv, mask, pos.reshape(s, 1), pos.reshape(s // tk, 1, tk), t5t)


ATT_B_TQ = 256
ATT_B_NKB = 3


def _attn_b_kernel(t256_ref, q_ref, k0_ref, k1_ref, k2_ref, v0_ref, v1_ref, v2_ref, posq_ref, posk_ref,
                   relt_ref, o_ref, bias_ref, *, consecutive):
    tq = ATT_B_TQ
    nh = LANE // B_HEAD_DIM
    hp = pl.program_id(0)
    i = pl.program_id(1)
    k_refs = (k0_ref, k1_ref, k2_ref)
    v_refs = (v0_ref, v1_ref, v2_ref)

    def build_bias():
        row_i = lax.broadcasted_iota(jnp.int32, (tq, tq), 0)
        col_i = lax.broadcasted_iota(jnp.int32, (tq, tq), 1)
        for j in range(ATT_B_NKB):
            back = (ATT_B_NKB - 1 - j) * tq
            if consecutive:
                rel = row_i - (col_i - back)
            else:
                rel = posq_ref[...] - posk_ref[jnp.maximum(i - (ATT_B_NKB - 1) + j, 0)]
            r = jnp.clip(rel, -B_REL_CLIP, B_REL_CLIP) + B_REL_CLIP
            dchunk = (row_i >> CHUNK_SHIFT) - ((col_i - back) >> CHUNK_SHIFT)
            band = (dchunk >= 0) & (dchunk <= B_PREV_CHUNKS)
            for hh in range(nh):
                h = hp * nh + hh
                row = relt_ref[pl.ds(h, 1), :] * LOG2E
                seg0 = jnp.broadcast_to(row[:, :LANE], (tq, LANE))
                seg1 = jnp.broadcast_to(row[:, LANE:2 * LANE], (tq, LANE))
                t256 = t256_ref[h] * LOG2E
                for c in range(tq // LANE):
                    cs = slice(c * LANE, (c + 1) * LANE)
                    rc = r[:, cs]
                    lo = rc & (LANE - 1)
                    bias = jnp.where(rc < LANE, _lane_gather(seg0, lo),
                                     jnp.where(rc < 2 * LANE, _lane_gather(seg1, lo), t256))
                    bias_ref[hh, :, j * tq + c * LANE:j * tq + (c + 1) * LANE] = jnp.where(band[:, cs], bias, -jnp.inf)

    if consecutive:
        pl.when(i == 0)(build_bias)
    else:
        build_bias()

    qp = q_ref[...]
    lane_head = lax.broadcasted_iota(jnp.int32, (1, LANE), 1) >> CHUNK_SHIFT
    out = jnp.zeros((tq, LANE), F32)
    for hh in range(nh):
        mine = lane_head == hh
        qm = jnp.where(mine, qp, jnp.zeros_like(qp))
        parts = []
        for j in range(ATT_B_NKB):
            sj = lax.dot_general(qm, k_refs[j][...], NT_DIMS, preferred_element_type=F32)
            sj = sj + bias_ref[hh, :, j * tq:(j + 1) * tq]
            if j < ATT_B_NKB - 1:
                sj = sj + jnp.where(i - (ATT_B_NKB - 1) + j >= 0, 0.0, -jnp.inf)
            parts.append(sj)
        s = jnp.concatenate(parts, axis=1)
        p = jnp.exp2(s - jnp.max(s, axis=1, keepdims=True)).astype(BF16)
        acc = jnp.zeros((tq, LANE), F32)
        for j in range(ATT_B_NKB):
            vj = v_refs[j][...]
            vm = jnp.where(mine, vj, jnp.ones_like(vj))
            acc = acc + jnp.dot(p[:, j * tq:(j + 1) * tq], vm, preferred_element_type=F32)
        rowsum = pltpu.roll(acc, B_HEAD_DIM, 1)
        out = out + jnp.where(mine, acc / rowsum, 0.0)
    o_ref[...] = out.astype(o_ref.dtype)


def band_attention(qkv, pos, rel_table, *, consecutive):
    s = qkv.shape[0]
    tq = ATT_B_TQ
    hw = B_HEADS * B_HEAD_DIM
    npair = hw // LANE
    assert LANE // B_HEAD_DIM == 2
    nrel = 2 * B_REL_CLIP + 1
    relt = jnp.zeros((B_HEADS, 3 * LANE), F32).at[:, :nrel].set(rel_table.T)
    t256 = rel_table[nrel - 1]

    def kv_spec(j, base):
        return pl.BlockSpec((tq, LANE),
                            lambda hp, i: (jnp.maximum(i - (ATT_B_NKB - 1) + j, 0), base + hp))

    return pl.pallas_call(
        functools.partial(_attn_b_kernel, consecutive=consecutive),
        name="band_attention",
        grid=(npair, s // tq),
        in_specs=[
            pl.BlockSpec(memory_space=pltpu.SMEM),
            pl.BlockSpec((tq, LANE), lambda hp, i: (i, hp)),
            kv_spec(0, npair), kv_spec(1, npair), kv_spec(2, npair),
            kv_spec(0, 2 * npair), kv_spec(1, 2 * npair), kv_spec(2, 2 * npair),
            pl.BlockSpec((tq, 1), lambda hp, i: (i, 0)),
            pl.BlockSpec((s // tq, 1, tq), lambda hp, i: (0, 0, 0)),
            pl.BlockSpec((B_HEADS, 3 * LANE), lambda hp, i: (0, 0)),
        ],
        out_specs=pl.BlockSpec((tq, LANE), lambda hp, i: (i, hp)),
        out_shape=jax.ShapeDtypeStruct((s, hw), BF16),
        scratch_shapes=[pltpu.VMEM((LANE // B_HEAD_DIM, tq, ATT_B_NKB * tq), F32)],
        compiler_params=_params(("parallel", "arbitrary")),
    )(t256, qkv, qkv, qkv, qkv, qkv, qkv, qkv, pos.reshape(s, 1), pos.reshape(s // tq, 1, tq), relt)


ATT_C_T = 512
ATT_C_HEADS = 2


def _attn_c_kernel(q_ref, kn_ref, kr_ref, v_ref, o_ref, m_ref, l_ref, acc_ref):
    t, nh = ATT_C_T, ATT_C_HEADS
    iq = pl.program_id(1)
    m_ref[...] = jnp.full(m_ref.shape, NEG_BIG, F32)
    l_ref[...] = jnp.zeros(l_ref.shape, F32)
    acc_ref[...] = jnp.zeros(acc_ref.shape, F32)

    def tile(kt, diag):
        k0 = pl.multiple_of(kt * t, t)
        kr_t = kr_ref[pl.ds(k0, t), :]
        for hh in range(nh):
            kcat = jnp.concatenate([kn_ref[pl.ds(k0, t), hh * C_NOPE:(hh + 1) * C_NOPE], kr_t], axis=1)
            s = lax.dot_general(q_ref[:, hh * 2 * LANE:(hh + 1) * 2 * LANE], kcat, NT_DIMS,
                                preferred_element_type=F32)
            if diag:
                qc = lax.broadcasted_iota(jnp.int32, (t, t), 0) >> CHUNK_SHIFT
                kc = lax.broadcasted_iota(jnp.int32, (t, t), 1) >> CHUNK_SHIFT
                s = jnp.where(kc <= qc, s, -jnp.inf)
            _softmax_step(s, v_ref[pl.ds(k0, t), hh * C_V:(hh + 1) * C_V], m_ref, l_ref, acc_ref,
                          slice(hh * t, (hh + 1) * t), t)

    def body(kt, carry):
        tile(kt, False)
        return carry

    lax.fori_loop(0, iq, body, 0)
    tile(iq, True)
    for hh in range(nh):
        rows = slice(hh * t, (hh + 1) * t)
        o_ref[:, hh * C_V:(hh + 1) * C_V] = (acc_ref[rows, :] / l_ref[rows, :]).astype(o_ref.dtype)


def mla_attention(qcat, kv, kr):
    s = qcat.shape[0]
    t, nh = ATT_C_T, ATT_C_HEADS
    assert s % t == 0
    return pl.pallas_call(
        _attn_c_kernel,
        name="mla_attention",
        grid=(C_HEADS // nh, s // t),
        in_specs=[
            pl.BlockSpec((t, nh * 2 * LANE), lambda h, i: (i, h)),
            pl.BlockSpec((s, nh * C_NOPE), lambda h, i: (0, h)),
            pl.BlockSpec((s, LANE), lambda h, i: (0, 0)),
            pl.BlockSpec((s, nh * C_V), lambda h, i: (0, C_HEADS // nh + h)),
        ],
        out_specs=pl.BlockSpec((t, nh * C_V), lambda h, i: (i, h)),
        out_shape=jax.ShapeDtypeStruct((s, C_HEADS * C_V), BF16),
        scratch_shapes=[pltpu.VMEM((nh * t, LANE), F32), pltpu.VMEM((nh * t, LANE), F32),
                        pltpu.VMEM((nh * t, C_V), F32)],
        compiler_params=_params(("parallel", "arbitrary")),
    )(qcat, kv, kr, kv)


def _rope_tables(pos):
    half = ROPE_DIM // 2
    inv = ROPE_BASE ** (-jnp.arange(half, dtype=F32) * 2.0 / ROPE_DIM)
    ang = pos.astype(F32)[:, None] * inv
    cos, sin = jnp.cos(ang), jnp.sin(ang)
    s = pos.shape[0]
    z = jnp.zeros((s, half), F32)
    c = jnp.concatenate([cos, cos, jnp.ones((s, LANE - ROPE_DIM), F32)], axis=1)
    s1 = jnp.concatenate([-sin, z, z, z], axis=1)
    s2 = jnp.concatenate([z, sin, z, z], axis=1)
    return c, s1, s2


def _query_scale(n_query, n_total, scale):
    return jnp.concatenate([jnp.full((n_query,), scale * LOG2E, F32), jnp.ones((n_total - n_query,), F32)])


def _by_position_layout(consecutive, fn, *args):
    return lax.cond(consecutive, functools.partial(fn, consecutive=True),
                    functools.partial(fn, consecutive=False), *args)


def _mixer_a(h, gain, pos, consecutive, tables, w_in, w_out, t5_table):
    s = h.shape[0]
    d = h.shape[1]
    nq = A_HEADS * A_HEAD_DIM
    nkv = A_KV_HEADS * A_HEAD_DIM
    nqi = IDX_HEADS * IDX_DIM
    o_qkv = nq + 2 * nkv
    o_ki = o_qkv + nqi + IDX_DIM
    w = w_in.astype(BF16)
    w_wi = jnp.zeros((d, LANE), BF16).at[:, :IDX_HEADS].set(w[:, o_ki:o_ki + IDX_HEADS])
    qkv = norm_mm(h, 0, d, gain, w[:, :o_qkv], tn=1024, out_dtype=BF16, name="a_qkv_proj",
                  col_scale=_query_scale(nq, o_qkv, A_HEAD_DIM ** -0.5))
    n_qiki = nqi + IDX_DIM
    qiki = norm_mm(h, 0, d, gain, w[:, o_qkv:o_ki], tn=n_qiki, out_dtype=BF16,
                   rope=(True,) * (n_qiki // LANE), tables=tables, name="a_idx_proj")
    wi = norm_mm(h, 0, d, gain, w_wi, tn=LANE, out_dtype=F32, name="a_idxw_proj")
    mask = dsa_indexer(qiki, wi, topk=min(IDX_TOPK_MAX, s // 4))
    o = _by_position_layout(consecutive, dsa_attention, qkv, mask, pos, t5_table)
    return mm_residual(o, w_out.astype(BF16), h, tn=w_out.shape[1])


def _mixer_b(h, gain, pos, consecutive, w_in, rel_table, w_out):
    d = h.shape[1]
    n = w_in.shape[1]
    qkv = norm_mm(h, 0, d, gain, w_in.astype(BF16), tn=1024, out_dtype=BF16, name="b_qkv_proj",
                  col_scale=_query_scale(n // 3, n, B_HEAD_DIM ** -0.5))
    o = _by_position_layout(consecutive, band_attention, qkv, pos, rel_table)
    return mm_residual(o, w_out.astype(BF16), h, tn=w_out.shape[1])


def _mixer_c(h, gain, tables, w_down, g_q, g_kv, w_uq, w_ukv, w_out):
    d = h.shape[1]
    lq = g_q.shape[0]
    lkv = g_kv.shape[0]
    n_down = lq + lkv + LANE
    wd = jnp.zeros((d, n_down), BF16).at[:, :w_down.shape[1]].set(w_down.astype(BF16))
    down = norm_mm(h, 0, d, gain, wd, tn=n_down, out_dtype=F32,
                   rope=(False,) * ((lq + lkv) // LANE) + (True,), tables=tables, name="c_down_proj")
    wq = w_uq.astype(BF16).reshape(lq, C_HEADS, C_NOPE + C_ROPE)
    wq = jnp.pad(wq, ((0, 0), (0, 0), (0, 2 * LANE - C_NOPE - C_ROPE))).reshape(lq, C_HEADS * 2 * LANE)
    nqc = wq.shape[1]
    qcat = norm_mm(down, 0, lq, g_q, wq, tn=1024, out_dtype=BF16, rope=(False, True) * 4, tables=tables,
                   name="c_uq_proj", col_scale=_query_scale(nqc, nqc, (C_NOPE + C_ROPE) ** -0.5))
    wkv = w_ukv.astype(BF16).reshape(lkv, C_HEADS, 2, C_NOPE).transpose(0, 2, 1, 3).reshape(lkv, -1)
    assert lq == lkv
    kv = norm_mm(down, 1, lkv, g_kv, wkv, tn=1024, out_dtype=BF16, name="c_ukv_proj")
    kr = down[:, lq + lkv:].astype(BF16)
    o = mla_attention(qcat, kv, kr)
    return mm_residual(o, w_out.astype(BF16), h, tn=w_out.shape[1])


def kernel(x, p, positions, t5_table, a_w_in, a_w_out, b_w_in, b_rel_table, b_w_out, c_w_down, c_q_norm,
           c_kv_norm, c_w_uq, c_w_ukv, c_w_out, attn_norm, ffn_norm, ffn_w_in, ffn_w_out, ple_norm,
           ple_w_gate, ple_w_proj, final_norm):
    assert x.shape[0] == 1
    depth = attn_norm.shape[0]
    h = x[0]
    pos = positions[0]
    tables = _rope_tables(pos)
    consecutive = jnp.all(pos[1:] - pos[:-1] == 1)
    for i in range(depth):
        j, kind = divmod(i, 3)
        if kind == 0:
            h = _mixer_a(h, attn_norm[i], pos, consecutive, tables, a_w_in[j], a_w_out[j], t5_table)
        elif kind == 1:
            h = _mixer_b(h, attn_norm[i], pos, consecutive, b_w_in[j], b_rel_table[j], b_w_out[j])
        else:
            h = _mixer_c(h, attn_norm[i], tables, c_w_down[j], c_q_norm[j], c_kv_norm[j], c_w_uq[j],
                         c_w_ukv[j], c_w_out[j])
        act = ffn_in(h, ffn_norm[i], ffn_w_in[i].astype(BF16))
        h = mm_residual(act, ffn_w_out[i].astype(BF16), h, tn=1024)
        h = ple(h, ple_norm[i], ple_w_gate[i].astype(BF16), p[i, 0], ple_w_proj[i].astype(BF16))
    return final_rms_norm(h, final_norm)[None]
```

```python
import functools
import math

import jax
import jax.numpy as jnp
from jax import lax
from jax.experimental import pallas as pl
from jax.experimental.pallas import tpu as pltpu

LANE = 128
VMEM_LIMIT_BYTES = 56 * 1024 * 1024

CHUNK = 64
CHUNK_SHIFT = 6
EPS = 1e-6
ROPE_BASE = 10000.0
ROPE_DIM = 64
A_HEADS = 16
A_KV_HEADS = 4
A_GROUP = 4
A_HEAD_DIM = 128
IDX_HEADS = 16
IDX_DIM = 128
IDX_TOPK_MAX = 256
T5_BUCKETS = 32
T5_MAX_DISTANCE = 1024
T5_FAR = 640
B_HEADS = 32
B_HEAD_DIM = 64
B_PREV_CHUNKS = 8
B_REL_CLIP = 128
C_HEADS = 16
C_NOPE = 128
C_ROPE = 64
C_V = 128
NEG_BIG = -1e30
INT_MIN = -(2 ** 31)
LOG2E = math.log2(math.e)

F32 = jnp.float32
BF16 = jnp.bfloat16
NT_DIMS = (((1,), (1,)), ((), ()))


def _params(sem):
    return pltpu.CompilerParams(dimension_semantics=sem, vmem_limit_bytes=VMEM_LIMIT_BYTES)


def _rms(x, g):
    ms = jnp.mean(x * x, axis=-1, keepdims=True)
    return (x * lax.rsqrt(ms + EPS)) * g


def _lane_gather(table, idx):
    return jnp.take_along_axis(table, idx, axis=1, mode="promise_in_bounds")


def _rope_group(y, c, s1, s2):
    return y * c + pltpu.roll(y, 96, 1) * s1 + pltpu.roll(y, 32, 1) * s2


def _norm_mm_kernel(*refs, rope, scaled):
    x_ref, g_ref, w_ref = refs[:3]
    rest = list(refs[3:])
    cs_ref = rest.pop(0) if scaled else None
    c_ref, s1_ref, s2_ref = (rest.pop(0), rest.pop(0), rest.pop(0)) if rope is not None else (None,) * 3
    o_ref, xn_ref = rest

    @pl.when(pl.program_id(1) == 0)
    def _():
        xn_ref[...] = _rms(x_ref[...], g_ref[...]).astype(BF16)

    y = jnp.dot(xn_ref[...], w_ref[...].astype(BF16), preferred_element_type=F32)
    if scaled:
        y = y * cs_ref[...]
    if rope is None:
        o_ref[...] = y.astype(o_ref.dtype)
    else:
        c, s1, s2 = c_ref[...], s1_ref[...], s2_ref[...]
        for gi, on in enumerate(rope):
            sl = slice(gi * LANE, (gi + 1) * LANE)
            yg = y[:, sl]
            if on:
                yg = _rope_group(yg, c, s1, s2)
            o_ref[:, sl] = yg.astype(o_ref.dtype)


def _w_spec(w, layer, k, tn, col0=0):
    if w.ndim == 2:
        return pl.BlockSpec((k, tn), lambda i, j: (0, col0 + j))
    return pl.BlockSpec((None, k, tn), lambda i, j: (layer, 0, col0 + j))


def norm_mm(x, x_col, kx, gain, w, *, tn, out_dtype, name, n=None, layer=None, col0=0, tm=512, rope=None,
            tables=None, col_scale=None):
    s = x.shape[0]
    n = w.shape[-1] if n is None else n
    tm = min(tm, s)
    in_specs = [
        pl.BlockSpec((tm, kx), lambda i, j: (i, x_col)),
        pl.BlockSpec((1, kx), lambda i, j: (0, 0)),
        _w_spec(w, layer, kx, tn, col0),
    ]
    args = [x, gain.reshape(1, kx), w]
    if col_scale is not None:
        in_specs.append(pl.BlockSpec((1, tn), lambda i, j: (0, j)))
        args.append(col_scale.reshape(1, n))
    if rope is not None:
        assert len(rope) == tn // LANE
        in_specs += [pl.BlockSpec((tm, LANE), lambda i, j: (i, 0))] * 3
        args += list(tables)
    return pl.pallas_call(
        functools.partial(_norm_mm_kernel, rope=rope, scaled=col_scale is not None),
        grid=(s // tm, n // tn),
        in_specs=in_specs,
        out_specs=pl.BlockSpec((tm, tn), lambda i, j: (i, j)),
        out_shape=jax.ShapeDtypeStruct((s, n), out_dtype),
        scratch_shapes=[pltpu.VMEM((tm, kx), BF16)],
        compiler_params=_params(("parallel", "arbitrary")),
        name=name,
    )(*args)


def _mm_res_kernel(x_ref, w_ref, r_ref, o_ref):
    o_ref[...] = r_ref[...] + jnp.dot(x_ref[...], w_ref[...].astype(BF16), preferred_element_type=F32)


def mm_residual(x, w, res, *, layer=None, tm=512, tn=512):
    s, k = x.shape
    n = w.shape[-1]
    tm = min(tm, s)
    return pl.pallas_call(
        _mm_res_kernel,
        name="mm_residual",
        grid=(s // tm, n // tn),
        in_specs=[
            pl.BlockSpec((tm, k), lambda i, j: (i, 0)),
            _w_spec(w, layer, k, tn),
            pl.BlockSpec((tm, tn), lambda i, j: (i, j)),
        ],
        out_specs=pl.BlockSpec((tm, tn), lambda i, j: (i, j)),
        out_shape=jax.ShapeDtypeStruct((s, n), F32),
        compiler_params=_params(("parallel", "arbitrary")),
    )(x, w, res)


def _ffn_in_kernel(x_ref, g_ref, wg_ref, wu_ref, o_ref, xn_ref):
    @pl.when(pl.program_id(1) == 0)
    def _():
        xn_ref[...] = _rms(x_ref[...], g_ref[...]).astype(BF16)

    xn = xn_ref[...]
    a = jnp.dot(xn, wg_ref[...].astype(BF16), preferred_element_type=F32)
    u = jnp.dot(xn, wu_ref[...].astype(BF16), preferred_element_type=F32)
    o_ref[...] = (a * jax.nn.sigmoid(a) * u).astype(o_ref.dtype)


def ffn_in(h, gain, w_in, *, layer=None, tm=1024, tn=512):
    s, d = h.shape
    f = w_in.shape[-1] // 2
    tm = min(tm, s)
    nj = f // tn
    return pl.pallas_call(
        _ffn_in_kernel,
        name="ffn_in",
        grid=(s // tm, nj),
        in_specs=[
            pl.BlockSpec((tm, d), lambda i, j: (i, 0)),
            pl.BlockSpec((1, d), lambda i, j: (0, 0)),
            _w_spec(w_in, layer, d, tn),
            _w_spec(w_in, layer, d, tn, nj),
        ],
        out_specs=pl.BlockSpec((tm, tn), lambda i, j: (i, j)),
        out_shape=jax.ShapeDtypeStruct((s, f), BF16),
        scratch_shapes=[pltpu.VMEM((tm, d), BF16)],
        compiler_params=_params(("parallel", "arbitrary")),
    )(h, gain.reshape(1, d), w_in, w_in)


def _ple_kernel(x_ref, g_ref, wg_ref, p_ref, wp_ref, h_ref, o_ref, xn_ref, pb_ref):
    @pl.when(pl.program_id(1) == 0)
    def _():
        xn_ref[...] = _rms(x_ref[...], g_ref[...]).astype(BF16)
        pb_ref[...] = p_ref[...].astype(BF16)

    gate = jax.nn.sigmoid(jnp.dot(xn_ref[...], wg_ref[...].astype(BF16), preferred_element_type=F32))
    proj = jnp.dot(pb_ref[...], wp_ref[...].astype(BF16), preferred_element_type=F32)
    o_ref[...] = h_ref[...] + gate * proj


def ple(h, gain, w_gate, p, w_proj, *, layer, tm=512, tn=1024):
    s, d = h.shape
    pd = p.shape[-1]
    tm = min(tm, s)
    return pl.pallas_call(
        _ple_kernel,
        name="ple",
        grid=(s // tm, d // tn),
        in_specs=[
            pl.BlockSpec((tm, d), lambda i, j: (i, 0)),
            pl.BlockSpec((1, d), lambda i, j: (0, 0)),
            _w_spec(w_gate, layer, d, tn),
            pl.BlockSpec((None, None, tm, pd), lambda i, j: (layer, 0, i, 0)),
            _w_spec(w_proj, layer, pd, tn),
            pl.BlockSpec((tm, tn), lambda i, j: (i, j)),
        ],
        out_specs=pl.BlockSpec((tm, tn), lambda i, j: (i, j)),
        out_shape=jax.ShapeDtypeStruct((s, d), F32),
        scratch_shapes=[pltpu.VMEM((tm, d), BF16), pltpu.VMEM((tm, pd), BF16)],
        compiler_params=_params(("parallel", "arbitrary")),
    )(h, gain.reshape(1, d), w_gate, p, w_proj, h)


def _final_norm_kernel(x_ref, g_ref, o_ref):
    o_ref[...] = _rms(x_ref[...], g_ref[...])


def final_rms_norm(h, gain, *, tm=512):
    s, d = h.shape
    tm = min(tm, s)
    return pl.pallas_call(
        _final_norm_kernel,
        name="final_norm",
        grid=(s // tm,),
        in_specs=[pl.BlockSpec((tm, d), lambda i: (i, 0)), pl.BlockSpec((1, d), lambda i: (0, 0))],
        out_specs=pl.BlockSpec((tm, d), lambda i: (i, 0)),
        out_shape=jax.ShapeDtypeStruct((s, d), F32),
        compiler_params=_params(("parallel",)),
    )(h, gain.reshape(1, d))


IDX_TQ = 256
IDX_KC = 256


def _orderable(x):
    b = pltpu.bitcast(x, jnp.int32)
    return jnp.where(b < 0, b ^ jnp.int32(0x7FFFFFFF), b)


def _indexer_kernel(qi_ref, ki_ref, wi_ref, mask_ref, keys_ref, *, topk, n_kblocks):
    tq, kc = IDX_TQ, IDX_KC
    i = pl.program_id(0)
    nkc = (i + 1) * (tq // kc)
    n_idx_bits = (n_kblocks * kc - 1).bit_length()
    w_t = (wi_ref[...] * (IDX_HEADS ** -0.5 * IDX_DIM ** -0.5)).T
    qchunk = (i * tq + lax.broadcasted_iota(jnp.int32, (kc, tq), 1)) >> CHUNK_SHIFT
    krow = lax.broadcasted_iota(jnp.int32, (kc, tq), 0)

    def valid_of(c):
        return ((c * kc + krow) >> CHUNK_SHIFT) <= qchunk

    def score_body(c, carry):
        k0 = pl.multiple_of(c * kc, kc)
        kblk = ki_ref[pl.ds(k0, kc), :].astype(BF16)
        acc = jnp.zeros((kc, tq), F32)
        for h in range(IDX_HEADS):
            s = lax.dot_general(kblk, qi_ref[:, h * IDX_DIM:(h + 1) * IDX_DIM], NT_DIMS,
                                preferred_element_type=F32)
            acc = acc + jnp.maximum(s, 0.0) * w_t[h:h + 1, :]
        keys_ref[pl.ds(k0, kc), :] = jnp.where(valid_of(c), _orderable(acc), jnp.int32(INT_MIN))
        return carry

    lax.fori_loop(0, nkc, score_body, 0)

    def count(pred_of):
        def body(c, cnt):
            k0 = pl.multiple_of(c * kc, kc)
            hit = pred_of(c, keys_ref[pl.ds(k0, kc), :]).astype(jnp.int32)
            return cnt + hit.reshape(kc // 8, 8, tq).sum(axis=0)
        cnt = lax.fori_loop(0, nkc, body, jnp.zeros((8, tq), jnp.int32))
        return cnt.sum(axis=0, keepdims=True)

    def bit_step(b, state):
        thr, cnt = state
        cand = thr + lax.shift_left(jnp.int32(1), 31 - b)
        c_cand = count(lambda c, keys: keys >= cand)
        take = c_cand >= topk
        return jnp.where(take, cand, thr), jnp.where(take, c_cand, cnt)

    thr, cnt = lax.fori_loop(
        0, 32, bit_step,
        (jnp.full((1, tq), INT_MIN, jnp.int32), jnp.full((1, tq), nkc * kc, jnp.int32)))

    def tie_limit():
        need = topk - count(lambda c, keys: keys > thr)
        def bit_body(b, x):
            cand = x + lax.shift_left(jnp.int32(1), n_idx_bits - 1 - b)
            below = count(lambda c, keys: (keys == thr) & ((c * kc + krow) < cand))
            return jnp.where(below < need, cand, x)
        return lax.fori_loop(0, n_idx_bits, bit_body, jnp.zeros((1, tq), jnp.int32))

    def write_masks(selected):
        def mask_body(c, carry):
            k0 = pl.multiple_of(c * kc, kc)
            sel = selected(c, keys_ref[pl.ds(k0, kc), :]) & valid_of(c)
            mask_ref[c] = jnp.where(sel, 0.0, -jnp.inf).astype(F32).T
            return carry
        lax.fori_loop(0, nkc, mask_body, 0)

    has_ties = jnp.max(jnp.where((cnt > topk) & (thr > INT_MIN), 1, 0)) > 0

    @pl.when(has_ties)
    def _():
        tie_last = tie_limit()
        write_masks(lambda c, keys: (keys > thr) | ((keys == thr) & ((c * kc + krow) <= tie_last)))

    @pl.when(jnp.logical_not(has_ties))
    def _():
        write_masks(lambda c, keys: keys >= thr)

    def fill_body(c, carry):
        mask_ref[c] = jnp.full((tq, kc), -jnp.inf, F32)
        return carry

    lax.fori_loop(nkc, n_kblocks, fill_body, 0)


def dsa_indexer(qi, kiwi, *, topk):
    s = qi.shape[0]
    tq, kc = IDX_TQ, IDX_KC
    nkb = s // kc
    return pl.pallas_call(
        functools.partial(_indexer_kernel, topk=topk, n_kblocks=nkb),
        name="dsa_indexer",
        grid=(s // tq,),
        in_specs=[
            pl.BlockSpec((tq, IDX_HEADS * IDX_DIM), lambda i: (i, 0)),
            pl.BlockSpec((s, IDX_DIM), lambda i: (0, 0)),
            pl.BlockSpec((tq, LANE), lambda i: (i, 1)),
        ],
        out_specs=pl.BlockSpec((nkb, tq, kc), lambda i: (0, i, 0)),
        out_shape=jax.ShapeDtypeStruct((nkb, s, kc), F32),
        scratch_shapes=[pltpu.VMEM((s, tq), jnp.int32)],
        compiler_params=_params(("parallel",)),
    )(qi, kiwi, kiwi)


ATT_A_TQ = 256
ATT_A_KB = 2
ATT_A_NEAR = (T5_FAR + ATT_A_KB * IDX_KC - 2) // IDX_KC + 1


def _t5_bucket(rel):
    nb = T5_BUCKETS // 2
    max_exact = nb // 2
    ret = jnp.where(rel > 0, nb, 0)
    n = jnp.abs(rel)
    nf = jnp.maximum(n, 1).astype(F32)
    large = max_exact + (jnp.log(nf / max_exact) / math.log(T5_MAX_DISTANCE / max_exact)
                         * (nb - max_exact)).astype(jnp.int32)
    large = jnp.minimum(large, nb - 1)
    return ret + jnp.where(n < max_exact, n, large)


def _softmax_step(s, v, m_ref, l_ref, acc_ref, rows, tk, shift=None):
    d = v.shape[1]
    m_prev = m_ref[rows, :]
    mx = jnp.max(s, axis=1, keepdims=True)
    if shift is not None:
        mx = mx + shift
    m_next = jnp.maximum(m_prev, mx)
    pivot = m_next if shift is None else m_next - shift
    p = jnp.exp2(s - jnp.concatenate([pivot] * (tk // LANE), axis=1)).astype(BF16)
    alpha = jnp.exp2(m_prev - m_next)
    v_ones = jnp.concatenate([v, jnp.ones((tk, LANE), BF16)], axis=1)
    pv = jnp.dot(p, v_ones, preferred_element_type=F32)
    m_ref[rows, :] = m_next
    l_ref[rows, :] = alpha * l_ref[rows, :] + pv[:, d:]
    acc_ref[rows, :] = acc_ref[rows, :] * alpha + pv[:, :d]


def _attn_a_kernel(qmin_ref, kmax_ref, t5s_ref, q_ref, k_ref, v_ref, mask_ref, posq_ref, posk_ref,
                   t5t_ref, o_ref, m_ref, l_ref, acc_ref, nbias_ref, *, consecutive):
    tq, kc, nb = ATT_A_TQ, IDX_KC, ATT_A_KB
    tk = nb * kc
    g = pl.program_id(0)
    i = pl.program_id(1)

    def head_bias(r, bucket):
        row = t5t_ref[pl.ds(g * A_GROUP + r, 1), :] * LOG2E
        tbl = jnp.broadcast_to(row, (tq, LANE))
        return jnp.concatenate([_lane_gather(tbl, bucket[:, c * LANE:(c + 1) * LANE])
                                for c in range(tk // LANE)], axis=1)

    if consecutive:
        @pl.when(i == 0)
        def _():
            rel0 = (lax.broadcasted_iota(jnp.int32, (tq, tk), 1)
                    - lax.broadcasted_iota(jnp.int32, (tq, tk), 0))
            for d in range(ATT_A_NEAR):
                bucket = _t5_bucket(rel0 + (d - ATT_A_NEAR + 1) * kc)
                for r in range(A_GROUP):
                    nbias_ref[d, r] = head_bias(r, bucket)

    m_ref[...] = jnp.full(m_ref.shape, NEG_BIG, F32)
    l_ref[...] = jnp.zeros(l_ref.shape, F32)
    acc_ref[...] = jnp.zeros(acc_ref.shape, F32)
    nkt = ((i + 1) * (tq // kc) + nb - 1) // nb
    qmin = qmin_ref[i]

    def is_far(kt):
        ktc = jnp.minimum(kt, nkt - 1) * nb
        kmax = kmax_ref[ktc]
        for b in range(1, nb):
            kmax = jnp.maximum(kmax, kmax_ref[ktc + b])
        return (kt < nkt) & ((qmin - kmax) >= T5_FAR)

    n_far = lax.while_loop(is_far, lambda kt: kt + 1, jnp.int32(0))

    def tile(kt, bias_of, shift_of):
        k0 = pl.multiple_of(kt * tk, tk)
        k_t = k_ref[pl.ds(k0, tk), :]
        v_t = v_ref[pl.ds(k0, tk), :]
        base = jnp.concatenate([mask_ref[kt * nb + b] for b in range(nb)], axis=1)
        for r in range(A_GROUP):
            rows = slice(r * tq, (r + 1) * tq)
            q_r = q_ref[:, r * A_HEAD_DIM:(r + 1) * A_HEAD_DIM]
            s = lax.dot_general(q_r, k_t, NT_DIMS, preferred_element_type=F32) + bias_of(r, base)
            _softmax_step(s, v_t, m_ref, l_ref, acc_ref, rows, tk, shift_of(r))

    def far_body(kt, carry):
        tile(kt, lambda r, base: base,
             lambda r: t5s_ref[T5_BUCKETS // 2 - 1, g * A_GROUP + r] * LOG2E)
        return carry

    def near_body(kt, carry):
        if consecutive:
            d = kt * nb - i * (tq // kc) + (ATT_A_NEAR - 1)
            tile(kt, lambda r, base: base + nbias_ref[d, r], lambda r: None)
        else:
            pk = jnp.concatenate([posk_ref[kt * nb + b] for b in range(nb)], axis=1)
            bucket = _t5_bucket(pk - posq_ref[...])
            tile(kt, lambda r, base: base + head_bias(r, bucket), lambda r: None)
        return carry

    lax.fori_loop(0, n_far, far_body, 0)
    lax.fori_loop(n_far, nkt, near_body, 0)
    for r in range(A_GROUP):
        rows = slice(r * tq, (r + 1) * tq)
        o_ref[:, r * A_HEAD_DIM:(r + 1) * A_HEAD_DIM] = (acc_ref[rows, :] / l_ref[rows, :]).astype(o_ref.dtype)


def dsa_attention(qkv, mask, pos, t5_table, *, consecutive):
    s = qkv.shape[0]
    tq, tk = ATT_A_TQ, IDX_KC
    assert (s // tk) % ATT_A_KB == 0
    gw = A_GROUP * A_HEAD_DIM
    qmin = pos.reshape(s // tq, tq).min(axis=1)
    kmax = pos.reshape(s // tk, tk).max(axis=1)
    t5t = jnp.zeros((A_HEADS, LANE), F32).at[:, :T5_BUCKETS].set(t5_table.T)
    kblk0 = (A_HEADS * A_HEAD_DIM) // A_HEAD_DIM
    smem = pl.BlockSpec(memory_space=pltpu.SMEM)
    nbias_shape = (ATT_A_NEAR, A_GROUP, tq, ATT_A_KB * tk) if consecutive else (1, 1, 8, LANE)
    return pl.pallas_call(
        functools.partial(_attn_a_kernel, consecutive=consecutive),
        name="dsa_attention",
        grid=(A_KV_HEADS, s // tq),
        in_specs=[
            smem, smem, smem,
            pl.BlockSpec((tq, gw), lambda g, i: (i, g)),
            pl.BlockSpec((s, A_HEAD_DIM), lambda g, i: (0, kblk0 + g)),
            pl.BlockSpec((s, A_HEAD_DIM), lambda g, i: (0, kblk0 + A_KV_HEADS + g)),
            pl.BlockSpec((s // tk, tq, tk), lambda g, i: (0, i, 0)),
            pl.BlockSpec((tq, 1), lambda g, i: (i, 0)),
            pl.BlockSpec((s // tk, 1, tk), lambda g, i: (0, 0, 0)),
            pl.BlockSpec((A_HEADS, LANE), lambda g, i: (0, 0)),
        ],
        out_specs=pl.BlockSpec((tq, gw), lambda g, i: (i, g)),
        out_shape=jax.ShapeDtypeStruct((s, A_HEADS * A_HEAD_DIM), BF16),
        scratch_shapes=[
            pltpu.VMEM((A_GROUP * tq, LANE), F32),
            pltpu.VMEM((A_GROUP * tq, LANE), F32),
            pltpu.VMEM((A_GROUP * tq, A_HEAD_DIM), F32),
            pltpu.VMEM(nbias_shape, F32),
        ],
        compiler_params=_params(("parallel", "arbitrary")),
    )(qmin, kmax, t5_table, qkv, qkv, qkv, mask, pos.reshape(s, 1), pos.reshape(s // tk, 1, tk), t5t)


ATT_B_TQ = 256
ATT_B_NKB = 3


def _attn_b_kernel(t256_ref, q_ref, k0_ref, k1_ref, k2_ref, v0_ref, v1_ref, v2_ref, posq_ref, posk_ref,
                   relt_ref, o_ref, bias_ref, *, consecutive):
    tq = ATT_B_TQ
    nh = LANE // B_HEAD_DIM
    hp = pl.program_id(0)
    i = pl.program_id(1)
    k_refs = (k0_ref, k1_ref, k2_ref)
    v_refs = (v0_ref, v1_ref, v2_ref)

    def build_bias():
        row_i = lax.broadcasted_iota(jnp.int32, (tq, tq), 0)
        col_i = lax.broadcasted_iota(jnp.int32, (tq, tq), 1)
        for j in range(ATT_B_NKB):
            back = (ATT_B_NKB - 1 - j) * tq
            if consecutive:
                rel = row_i - (col_i - back)
            else:
                rel = posq_ref[...] - posk_ref[jnp.maximum(i - (ATT_B_NKB - 1) + j, 0)]
            r = jnp.clip(rel, -B_REL_CLIP, B_REL_CLIP) + B_REL_CLIP
            dchunk = (row_i >> CHUNK_SHIFT) - ((col_i - back) >> CHUNK_SHIFT)
            band = (dchunk >= 0) & (dchunk <= B_PREV_CHUNKS)
            for hh in range(nh):
                h = hp * nh + hh
                row = relt_ref[pl.ds(h, 1), :] * LOG2E
                seg0 = jnp.broadcast_to(row[:, :LANE], (tq, LANE))
                seg1 = jnp.broadcast_to(row[:, LANE:2 * LANE], (tq, LANE))
                t256 = t256_ref[h] * LOG2E
                for c in range(tq // LANE):
                    cs = slice(c * LANE, (c + 1) * LANE)
                    rc = r[:, cs]
                    lo = rc & (LANE - 1)
                    bias = jnp.where(rc < LANE, _lane_gather(seg0, lo),
                                     jnp.where(rc < 2 * LANE, _lane_gather(seg1, lo), t256))
                    bias_ref[hh, :, j * tq + c * LANE:j * tq + (c + 1) * LANE] = jnp.where(band[:, cs], bias, -jnp.inf)

    if consecutive:
        pl.when(i == 0)(build_bias)
    else:
        build_bias()

    qp = q_ref[...]
    lane_head = lax.broadcasted_iota(jnp.int32, (1, LANE), 1) >> CHUNK_SHIFT
    out = jnp.zeros((tq, LANE), F32)
    for hh in range(nh):
        mine = lane_head == hh
        qm = jnp.where(mine, qp, jnp.zeros_like(qp))
        parts = []
        for j in range(ATT_B_NKB):
            sj = lax.dot_general(qm, k_refs[j][...], NT_DIMS, preferred_element_type=F32)
            sj = sj + bias_ref[hh, :, j * tq:(j + 1) * tq]
            if j < ATT_B_NKB - 1:
                sj = sj + jnp.where(i - (ATT_B_NKB - 1) + j >= 0, 0.0, -jnp.inf)
            parts.append(sj)
        s = jnp.concatenate(parts, axis=1)
        p = jnp.exp2(s - jnp.max(s, axis=1, keepdims=True)).astype(BF16)
        acc = jnp.zeros((tq, LANE), F32)
        for j in range(ATT_B_NKB):
            vj = v_refs[j][...]
            vm = jnp.where(mine, vj, jnp.ones_like(vj))
            acc = acc + jnp.dot(p[:, j * tq:(j + 1) * tq], vm, preferred_element_type=F32)
        rowsum = pltpu.roll(acc, B_HEAD_DIM, 1)
        out = out + jnp.where(mine, acc / rowsum, 0.0)
    o_ref[...] = out.astype(o_ref.dtype)


def band_attention(qkv, pos, rel_table, *, consecutive):
    s = qkv.shape[0]
    tq = ATT_B_TQ
    hw = B_HEADS * B_HEAD_DIM
    npair = hw // LANE
    assert LANE // B_HEAD_DIM == 2
    nrel = 2 * B_REL_CLIP + 1
    relt = jnp.zeros((B_HEADS, 3 * LANE), F32).at[:, :nrel].set(rel_table.T)
    t256 = rel_table[nrel - 1]

    def kv_spec(j, base):
        return pl.BlockSpec((tq, LANE),
                            lambda hp, i: (jnp.maximum(i - (ATT_B_NKB - 1) + j, 0), base + hp))

    return pl.pallas_call(
        functools.partial(_attn_b_kernel, consecutive=consecutive),
        name="band_attention",
        grid=(npair, s // tq),
        in_specs=[
            pl.BlockSpec(memory_space=pltpu.SMEM),
            pl.BlockSpec((tq, LANE), lambda hp, i: (i, hp)),
            kv_spec(0, npair), kv_spec(1, npair), kv_spec(2, npair),
            kv_spec(0, 2 * npair), kv_spec(1, 2 * npair), kv_spec(2, 2 * npair),
            pl.BlockSpec((tq, 1), lambda hp, i: (i, 0)),
            pl.BlockSpec((s // tq, 1, tq), lambda hp, i: (0, 0, 0)),
            pl.BlockSpec((B_HEADS, 3 * LANE), lambda hp, i: (0, 0)),
        ],
        out_specs=pl.BlockSpec((tq, LANE), lambda hp, i: (i, hp)),
        out_shape=jax.ShapeDtypeStruct((s, hw), BF16),
        scratch_shapes=[pltpu.VMEM((LANE // B_HEAD_DIM, tq, ATT_B_NKB * tq), F32)],
        compiler_params=_params(("parallel", "arbitrary")),
    )(t256, qkv, qkv, qkv, qkv, qkv, qkv, qkv, pos.reshape(s, 1), pos.reshape(s // tq, 1, tq), relt)


ATT_C_T = 512
ATT_C_HEADS = 2


def _attn_c_kernel(q_ref, kn_ref, kr_ref, v_ref, o_ref, m_ref, l_ref, acc_ref):
    t, nh = ATT_C_T, ATT_C_HEADS
    iq = pl.program_id(1)
    m_ref[...] = jnp.full(m_ref.shape, NEG_BIG, F32)
    l_ref[...] = jnp.zeros(l_ref.shape, F32)
    acc_ref[...] = jnp.zeros(acc_ref.shape, F32)

    def tile(kt, diag):
        k0 = pl.multiple_of(kt * t, t)
        kr_t = kr_ref[pl.ds(k0, t), :]
        for hh in range(nh):
            kcat = jnp.concatenate([kn_ref[pl.ds(k0, t), hh * C_NOPE:(hh + 1) * C_NOPE], kr_t], axis=1)
            s = lax.dot_general(q_ref[:, hh * 2 * LANE:(hh + 1) * 2 * LANE], kcat, NT_DIMS,
                                preferred_element_type=F32)
            if diag:
                qc = lax.broadcasted_iota(jnp.int32, (t, t), 0) >> CHUNK_SHIFT
                kc = lax.broadcasted_iota(jnp.int32, (t, t), 1) >> CHUNK_SHIFT
                s = jnp.where(kc <= qc, s, -jnp.inf)
            _softmax_step(s, v_ref[pl.ds(k0, t), hh * C_V:(hh + 1) * C_V], m_ref, l_ref, acc_ref,
                          slice(hh * t, (hh + 1) * t), t)

    def body(kt, carry):
        tile(kt, False)
        return carry

    lax.fori_loop(0, iq, body, 0)
    tile(iq, True)
    for hh in range(nh):
        rows = slice(hh * t, (hh + 1) * t)
        o_ref[:, hh * C_V:(hh + 1) * C_V] = (acc_ref[rows, :] / l_ref[rows, :]).astype(o_ref.dtype)


def mla_attention(qcat, kv, kr):
    s = qcat.shape[0]
    t, nh = ATT_C_T, ATT_C_HEADS
    assert s % t == 0
    return pl.pallas_call(
        _attn_c_kernel,
        name="mla_attention",
        grid=(C_HEADS // nh, s // t),
        in_specs=[
            pl.BlockSpec((t, nh * 2 * LANE), lambda h, i: (i, h)),
            pl.BlockSpec((s, nh * C_NOPE), lambda h, i: (0, h)),
            pl.BlockSpec((s, LANE), lambda h, i: (0, 0)),
            pl.BlockSpec((s, nh * C_V), lambda h, i: (0, C_HEADS // nh + h)),
        ],
        out_specs=pl.BlockSpec((t, nh * C_V), lambda h, i: (i, h)),
        out_shape=jax.ShapeDtypeStruct((s, C_HEADS * C_V), BF16),
        scratch_shapes=[pltpu.VMEM((nh * t, LANE), F32), pltpu.VMEM((nh * t, LANE), F32),
                        pltpu.VMEM((nh * t, C_V), F32)],
        compiler_params=_params(("parallel", "arbitrary")),
    )(qcat, kv, kr, kv)


def _rope_tables(pos):
    half = ROPE_DIM // 2
    inv = ROPE_BASE ** (-jnp.arange(half, dtype=F32) * 2.0 / ROPE_DIM)
    ang = pos.astype(F32)[:, None] * inv
    cos, sin = jnp.cos(ang), jnp.sin(ang)
    s = pos.shape[0]
    z = jnp.zeros((s, half), F32)
    c = jnp.concatenate([cos, cos, jnp.ones((s, LANE - ROPE_DIM), F32)], axis=1)
    s1 = jnp.concatenate([-sin, z, z, z], axis=1)
    s2 = jnp.concatenate([z, sin, z, z], axis=1)
    return c, s1, s2


def _query_scale(n_query, n_total, scale):
    return jnp.concatenate([jnp.full((n_query,), scale * LOG2E, F32), jnp.ones((n_total - n_query,), F32)])


def _by_position_layout(consecutive, fn, *args):
    return lax.cond(consecutive, functools.partial(fn, consecutive=True),
                    functools.partial(fn, consecutive=False), *args)


def _mixer_a(h, gain, pos, consecutive, tables, w_in, w_out, layer, t5_table):
    s = h.shape[0]
    d = h.shape[1]
    nq = A_HEADS * A_HEAD_DIM
    nkv = A_KV_HEADS * A_HEAD_DIM
    nqi = IDX_HEADS * IDX_DIM
    o_qkv = nq + 2 * nkv
    o_ki = o_qkv + nqi
    tn = 1024
    assert o_qkv % tn == 0 and nqi % tn == 0
    qkv = norm_mm(h, 0, d, gain, w_in, layer=layer, n=o_qkv, tn=tn, out_dtype=BF16, name="a_qkv_proj",
                  col_scale=_query_scale(nq, o_qkv, A_HEAD_DIM ** -0.5))
    qi = norm_mm(h, 0, d, gain, w_in, layer=layer, n=nqi, col0=o_qkv // tn, tn=tn, out_dtype=BF16,
                 rope=(True,) * (tn // LANE), tables=tables, name="a_idxq_proj")
    w_kw = jnp.zeros((d, 2 * LANE), BF16).at[:, :IDX_DIM + IDX_HEADS].set(w_in[layer, :, o_ki:].astype(BF16))
    kiwi = norm_mm(h, 0, d, gain, w_kw, tn=2 * LANE, out_dtype=F32, rope=(True, False), tables=tables,
                   name="a_idxk_proj")
    mask = dsa_indexer(qi, kiwi, topk=min(IDX_TOPK_MAX, s // 4))
    o = _by_position_layout(consecutive, dsa_attention, qkv, mask, pos, t5_table)
    return mm_residual(o, w_out, h, layer=layer, tn=1024)


def _mixer_b(h, gain, pos, consecutive, w_in, rel_table, w_out, layer):
    d = h.shape[1]
    n = w_in.shape[-1]
    qkv = norm_mm(h, 0, d, gain, w_in, layer=layer, tn=1024, out_dtype=BF16, name="b_qkv_proj",
                  col_scale=_query_scale(n // 3, n, B_HEAD_DIM ** -0.5))
    o = _by_position_layout(consecutive, band_attention, qkv, pos, rel_table)
    return mm_residual(o, w_out, h, layer=layer, tn=1024)


def _mixer_c(h, gain, tables, w_down, g_q, g_kv, w_uq, w_ukv, w_out, layer):
    d = h.shape[1]
    lq = g_q.shape[0]
    lkv = g_kv.shape[0]
    n_down = lq + lkv + LANE
    wd = jnp.zeros((d, n_down), BF16).at[:, :w_down.shape[1]].set(w_down.astype(BF16))
    down = norm_mm(h, 0, d, gain, wd, tn=n_down, out_dtype=F32,
                   rope=(False,) * ((lq + lkv) // LANE) + (True,), tables=tables, name="c_down_proj")
    wq = w_uq.astype(BF16).reshape(lq, C_HEADS, C_NOPE + C_ROPE)
    wq = jnp.pad(wq, ((0, 0), (0, 0), (0, 2 * LANE - C_NOPE - C_ROPE))).reshape(lq, C_HEADS * 2 * LANE)
    nqc = wq.shape[1]
    qcat = norm_mm(down, 0, lq, g_q, wq, tn=1024, out_dtype=BF16, rope=(False, True) * 4, tables=tables,
                   name="c_uq_proj", col_scale=_query_scale(nqc, nqc, (C_NOPE + C_ROPE) ** -0.5))
    wkv = w_ukv.astype(BF16).reshape(lkv, C_HEADS, 2, C_NOPE).transpose(0, 2, 1, 3).reshape(lkv, -1)
    assert lq == lkv
    kv = norm_mm(down, 1, lkv, g_kv, wkv, tn=1024, out_dtype=BF16, name="c_ukv_proj")
    kr = down[:, lq + lkv:].astype(BF16)
    o = mla_attention(qcat, kv, kr)
    return mm_residual(o, w_out, h, layer=layer, tn=1024)


def kernel(x, p, positions, t5_table, a_w_in, a_w_out, b_w_in, b_rel_table, b_w_out, c_w_down, c_q_norm,
           c_kv_norm, c_w_uq, c_w_ukv, c_w_out, attn_norm, ffn_norm, ffn_w_in, ffn_w_out, ple_norm,
           ple_w_gate, ple_w_proj, final_norm):
    assert x.shape[0] == 1
    depth = attn_norm.shape[0]
    h = x[0]
    pos = positions[0]
    tables = _rope_tables(pos)
    consecutive = jnp.all(pos[1:] - pos[:-1] == 1)
    for i in range(depth):
        j, kind = divmod(i, 3)
        if kind == 0:
            h = _mixer_a(h, attn_norm[i], pos, consecutive, tables, a_w_in, a_w_out, j, t5_table)
        elif kind == 1:
            h = _mixer_b(h, attn_norm[i], pos, consecutive, b_w_in, b_rel_table[j], b_w_out, j)
        else:
            h = _mixer_c(h, attn_norm[i], tables, c_w_down[j], c_q_norm[j], c_kv_norm[j], c_w_uq[j],
                         c_w_ukv[j], c_w_out, j)
        act = ffn_in(h, ffn_norm[i], ffn_w_in, layer=i)
        h = mm_residual(act, ffn_w_out, h, layer=i)
        h = ple(h, ple_norm[i], ple_w_gate, p, ple_w_proj, layer=i)
    return final_rms_norm(h, final_norm)[None]
```

```python
import functools
import math

import jax
import jax.numpy as jnp
from jax import lax
from jax.experimental import pallas as pl
from jax.experimental.pallas import tpu as pltpu

LANE = 128
VMEM_LIMIT_BYTES = 56 * 1024 * 1024

CHUNK = 64
CHUNK_SHIFT = 6
EPS = 1e-6
ROPE_BASE = 10000.0
ROPE_DIM = 64
A_HEADS = 16
A_KV_HEADS = 4
A_GROUP = 4
A_HEAD_DIM = 128
IDX_HEADS = 16
IDX_DIM = 128
IDX_TOPK_MAX = 256
T5_BUCKETS = 32
T5_MAX_DISTANCE = 1024
T5_FAR = 640
B_HEADS = 32
B_HEAD_DIM = 64
B_PREV_CHUNKS = 8
B_REL_CLIP = 128
C_HEADS = 16
C_NOPE = 128
C_ROPE = 64
C_V = 128
NEG_BIG = -1e30
INT_MIN = -(2 ** 31)
LOG2E = math.log2(math.e)

F32 = jnp.float32
BF16 = jnp.bfloat16
NT_DIMS = (((1,), (1,)), ((), ()))


def _params(sem):
    return pltpu.CompilerParams(dimension_semantics=sem, vmem_limit_bytes=VMEM_LIMIT_BYTES)


def _rms(x, g):
    ms = jnp.mean(x * x, axis=-1, keepdims=True)
    return (x * lax.rsqrt(ms + EPS)) * g


def _lane_gather(table, idx):
    return jnp.take_along_axis(table, idx, axis=1, mode="promise_in_bounds")


def _rope_group(y, c, s1, s2):
    return y * c + pltpu.roll(y, 96, 1) * s1 + pltpu.roll(y, 32, 1) * s2


def _norm_mm_kernel(*refs, rope, scaled):
    x_ref, g_ref, w_ref = refs[:3]
    rest = list(refs[3:])
    cs_ref = rest.pop(0) if scaled else None
    c_ref, s1_ref, s2_ref = (rest.pop(0), rest.pop(0), rest.pop(0)) if rope is not None else (None,) * 3
    o_ref, xn_ref = rest

    @pl.when(pl.program_id(1) == 0)
    def _():
        xn_ref[...] = _rms(x_ref[...], g_ref[...]).astype(BF16)

    y = jnp.dot(xn_ref[...], w_ref[...].astype(BF16), preferred_element_type=F32)
    if scaled:
        y = y * cs_ref[...]
    if rope is None:
        o_ref[...] = y.astype(o_ref.dtype)
    else:
        c, s1, s2 = c_ref[...], s1_ref[...], s2_ref[...]
        for gi, on in enumerate(rope):
            sl = slice(gi * LANE, (gi + 1) * LANE)
            yg = y[:, sl]
            if on:
                yg = _rope_group(yg, c, s1, s2)
            o_ref[:, sl] = yg.astype(o_ref.dtype)


def _w_spec(w, layer, k, tn, col0=0):
    if w.ndim == 2:
        return pl.BlockSpec((k, tn), lambda i, j: (0, col0 + j))
    return pl.BlockSpec((None, k, tn), lambda i, j: (layer, 0, col0 + j))


def norm_mm(x, x_col, kx, gain, w, *, tn, out_dtype, name, n=None, layer=None, col0=0, tm=512, rope=None,
            tables=None, col_scale=None):
    s = x.shape[0]
    n = w.shape[-1] if n is None else n
    tm = min(tm, s)
    in_specs = [
        pl.BlockSpec((tm, kx), lambda i, j: (i, x_col)),
        pl.BlockSpec((1, kx), lambda i, j: (0, 0)),
        _w_spec(w, layer, kx, tn, col0),
    ]
    args = [x, gain.reshape(1, kx), w]
    if col_scale is not None:
        in_specs.append(pl.BlockSpec((1, tn), lambda i, j: (0, j)))
        args.append(col_scale.reshape(1, n))
    if rope is not None:
        assert len(rope) == tn // LANE
        in_specs += [pl.BlockSpec((tm, LANE), lambda i, j: (i, 0))] * 3
        args += list(tables)
    return pl.pallas_call(
        functools.partial(_norm_mm_kernel, rope=rope, scaled=col_scale is not None),
        grid=(s // tm, n // tn),
        in_specs=in_specs,
        out_specs=pl.BlockSpec((tm, tn), lambda i, j: (i, j)),
        out_shape=jax.ShapeDtypeStruct((s, n), out_dtype),
        scratch_shapes=[pltpu.VMEM((tm, kx), BF16)],
        compiler_params=_params(("parallel", "arbitrary")),
        name=name,
    )(*args)


def _mm_res_kernel(x_ref, w_ref, r_ref, o_ref):
    o_ref[...] = r_ref[...] + jnp.dot(x_ref[...], w_ref[...].astype(BF16), preferred_element_type=F32)


def mm_residual(x, w, res, *, layer=None, tm=512, tn=512):
    s, k = x.shape
    n = w.shape[-1]
    tm = min(tm, s)
    return pl.pallas_call(
        _mm_res_kernel,
        name="mm_residual",
        grid=(s // tm, n // tn),
        in_specs=[
            pl.BlockSpec((tm, k), lambda i, j: (i, 0)),
            _w_spec(w, layer, k, tn),
            pl.BlockSpec((tm, tn), lambda i, j: (i, j)),
        ],
        out_specs=pl.BlockSpec((tm, tn), lambda i, j: (i, j)),
        out_shape=jax.ShapeDtypeStruct((s, n), F32),
        compiler_params=_params(("parallel", "arbitrary")),
    )(x, w, res)


def _ffn_in_kernel(x_ref, g_ref, wg_ref, wu_ref, o_ref, xn_ref):
    @pl.when(pl.program_id(1) == 0)
    def _():
        xn_ref[...] = _rms(x_ref[...], g_ref[...]).astype(BF16)

    xn = xn_ref[...]
    a = jnp.dot(xn, wg_ref[...].astype(BF16), preferred_element_type=F32)
    u = jnp.dot(xn, wu_ref[...].astype(BF16), preferred_element_type=F32)
    o_ref[...] = (a * jax.nn.sigmoid(a) * u).astype(o_ref.dtype)


def ffn_in(h, gain, w_in, *, layer=None, tm=1024, tn=512):
    s, d = h.shape
    f = w_in.shape[-1] // 2
    tm = min(tm, s)
    nj = f // tn
    return pl.pallas_call(
        _ffn_in_kernel,
        name="ffn_in",
        grid=(s // tm, nj),
        in_specs=[
            pl.BlockSpec((tm, d), lambda i, j: (i, 0)),
            pl.BlockSpec((1, d), lambda i, j: (0, 0)),
            _w_spec(w_in, layer, d, tn),
            _w_spec(w_in, layer, d, tn, nj),
        ],
        out_specs=pl.BlockSpec((tm, tn), lambda i, j: (i, j)),
        out_shape=jax.ShapeDtypeStruct((s, f), BF16),
        scratch_shapes=[pltpu.VMEM((tm, d), BF16)],
        compiler_params=_params(("parallel", "arbitrary")),
    )(h, gain.reshape(1, d), w_in, w_in)


def _ple_kernel(x_ref, g_ref, wg_ref, p_ref, wp_ref, h_ref, o_ref, xn_ref, pb_ref):
    @pl.when(pl.program_id(1) == 0)
    def _():
        xn_ref[...] = _rms(x_ref[...], g_ref[...]).astype(BF16)
        pb_ref[...] = p_ref[...].astype(BF16)

    gate = jax.nn.sigmoid(jnp.dot(xn_ref[...], wg_ref[...].astype(BF16), preferred_element_type=F32))
    proj = jnp.dot(pb_ref[...], wp_ref[...].astype(BF16), preferred_element_type=F32)
    o_ref[...] = h_ref[...] + gate * proj


def ple(h, gain, w_gate, p, w_proj, *, layer, tm=512, tn=2048):
    s, d = h.shape
    pd = p.shape[-1]
    tm = min(tm, s)
    return pl.pallas_call(
        _ple_kernel,
        name="ple",
        grid=(s // tm, d // tn),
        in_specs=[
            pl.BlockSpec((tm, d), lambda i, j: (i, 0)),
            pl.BlockSpec((1, d), lambda i, j: (0, 0)),
            _w_spec(w_gate, layer, d, tn),
            pl.BlockSpec((None, None, tm, pd), lambda i, j: (layer, 0, i, 0)),
            _w_spec(w_proj, layer, pd, tn),
            pl.BlockSpec((tm, tn), lambda i, j: (i, j)),
        ],
        out_specs=pl.BlockSpec((tm, tn), lambda i, j: (i, j)),
        out_shape=jax.ShapeDtypeStruct((s, d), F32),
        scratch_shapes=[pltpu.VMEM((tm, d), BF16), pltpu.VMEM((tm, pd), BF16)],
        compiler_params=_params(("parallel", "arbitrary")),
    )(h, gain.reshape(1, d), w_gate, p, w_proj, h)


def _final_norm_kernel(x_ref, g_ref, o_ref):
    o_ref[...] = _rms(x_ref[...], g_ref[...])


def final_rms_norm(h, gain, *, tm=512):
    s, d = h.shape
    tm = min(tm, s)
    return pl.pallas_call(
        _final_norm_kernel,
        name="final_norm",
        grid=(s // tm,),
        in_specs=[pl.BlockSpec((tm, d), lambda i: (i, 0)), pl.BlockSpec((1, d), lambda i: (0, 0))],
        out_specs=pl.BlockSpec((tm, d), lambda i: (i, 0)),
        out_shape=jax.ShapeDtypeStruct((s, d), F32),
        compiler_params=_params(("parallel",)),
    )(h, gain.reshape(1, d))


IDX_TQ = 256
IDX_KC = 256


def _orderable(x):
    b = pltpu.bitcast(x, jnp.int32)
    return jnp.where(b < 0, b ^ jnp.int32(0x7FFFFFFF), b)


def _indexer_kernel(qi_ref, ki_ref, wi_ref, mask_ref, keys_ref, *, topk, n_kblocks):
    tq, kc = IDX_TQ, IDX_KC
    i = pl.program_id(0)
    nkc = (i + 1) * (tq // kc)
    n_idx_bits = (n_kblocks * kc - 1).bit_length()
    w_t = (wi_ref[...] * (IDX_HEADS ** -0.5 * IDX_DIM ** -0.5)).T
    qchunk = (i * tq + lax.broadcasted_iota(jnp.int32, (kc, tq), 1)) >> CHUNK_SHIFT
    krow = lax.broadcasted_iota(jnp.int32, (kc, tq), 0)

    def valid_of(c):
        return ((c * kc + krow) >> CHUNK_SHIFT) <= qchunk

    def score_body(c, carry):
        k0 = pl.multiple_of(c * kc, kc)
        kblk = ki_ref[pl.ds(k0, kc), :].astype(BF16)
        acc = jnp.zeros((kc, tq), F32)
        for h in range(IDX_HEADS):
            s = lax.dot_general(kblk, qi_ref[:, h * IDX_DIM:(h + 1) * IDX_DIM], NT_DIMS,
                                preferred_element_type=F32)
            acc = acc + jnp.maximum(s, 0.0) * w_t[h:h + 1, :]
        keys_ref[pl.ds(k0, kc), :] = jnp.where(valid_of(c), _orderable(acc), jnp.int32(INT_MIN))
        return carry

    lax.fori_loop(0, nkc, score_body, 0)

    def count(pred_of):
        def body(c, cnt):
            k0 = pl.multiple_of(c * kc, kc)
            hit = pred_of(c, keys_ref[pl.ds(k0, kc), :]).astype(jnp.int32)
            return cnt + hit.reshape(kc // 8, 8, tq).sum(axis=0)
        cnt = lax.fori_loop(0, nkc, body, jnp.zeros((8, tq), jnp.int32))
        return cnt.sum(axis=0, keepdims=True)

    def bit_step(b, state):
        thr, cnt = state
        cand = thr + lax.shift_left(jnp.int32(1), 31 - b)
        c_cand = count(lambda c, keys: keys >= cand)
        take = c_cand >= topk
        return jnp.where(take, cand, thr), jnp.where(take, c_cand, cnt)

    thr, cnt = lax.fori_loop(
        0, 32, bit_step,
        (jnp.full((1, tq), INT_MIN, jnp.int32), jnp.full((1, tq), nkc * kc, jnp.int32)))

    def tie_limit():
        need = topk - count(lambda c, keys: keys > thr)
        def bit_body(b, x):
            cand = x + lax.shift_left(jnp.int32(1), n_idx_bits - 1 - b)
            below = count(lambda c, keys: (keys == thr) & ((c * kc + krow) < cand))
            return jnp.where(below < need, cand, x)
        return lax.fori_loop(0, n_idx_bits, bit_body, jnp.zeros((1, tq), jnp.int32))

    def write_masks(selected):
        def mask_body(c, carry):
            k0 = pl.multiple_of(c * kc, kc)
            sel = selected(c, keys_ref[pl.ds(k0, kc), :]) & valid_of(c)
            mask_ref[c] = jnp.where(sel, 0.0, -jnp.inf).astype(F32).T
            return carry
        lax.fori_loop(0, nkc, mask_body, 0)

    has_ties = jnp.max(jnp.where((cnt > topk) & (thr > INT_MIN), 1, 0)) > 0

    @pl.when(has_ties)
    def _():
        tie_last = tie_limit()
        write_masks(lambda c, keys: (keys > thr) | ((keys == thr) & ((c * kc + krow) <= tie_last)))

    @pl.when(jnp.logical_not(has_ties))
    def _():
        write_masks(lambda c, keys: keys >= thr)

    def fill_body(c, carry):
        mask_ref[c] = jnp.full((tq, kc), -jnp.inf, F32)
        return carry

    lax.fori_loop(nkc, n_kblocks, fill_body, 0)


def dsa_indexer(qi, kiwi, *, topk):
    s = qi.shape[0]
    tq, kc = IDX_TQ, IDX_KC
    nkb = s // kc
    return pl.pallas_call(
        functools.partial(_indexer_kernel, topk=topk, n_kblocks=nkb),
        name="dsa_indexer",
        grid=(s // tq,),
        in_specs=[
            pl.BlockSpec((tq, IDX_HEADS * IDX_DIM), lambda i: (i, 0)),
            pl.BlockSpec((s, IDX_DIM), lambda i: (0, 0)),
            pl.BlockSpec((tq, LANE), lambda i: (i, 1)),
        ],
        out_specs=pl.BlockSpec((nkb, tq, kc), lambda i: (0, i, 0)),
        out_shape=jax.ShapeDtypeStruct((nkb, s, kc), F32),
        scratch_shapes=[pltpu.VMEM((s, tq), jnp.int32)],
        compiler_params=_params(("parallel",)),
    )(qi, kiwi, kiwi)


ATT_A_TQ = 256
ATT_A_KB = 2
ATT_A_NEAR = (T5_FAR + ATT_A_KB * IDX_KC - 2) // IDX_KC + 1


def _t5_bucket(rel):
    nb = T5_BUCKETS // 2
    max_exact = nb // 2
    ret = jnp.where(rel > 0, nb, 0)
    n = jnp.abs(rel)
    nf = jnp.maximum(n, 1).astype(F32)
    large = max_exact + (jnp.log(nf / max_exact) / math.log(T5_MAX_DISTANCE / max_exact)
                         * (nb - max_exact)).astype(jnp.int32)
    large = jnp.minimum(large, nb - 1)
    return ret + jnp.where(n < max_exact, n, large)


def _softmax_step(s, v, m_ref, l_ref, acc_ref, rows, tk, shift=None):
    d = v.shape[1]
    m_prev = m_ref[rows, :]
    mx = jnp.max(s, axis=1, keepdims=True)
    if shift is not None:
        mx = mx + shift
    m_next = jnp.maximum(m_prev, mx)
    pivot = m_next if shift is None else m_next - shift
    p = jnp.exp2(s - jnp.concatenate([pivot] * (tk // LANE), axis=1)).astype(BF16)
    alpha = jnp.exp2(m_prev - m_next)
    v_ones = jnp.concatenate([v, jnp.ones((tk, LANE), BF16)], axis=1)
    pv = jnp.dot(p, v_ones, preferred_element_type=F32)
    m_ref[rows, :] = m_next
    l_ref[rows, :] = alpha * l_ref[rows, :] + pv[:, d:]
    acc_ref[rows, :] = acc_ref[rows, :] * alpha + pv[:, :d]


def _for_tiles_in_pairs(lo, hi, tile_fn):
    def pair(kp, carry):
        tile_fn(lo + 2 * kp)
        tile_fn(lo + 2 * kp + 1)
        return carry

    lax.fori_loop(0, (hi - lo) // 2, pair, 0)
    pl.when((hi - lo) % 2 == 1)(lambda: tile_fn(hi - 1))


def _attn_a_kernel(qmin_ref, kmax_ref, t5s_ref, q_ref, k_ref, v_ref, mask_ref, posq_ref, posk_ref,
                   t5t_ref, o_ref, m_ref, l_ref, acc_ref, nbias_ref, *, consecutive):
    tq, kc, nb = ATT_A_TQ, IDX_KC, ATT_A_KB
    tk = nb * kc
    g = pl.program_id(0)
    i = pl.program_id(1)

    def head_bias(r, bucket):
        row = t5t_ref[pl.ds(g * A_GROUP + r, 1), :] * LOG2E
        tbl = jnp.broadcast_to(row, (tq, LANE))
        return jnp.concatenate([_lane_gather(tbl, bucket[:, c * LANE:(c + 1) * LANE])
                                for c in range(tk // LANE)], axis=1)

    if consecutive:
        @pl.when(i == 0)
        def _():
            rel0 = (lax.broadcasted_iota(jnp.int32, (tq, tk), 1)
                    - lax.broadcasted_iota(jnp.int32, (tq, tk), 0))
            for d in range(ATT_A_NEAR):
                bucket = _t5_bucket(rel0 + (d - ATT_A_NEAR + 1) * kc)
                for r in range(A_GROUP):
                    nbias_ref[d, r] = head_bias(r, bucket)

    m_ref[...] = jnp.full(m_ref.shape, NEG_BIG, F32)
    l_ref[...] = jnp.zeros(l_ref.shape, F32)
    acc_ref[...] = jnp.zeros(acc_ref.shape, F32)
    nkt = ((i + 1) * (tq // kc) + nb - 1) // nb
    qmin = qmin_ref[i]

    def is_far(kt):
        ktc = jnp.minimum(kt, nkt - 1) * nb
        kmax = kmax_ref[ktc]
        for b in range(1, nb):
            kmax = jnp.maximum(kmax, kmax_ref[ktc + b])
        return (kt < nkt) & ((qmin - kmax) >= T5_FAR)

    n_far = lax.while_loop(is_far, lambda kt: kt + 1, jnp.int32(0))

    def tile(kt, bias_of, shift_of):
        k0 = pl.multiple_of(kt * tk, tk)
        k_t = k_ref[pl.ds(k0, tk), :]
        v_t = v_ref[pl.ds(k0, tk), :]
        base = jnp.concatenate([mask_ref[kt * nb + b] for b in range(nb)], axis=1)
        for r in range(A_GROUP):
            rows = slice(r * tq, (r + 1) * tq)
            q_r = q_ref[:, r * A_HEAD_DIM:(r + 1) * A_HEAD_DIM]
            s = lax.dot_general(q_r, k_t, NT_DIMS, preferred_element_type=F32) + bias_of(r, base)
            _softmax_step(s, v_t, m_ref, l_ref, acc_ref, rows, tk, shift_of(r))

    def far_tile(kt):
        tile(kt, lambda r, base: base,
             lambda r: t5s_ref[T5_BUCKETS // 2 - 1, g * A_GROUP + r] * LOG2E)

    def near_tile(kt):
        if consecutive:
            d = kt * nb - i * (tq // kc) + (ATT_A_NEAR - 1)
            tile(kt, lambda r, base: base + nbias_ref[d, r], lambda r: None)
        else:
            pk = jnp.concatenate([posk_ref[kt * nb + b] for b in range(nb)], axis=1)
            bucket = _t5_bucket(pk - posq_ref[...])
            tile(kt, lambda r, base: base + head_bias(r, bucket), lambda r: None)

    _for_tiles_in_pairs(0, n_far, far_tile)
    _for_tiles_in_pairs(n_far, nkt, near_tile)
    for r in range(A_GROUP):
        rows = slice(r * tq, (r + 1) * tq)
        o_ref[:, r * A_HEAD_DIM:(r + 1) * A_HEAD_DIM] = (acc_ref[rows, :] / l_ref[rows, :]).astype(o_ref.dtype)


def dsa_attention(qkv, mask, pos, t5_table, *, consecutive):
    s = qkv.shape[0]
    tq, tk = ATT_A_TQ, IDX_KC
    assert (s // tk) % ATT_A_KB == 0
    gw = A_GROUP * A_HEAD_DIM
    qmin = pos.reshape(s // tq, tq).min(axis=1)
    kmax = pos.reshape(s // tk, tk).max(axis=1)
    t5t = jnp.zeros((A_HEADS, LANE), F32).at[:, :T5_BUCKETS].set(t5_table.T)
    kblk0 = (A_HEADS * A_HEAD_DIM) // A_HEAD_DIM
    smem = pl.BlockSpec(memory_space=pltpu.SMEM)
    nbias_shape = (ATT_A_NEAR, A_GROUP, tq, ATT_A_KB * tk) if consecutive else (1, 1, 8, LANE)
    return pl.pallas_call(
        functools.partial(_attn_a_kernel, consecutive=consecutive),
        name="dsa_attention",
        grid=(A_KV_HEADS, s // tq),
        in_specs=[
            smem, smem, smem,
            pl.BlockSpec((tq, gw), lambda g, i: (i, g)),
            pl.BlockSpec((s, A_HEAD_DIM), lambda g, i: (0, kblk0 + g)),
            pl.BlockSpec((s, A_HEAD_DIM), lambda g, i: (0, kblk0 + A_KV_HEADS + g)),
            pl.BlockSpec((s // tk, tq, tk), lambda g, i: (0, i, 0)),
            pl.BlockSpec((tq, 1), lambda g, i: (i, 0)),
            pl.BlockSpec((s // tk, 1, tk), lambda g, i: (0, 0, 0)),
            pl.BlockSpec((A_HEADS, LANE), lambda g, i: (0, 0)),
        ],
        out_specs=pl.BlockSpec((tq, gw), lambda g, i: (i, g)),
        out_shape=jax.ShapeDtypeStruct((s, A_HEADS * A_HEAD_DIM), BF16),
        scratch_shapes=[
            pltpu.VMEM((A_GROUP * tq, LANE), F32),
            pltpu.VMEM((A_GROUP * tq, LANE), F32),
            pltpu.VMEM((A_GROUP * tq, A_HEAD_DIM), F32),
            pltpu.VMEM(nbias_shape, F32),
        ],
        compiler_params=_params(("parallel", "arbitrary")),
    )(qmin, kmax, t5_table, qkv, qkv, qkv, mask, pos.reshape(s, 1), pos.reshape(s // tk, 1, tk), t5t)


ATT_B_TQ = 256
ATT_B_NKB = 3
ATT_B_LG = 2


def _attn_b_kernel(t256_ref, q_ref, k0_ref, k1_ref, k2_ref, v0_ref, v1_ref, v2_ref, posq_ref, posk_ref,
                   relt_ref, o_ref, bias_ref, *, consecutive):
    tq = ATT_B_TQ
    nh = LANE // B_HEAD_DIM
    gp = pl.program_id(0)
    i = pl.program_id(1)
    k_refs = (k0_ref, k1_ref, k2_ref)
    v_refs = (v0_ref, v1_ref, v2_ref)

    def build_bias():
        row_i = lax.broadcasted_iota(jnp.int32, (tq, tq), 0)
        col_i = lax.broadcasted_iota(jnp.int32, (tq, tq), 1)
        for j in range(ATT_B_NKB):
            back = (ATT_B_NKB - 1 - j) * tq
            if consecutive:
                rel = row_i - (col_i - back)
            else:
                rel = posq_ref[...] - posk_ref[jnp.maximum(i - (ATT_B_NKB - 1) + j, 0)]
            r = jnp.clip(rel, -B_REL_CLIP, B_REL_CLIP) + B_REL_CLIP
            dchunk = (row_i >> CHUNK_SHIFT) - ((col_i - back) >> CHUNK_SHIFT)
            band = (dchunk >= 0) & (dchunk <= B_PREV_CHUNKS)
            for hl in range(ATT_B_LG * nh):
                h = gp * (ATT_B_LG * nh) + hl
                row = relt_ref[pl.ds(h, 1), :] * LOG2E
                seg0 = jnp.broadcast_to(row[:, :LANE], (tq, LANE))
                seg1 = jnp.broadcast_to(row[:, LANE:2 * LANE], (tq, LANE))
                t256 = t256_ref[h] * LOG2E
                for c in range(tq // LANE):
                    cs = slice(c * LANE, (c + 1) * LANE)
                    rc = r[:, cs]
                    lo = rc & (LANE - 1)
                    bias = jnp.where(rc < LANE, _lane_gather(seg0, lo),
                                     jnp.where(rc < 2 * LANE, _lane_gather(seg1, lo), t256))
                    bias_ref[hl, :, j * tq + c * LANE:j * tq + (c + 1) * LANE] = jnp.where(band[:, cs], bias, -jnp.inf)

    if consecutive:
        pl.when(i == 0)(build_bias)
    else:
        build_bias()

    lane_head = lax.broadcasted_iota(jnp.int32, (1, LANE), 1) >> CHUNK_SHIFT
    for lg in range(ATT_B_LG):
        lanes = slice(lg * LANE, (lg + 1) * LANE)
        qp = q_ref[:, lanes]
        out = jnp.zeros((tq, LANE), F32)
        for hh in range(nh):
            mine = lane_head == hh
            qm = jnp.where(mine, qp, jnp.zeros_like(qp))
            parts = []
            for j in range(ATT_B_NKB):
                sj = lax.dot_general(qm, k_refs[j][:, lanes], NT_DIMS, preferred_element_type=F32)
                sj = sj + bias_ref[lg * nh + hh, :, j * tq:(j + 1) * tq]
                if j < ATT_B_NKB - 1:
                    sj = sj + jnp.where(i - (ATT_B_NKB - 1) + j >= 0, 0.0, -jnp.inf)
                parts.append(sj)
            s = jnp.concatenate(parts, axis=1)
            p = jnp.exp2(s - jnp.max(s, axis=1, keepdims=True)).astype(BF16)
            acc = jnp.zeros((tq, LANE), F32)
            for j in range(ATT_B_NKB):
                vj = v_refs[j][:, lanes]
                vm = jnp.where(mine, vj, jnp.ones_like(vj))
                acc = acc + jnp.dot(p[:, j * tq:(j + 1) * tq], vm, preferred_element_type=F32)
            rowsum = pltpu.roll(acc, B_HEAD_DIM, 1)
            out = out + jnp.where(mine, acc / rowsum, 0.0)
        o_ref[:, lanes] = out.astype(o_ref.dtype)


def band_attention(qkv, pos, rel_table, *, consecutive):
    s = qkv.shape[0]
    tq = ATT_B_TQ
    hw = B_HEADS * B_HEAD_DIM
    bw = ATT_B_LG * LANE
    ngrp = hw // bw
    assert LANE // B_HEAD_DIM == 2
    nrel = 2 * B_REL_CLIP + 1
    relt = jnp.zeros((B_HEADS, 3 * LANE), F32).at[:, :nrel].set(rel_table.T)
    t256 = rel_table[nrel - 1]

    def kv_spec(j, base):
        return pl.BlockSpec((tq, bw), lambda gp, i: (jnp.maximum(i - (ATT_B_NKB - 1) + j, 0), base + gp))

    return pl.pallas_call(
        functools.partial(_attn_b_kernel, consecutive=consecutive),
        name="band_attention",
        grid=(ngrp, s // tq),
        in_specs=[
            pl.BlockSpec(memory_space=pltpu.SMEM),
            pl.BlockSpec((tq, bw), lambda gp, i: (i, gp)),
            kv_spec(0, ngrp), kv_spec(1, ngrp), kv_spec(2, ngrp),
            kv_spec(0, 2 * ngrp), kv_spec(1, 2 * ngrp), kv_spec(2, 2 * ngrp),
            pl.BlockSpec((tq, 1), lambda gp, i: (i, 0)),
            pl.BlockSpec((s // tq, 1, tq), lambda gp, i: (0, 0, 0)),
            pl.BlockSpec((B_HEADS, 3 * LANE), lambda gp, i: (0, 0)),
        ],
        out_specs=pl.BlockSpec((tq, bw), lambda gp, i: (i, gp)),
        out_shape=jax.ShapeDtypeStruct((s, hw), BF16),
        scratch_shapes=[pltpu.VMEM((ATT_B_LG * (LANE // B_HEAD_DIM), tq, ATT_B_NKB * tq), F32)],
        compiler_params=_params(("parallel", "arbitrary")),
    )(t256, qkv, qkv, qkv, qkv, qkv, qkv, qkv, pos.reshape(s, 1), pos.reshape(s // tq, 1, tq), relt)


ATT_C_T = 512
ATT_C_HEADS = 2


def _attn_c_kernel(q_ref, kn_ref, kr_ref, v_ref, o_ref, m_ref, l_ref, acc_ref):
    t, nh = ATT_C_T, ATT_C_HEADS
    iq = pl.program_id(1)
    m_ref[...] = jnp.full(m_ref.shape, NEG_BIG, F32)
    l_ref[...] = jnp.zeros(l_ref.shape, F32)
    acc_ref[...] = jnp.zeros(acc_ref.shape, F32)

    def tile(kt, diag):
        k0 = pl.multiple_of(kt * t, t)
        kr_t = kr_ref[pl.ds(k0, t), :]
        for hh in range(nh):
            kcat = jnp.concatenate([kn_ref[pl.ds(k0, t), hh * C_NOPE:(hh + 1) * C_NOPE], kr_t], axis=1)
            s = lax.dot_general(q_ref[:, hh * 2 * LANE:(hh + 1) * 2 * LANE], kcat, NT_DIMS,
                                preferred_element_type=F32)
            if diag:
                qc = lax.broadcasted_iota(jnp.int32, (t, t), 0) >> CHUNK_SHIFT
                kc = lax.broadcasted_iota(jnp.int32, (t, t), 1) >> CHUNK_SHIFT
                s = jnp.where(kc <= qc, s, -jnp.inf)
            _softmax_step(s, v_ref[pl.ds(k0, t), hh * C_V:(hh + 1) * C_V], m_ref, l_ref, acc_ref,
                          slice(hh * t, (hh + 1) * t), t)

    _for_tiles_in_pairs(0, iq, lambda kt: tile(kt, False))
    tile(iq, True)
    for hh in range(nh):
        rows = slice(hh * t, (hh + 1) * t)
        o_ref[:, hh * C_V:(hh + 1) * C_V] = (acc_ref[rows, :] / l_ref[rows, :]).astype(o_ref.dtype)


def mla_attention(qcat, kv, kr):
    s = qcat.shape[0]
    t, nh = ATT_C_T, ATT_C_HEADS
    assert s % t == 0
    return pl.pallas_call(
        _attn_c_kernel,
        name="mla_attention",
        grid=(C_HEADS // nh, s // t),
        in_specs=[
            pl.BlockSpec((t, nh * 2 * LANE), lambda h, i: (i, h)),
            pl.BlockSpec((s, nh * C_NOPE), lambda h, i: (0, h)),
            pl.BlockSpec((s, LANE), lambda h, i: (0, 0)),
            pl.BlockSpec((s, nh * C_V), lambda h, i: (0, C_HEADS // nh + h)),
        ],
        out_specs=pl.BlockSpec((t, nh * C_V), lambda h, i: (i, h)),
        out_shape=jax.ShapeDtypeStruct((s, C_HEADS * C_V), BF16),
        scratch_shapes=[pltpu.VMEM((nh * t, LANE), F32), pltpu.VMEM((nh * t, LANE), F32),
                        pltpu.VMEM((nh * t, C_V), F32)],
        compiler_params=_params(("parallel", "arbitrary")),
    )(qcat, kv, kr, kv)


def _rope_tables(pos):
    half = ROPE_DIM // 2
    inv = ROPE_BASE ** (-jnp.arange(half, dtype=F32) * 2.0 / ROPE_DIM)
    ang = pos.astype(F32)[:, None] * inv
    cos, sin = jnp.cos(ang), jnp.sin(ang)
    s = pos.shape[0]
    z = jnp.zeros((s, half), F32)
    c = jnp.concatenate([cos, cos, jnp.ones((s, LANE - ROPE_DIM), F32)], axis=1)
    s1 = jnp.concatenate([-sin, z, z, z], axis=1)
    s2 = jnp.concatenate([z, sin, z, z], axis=1)
    return c, s1, s2


def _query_scale(n_query, n_total, scale):
    return jnp.concatenate([jnp.full((n_query,), scale * LOG2E, F32), jnp.ones((n_total - n_query,), F32)])


def _by_position_layout(consecutive, fn, *args):
    return lax.cond(consecutive, functools.partial(fn, consecutive=True),
                    functools.partial(fn, consecutive=False), *args)


def _mixer_a(h, gain, pos, consecutive, tables, w_in, w_out, layer, t5_table):
    s = h.shape[0]
    d = h.shape[1]
    nq = A_HEADS * A_HEAD_DIM
    nkv = A_KV_HEADS * A_HEAD_DIM
    nqi = IDX_HEADS * IDX_DIM
    o_qkv = nq + 2 * nkv
    o_ki = o_qkv + nqi
    tn = 1024
    assert o_qkv % tn == 0 and nqi % tn == 0
    qkv = norm_mm(h, 0, d, gain, w_in, layer=layer, n=o_qkv, tn=tn, out_dtype=BF16, name="a_qkv_proj",
                  col_scale=_query_scale(nq, o_qkv, A_HEAD_DIM ** -0.5))
    qi = norm_mm(h, 0, d, gain, w_in, layer=layer, n=nqi, col0=o_qkv // tn, tn=tn, out_dtype=BF16,
                 rope=(True,) * (tn // LANE), tables=tables, name="a_idxq_proj")
    w_kw = jnp.zeros((d, 2 * LANE), BF16).at[:, :IDX_DIM + IDX_HEADS].set(w_in[layer, :, o_ki:].astype(BF16))
    kiwi = norm_mm(h, 0, d, gain, w_kw, tn=2 * LANE, out_dtype=F32, rope=(True, False), tables=tables,
                   name="a_idxk_proj")
    mask = dsa_indexer(qi, kiwi, topk=min(IDX_TOPK_MAX, s // 4))
    o = _by_position_layout(consecutive, dsa_attention, qkv, mask, pos, t5_table)
    return mm_residual(o, w_out, h, layer=layer, tn=w_out.shape[-1])


def _mixer_b(h, gain, pos, consecutive, w_in, rel_table, w_out, layer):
    d = h.shape[1]
    n = w_in.shape[-1]
    qkv = norm_mm(h, 0, d, gain, w_in, layer=layer, tn=1024, out_dtype=BF16, name="b_qkv_proj",
                  col_scale=_query_scale(n // 3, n, B_HEAD_DIM ** -0.5))
    o = _by_position_layout(consecutive, band_attention, qkv, pos, rel_table)
    return mm_residual(o, w_out, h, layer=layer, tn=w_out.shape[-1])


def _mixer_c(h, gain, tables, w_down, g_q, g_kv, w_uq, w_ukv, w_out, layer):
    d = h.shape[1]
    lq = g_q.shape[0]
    lkv = g_kv.shape[0]
    n_down = lq + lkv + LANE
    wd = jnp.zeros((d, n_down), BF16).at[:, :w_down.shape[1]].set(w_down.astype(BF16))
    down = norm_mm(h, 0, d, gain, wd, tn=n_down, out_dtype=F32,
                   rope=(False,) * ((lq + lkv) // LANE) + (True,), tables=tables, name="c_down_proj")
    wq = w_uq.astype(BF16).reshape(lq, C_HEADS, C_NOPE + C_ROPE)
    wq = jnp.pad(wq, ((0, 0), (0, 0), (0, 2 * LANE - C_NOPE - C_ROPE))).reshape(lq, C_HEADS * 2 * LANE)
    nqc = wq.shape[1]
    qcat = norm_mm(down, 0, lq, g_q, wq, tn=1024, out_dtype=BF16, rope=(False, True) * 4, tables=tables,
                   name="c_uq_proj", col_scale=_query_scale(nqc, nqc, (C_NOPE + C_ROPE) ** -0.5))
    wkv = w_ukv.astype(BF16).reshape(lkv, C_HEADS, 2, C_NOPE).transpose(0, 2, 1, 3).reshape(lkv, -1)
    assert lq == lkv
    kv = norm_mm(down, 1, lkv, g_kv, wkv, tn=1024, out_dtype=BF16, name="c_ukv_proj")
    kr = down[:, lq + lkv:].astype(BF16)
    o = mla_attention(qcat, kv, kr)
    return mm_residual(o, w_out, h, layer=layer, tn=w_out.shape[-1])


def kernel(x, p, positions, t5_table, a_w_in, a_w_out, b_w_in, b_rel_table, b_w_out, c_w_down, c_q_norm,
           c_kv_norm, c_w_uq, c_w_ukv, c_w_out, attn_norm, ffn_norm, ffn_w_in, ffn_w_out, ple_norm,
           ple_w_gate, ple_w_proj, final_norm):
    assert x.shape[0] == 1
    depth = attn_norm.shape[0]
    h = x[0]
    pos = positions[0]
    tables = _rope_tables(pos)
    consecutive = jnp.all(pos[1:] - pos[:-1] == 1)
    a_w_in, a_w_out, b_w_in, b_w_out, c_w_out, ffn_w_in, ffn_w_out, ple_w_gate, ple_w_proj = (
        w.astype(BF16) for w in (a_w_in, a_w_out, b_w_in, b_w_out, c_w_out, ffn_w_in, ffn_w_out, ple_w_gate,
                                 ple_w_proj))
    for i in range(depth):
        j, kind = divmod(i, 3)
        if kind == 0:
            h = _mixer_a(h, attn_norm[i], pos, consecutive, tables, a_w_in, a_w_out, j, t5_table)
        elif kind == 1:
            h = _mixer_b(h, attn_norm[i], pos, consecutive, b_w_in, b_rel_table[j], b_w_out, j)
        else:
            h = _mixer_c(h, attn_norm[i], tables, c_w_down[j], c_q_norm[j], c_kv_norm[j], c_w_uq[j],
                         c_w_ukv[j], c_w_out, j)
        act = ffn_in(h, ffn_norm[i], ffn_w_in, layer=i)
        h = mm_residual(act, ffn_w_out, h, layer=i, tn=1024)
        h = ple(h, ple_norm[i], ple_w_gate, p, ple_w_proj, layer=i)
    return final_rms_norm(h, final_norm)[None]
```

```python
import functools
import math

import jax
import jax.numpy as jnp
from jax import lax
from jax.experimental import pallas as pl
from jax.experimental.pallas import tpu as pltpu

LANE = 128
VMEM_LIMIT_BYTES = 56 * 1024 * 1024

CHUNK = 64
CHUNK_SHIFT = 6
EPS = 1e-6
ROPE_BASE = 10000.0
ROPE_DIM = 64
A_HEADS = 16
A_KV_HEADS = 4
A_GROUP = 4
A_HEAD_DIM = 128
IDX_HEADS = 16
IDX_DIM = 128
IDX_TOPK_MAX = 256
T5_BUCKETS = 32
T5_MAX_DISTANCE = 1024
T5_FAR = 640
B_HEADS = 32
B_HEAD_DIM = 64
B_PREV_CHUNKS = 8
B_REL_CLIP = 128
C_HEADS = 16
C_NOPE = 128
C_ROPE = 64
C_V = 128
NEG_BIG = -1e30
INT_MIN = -(2 ** 31)
LOG2E = math.log2(math.e)

F32 = jnp.float32
BF16 = jnp.bfloat16
NT_DIMS = (((1,), (1,)), ((), ()))


def _params(sem):
    return pltpu.CompilerParams(dimension_semantics=sem, vmem_limit_bytes=VMEM_LIMIT_BYTES)


def _rms(x, g):
    ms = jnp.mean(x * x, axis=-1, keepdims=True)
    return (x * lax.rsqrt(ms + EPS)) * g


def _lane_gather(table, idx):
    return jnp.take_along_axis(table, idx, axis=1, mode="promise_in_bounds")


def _rope_group(y, c, s1, s2):
    return y * c + pltpu.roll(y, 96, 1) * s1 + pltpu.roll(y, 32, 1) * s2


def _norm_mm_kernel(*refs, rope, scaled):
    x_ref, g_ref, w_ref = refs[:3]
    rest = list(refs[3:])
    cs_ref = rest.pop(0) if scaled else None
    c_ref, s1_ref, s2_ref = (rest.pop(0), rest.pop(0), rest.pop(0)) if rope is not None else (None,) * 3
    o_ref, xn_ref = rest

    @pl.when(pl.program_id(1) == 0)
    def _():
        xn_ref[...] = _rms(x_ref[...], g_ref[...]).astype(BF16)

    y = jnp.dot(xn_ref[...], w_ref[...].astype(BF16), preferred_element_type=F32)
    if scaled:
        y = y * cs_ref[...]
    if rope is None:
        o_ref[...] = y.astype(o_ref.dtype)
    else:
        c, s1, s2 = c_ref[...], s1_ref[...], s2_ref[...]
        for gi, on in enumerate(rope):
            sl = slice(gi * LANE, (gi + 1) * LANE)
            yg = y[:, sl]
            if on:
                yg = _rope_group(yg, c, s1, s2)
            o_ref[:, sl] = yg.astype(o_ref.dtype)


def _w_spec(w, layer, k, tn, col0=0):
    if w.ndim == 2:
        return pl.BlockSpec((k, tn), lambda i, j: (0, col0 + j))
    return pl.BlockSpec((None, k, tn), lambda i, j: (layer, 0, col0 + j))


def norm_mm(x, x_col, kx, gain, w, *, tn, out_dtype, name, n=None, layer=None, col0=0, tm=512, rope=None,
            tables=None, col_scale=None):
    s = x.shape[0]
    n = w.shape[-1] if n is None else n
    tm = min(tm, s)
    in_specs = [
        pl.BlockSpec((tm, kx), lambda i, j: (i, x_col)),
        pl.BlockSpec((1, kx), lambda i, j: (0, 0)),
        _w_spec(w, layer, kx, tn, col0),
    ]
    args = [x, gain.reshape(1, kx), w]
    if col_scale is not None:
        in_specs.append(pl.BlockSpec((1, tn), lambda i, j: (0, j)))
        args.append(col_scale.reshape(1, n))
    if rope is not None:
        assert len(rope) == tn // LANE
        in_specs += [pl.BlockSpec((tm, LANE), lambda i, j: (i, 0))] * 3
        args += list(tables)
    return pl.pallas_call(
        functools.partial(_norm_mm_kernel, rope=rope, scaled=col_scale is not None),
        grid=(s // tm, n // tn),
        in_specs=in_specs,
        out_specs=pl.BlockSpec((tm, tn), lambda i, j: (i, j)),
        out_shape=jax.ShapeDtypeStruct((s, n), out_dtype),
        scratch_shapes=[pltpu.VMEM((tm, kx), BF16)],
        compiler_params=_params(("parallel", "arbitrary")),
        name=name,
    )(*args)


def _mm_res_kernel(x_ref, w_ref, r_ref, o_ref):
    o_ref[...] = r_ref[...] + jnp.dot(x_ref[...], w_ref[...].astype(BF16), preferred_element_type=F32)


def mm_residual(x, w, res, *, layer=None, tm=512, tn=512):
    s, k = x.shape
    n = w.shape[-1]
    tm = min(tm, s)
    return pl.pallas_call(
        _mm_res_kernel,
        name="mm_residual",
        grid=(s // tm, n // tn),
        in_specs=[
            pl.BlockSpec((tm, k), lambda i, j: (i, 0)),
            _w_spec(w, layer, k, tn),
            pl.BlockSpec((tm, tn), lambda i, j: (i, j)),
        ],
        out_specs=pl.BlockSpec((tm, tn), lambda i, j: (i, j)),
        out_shape=jax.ShapeDtypeStruct((s, n), F32),
        compiler_params=_params(("parallel", "arbitrary")),
    )(x, w, res)


def _ffn_in_kernel(x_ref, g_ref, wg_ref, wu_ref, o_ref, xn_ref):
    @pl.when(pl.program_id(1) == 0)
    def _():
        xn_ref[...] = _rms(x_ref[...], g_ref[...]).astype(BF16)

    xn = xn_ref[...]
    a = jnp.dot(xn, wg_ref[...].astype(BF16), preferred_element_type=F32)
    u = jnp.dot(xn, wu_ref[...].astype(BF16), preferred_element_type=F32)
    o_ref[...] = (a * jax.nn.sigmoid(a) * u).astype(o_ref.dtype)


def ffn_in(h, gain, w_in, *, layer=None, tm=1024, tn=512):
    s, d = h.shape
    f = w_in.shape[-1] // 2
    tm = min(tm, s)
    nj = f // tn
    return pl.pallas_call(
        _ffn_in_kernel,
        name="ffn_in",
        grid=(s // tm, nj),
        in_specs=[
            pl.BlockSpec((tm, d), lambda i, j: (i, 0)),
            pl.BlockSpec((1, d), lambda i, j: (0, 0)),
            _w_spec(w_in, layer, d, tn),
            _w_spec(w_in, layer, d, tn, nj),
        ],
        out_specs=pl.BlockSpec((tm, tn), lambda i, j: (i, j)),
        out_shape=jax.ShapeDtypeStruct((s, f), BF16),
        scratch_shapes=[pltpu.VMEM((tm, d), BF16)],
        compiler_params=_params(("parallel", "arbitrary")),
    )(h, gain.reshape(1, d), w_in, w_in)


def _ple_kernel(x_ref, g_ref, wg_ref, p_ref, wp_ref, h_ref, o_ref, xn_ref, pb_ref):
    @pl.when(pl.program_id(1) == 0)
    def _():
        xn_ref[...] = _rms(x_ref[...], g_ref[...]).astype(BF16)
        pb_ref[...] = p_ref[...].astype(BF16)

    gate = jax.nn.sigmoid(jnp.dot(xn_ref[...], wg_ref[...].astype(BF16), preferred_element_type=F32))
    proj = jnp.dot(pb_ref[...], wp_ref[...].astype(BF16), preferred_element_type=F32)
    o_ref[...] = h_ref[...] + gate * proj


def ple(h, gain, w_gate, p, w_proj, *, layer, tm=512, tn=2048):
    s, d = h.shape
    pd = p.shape[-1]
    tm = min(tm, s)
    return pl.pallas_call(
        _ple_kernel,
        name="ple",
        grid=(s // tm, d // tn),
        in_specs=[
            pl.BlockSpec((tm, d), lambda i, j: (i, 0)),
            pl.BlockSpec((1, d), lambda i, j: (0, 0)),
            _w_spec(w_gate, layer, d, tn),
            pl.BlockSpec((None, None, tm, pd), lambda i, j: (layer, 0, i, 0)),
            _w_spec(w_proj, layer, pd, tn),
            pl.BlockSpec((tm, tn), lambda i, j: (i, j)),
        ],
        out_specs=pl.BlockSpec((tm, tn), lambda i, j: (i, j)),
        out_shape=jax.ShapeDtypeStruct((s, d), F32),
        scratch_shapes=[pltpu.VMEM((tm, d), BF16), pltpu.VMEM((tm, pd), BF16)],
        compiler_params=_params(("parallel", "arbitrary")),
    )(h, gain.reshape(1, d), w_gate, p, w_proj, h)


def _final_norm_kernel(x_ref, g_ref, o_ref):
    o_ref[...] = _rms(x_ref[...], g_ref[...])


def final_rms_norm(h, gain, *, tm=512):
    s, d = h.shape
    tm = min(tm, s)
    return pl.pallas_call(
        _final_norm_kernel,
        name="final_norm",
        grid=(s // tm,),
        in_specs=[pl.BlockSpec((tm, d), lambda i: (i, 0)), pl.BlockSpec((1, d), lambda i: (0, 0))],
        out_specs=pl.BlockSpec((tm, d), lambda i: (i, 0)),
        out_shape=jax.ShapeDtypeStruct((s, d), F32),
        compiler_params=_params(("parallel",)),
    )(h, gain.reshape(1, d))


IDX_TQ = 256
IDX_KC = 256


def _orderable(x):
    b = pltpu.bitcast(x, jnp.int32)
    return jnp.where(b < 0, b ^ jnp.int32(0x7FFFFFFF), b)


def _indexer_kernel(qi_ref, ki_ref, wi_ref, mask_ref, keys_ref, *, topk, n_kblocks):
    tq, kc = IDX_TQ, IDX_KC
    i = pl.program_id(0)
    nkc = (i + 1) * (tq // kc)
    n_idx_bits = (n_kblocks * kc - 1).bit_length()
    w_t = (wi_ref[...] * (IDX_HEADS ** -0.5 * IDX_DIM ** -0.5)).T
    qchunk = (i * tq + lax.broadcasted_iota(jnp.int32, (kc, tq), 1)) >> CHUNK_SHIFT
    krow = lax.broadcasted_iota(jnp.int32, (kc, tq), 0)

    def valid_of(c):
        return ((c * kc + krow) >> CHUNK_SHIFT) <= qchunk

    def score_body(c, carry):
        k0 = pl.multiple_of(c * kc, kc)
        kblk = ki_ref[pl.ds(k0, kc), :].astype(BF16)
        acc = jnp.zeros((kc, tq), F32)
        for h in range(IDX_HEADS):
            s = lax.dot_general(kblk, qi_ref[:, h * IDX_DIM:(h + 1) * IDX_DIM], NT_DIMS,
                                preferred_element_type=F32)
            acc = acc + jnp.maximum(s, 0.0) * w_t[h:h + 1, :]
        keys_ref[pl.ds(k0, kc), :] = jnp.where(valid_of(c), _orderable(acc), jnp.int32(INT_MIN))
        return carry

    lax.fori_loop(0, nkc, score_body, 0)

    def count(pred_of):
        def body(c, cnt):
            k0 = pl.multiple_of(c * kc, kc)
            hit = pred_of(c, keys_ref[pl.ds(k0, kc), :]).astype(jnp.int32)
            return cnt + hit.reshape(kc // 8, 8, tq).sum(axis=0)
        cnt = lax.fori_loop(0, nkc, body, jnp.zeros((8, tq), jnp.int32))
        return cnt.sum(axis=0, keepdims=True)

    def bit_step(b, state):
        thr, cnt = state
        cand = thr + lax.shift_left(jnp.int32(1), 31 - b)
        c_cand = count(lambda c, keys: keys >= cand)
        take = c_cand >= topk
        return jnp.where(take, cand, thr), jnp.where(take, c_cand, cnt)

    thr, cnt = lax.fori_loop(
        0, 32, bit_step,
        (jnp.full((1, tq), INT_MIN, jnp.int32), jnp.full((1, tq), nkc * kc, jnp.int32)))

    def tie_limit():
        need = topk - count(lambda c, keys: keys > thr)
        def bit_body(b, x):
            cand = x + lax.shift_left(jnp.int32(1), n_idx_bits - 1 - b)
            below = count(lambda c, keys: (keys == thr) & ((c * kc + krow) < cand))
            return jnp.where(below < need, cand, x)
        return lax.fori_loop(0, n_idx_bits, bit_body, jnp.zeros((1, tq), jnp.int32))

    def write_masks(selected):
        def mask_body(c, carry):
            k0 = pl.multiple_of(c * kc, kc)
            sel = selected(c, keys_ref[pl.ds(k0, kc), :]) & valid_of(c)
            mask_ref[c] = jnp.where(sel, 0.0, -jnp.inf).astype(F32).T
            return carry
        lax.fori_loop(0, nkc, mask_body, 0)

    has_ties = jnp.max(jnp.where((cnt > topk) & (thr > INT_MIN), 1, 0)) > 0

    @pl.when(has_ties)
    def _():
        tie_last = tie_limit()
        write_masks(lambda c, keys: (keys > thr) | ((keys == thr) & ((c * kc + krow) <= tie_last)))

    @pl.when(jnp.logical_not(has_ties))
    def _():
        write_masks(lambda c, keys: keys >= thr)

    def fill_body(c, carry):
        mask_ref[c] = jnp.full((tq, kc), -jnp.inf, F32)
        return carry

    lax.fori_loop(nkc, n_kblocks, fill_body, 0)


def dsa_indexer(qi, kiwi, *, topk):
    s = qi.shape[0]
    tq, kc = IDX_TQ, IDX_KC
    nkb = s // kc
    return pl.pallas_call(
        functools.partial(_indexer_kernel, topk=topk, n_kblocks=nkb),
        name="dsa_indexer",
        grid=(s // tq,),
        in_specs=[
            pl.BlockSpec((tq, IDX_HEADS * IDX_DIM), lambda i: (i, 0)),
            pl.BlockSpec((s, IDX_DIM), lambda i: (0, 0)),
            pl.BlockSpec((tq, LANE), lambda i: (i, 1)),
        ],
        out_specs=pl.BlockSpec((nkb, tq, kc), lambda i: (0, i, 0)),
        out_shape=jax.ShapeDtypeStruct((nkb, s, kc), F32),
        scratch_shapes=[pltpu.VMEM((s, tq), jnp.int32)],
        compiler_params=_params(("parallel",)),
    )(qi, kiwi, kiwi)


ATT_A_TQ = 256
ATT_A_KB = 2
ATT_A_NEAR = (T5_FAR + ATT_A_KB * IDX_KC - 2) // IDX_KC + 1


def _t5_bucket(rel):
    nb = T5_BUCKETS // 2
    max_exact = nb // 2
    ret = jnp.where(rel > 0, nb, 0)
    n = jnp.abs(rel)
    nf = jnp.maximum(n, 1).astype(F32)
    large = max_exact + (jnp.log(nf / max_exact) / math.log(T5_MAX_DISTANCE / max_exact)
                         * (nb - max_exact)).astype(jnp.int32)
    large = jnp.minimum(large, nb - 1)
    return ret + jnp.where(n < max_exact, n, large)


def _softmax_step(s, v, m_ref, l_ref, acc_ref, rows, tk, shift=None):
    d = v.shape[1]
    m_prev = m_ref[rows, :]
    mx = jnp.max(s, axis=1, keepdims=True)
    if shift is not None:
        mx = mx + shift
    m_next = jnp.maximum(m_prev, mx)
    pivot = m_next if shift is None else m_next - shift
    p = jnp.exp2(s - jnp.concatenate([pivot] * (tk // LANE), axis=1)).astype(BF16)
    alpha = jnp.exp2(m_prev - m_next)
    v_ones = jnp.concatenate([v, jnp.ones((tk, LANE), BF16)], axis=1)
    pv = jnp.dot(p, v_ones, preferred_element_type=F32)
    m_ref[rows, :] = m_next
    l_ref[rows, :] = alpha * l_ref[rows, :] + pv[:, d:]
    acc_ref[rows, :] = acc_ref[rows, :] * alpha + pv[:, :d]


def _for_tiles_in_pairs(lo, hi, tile_fn):
    def pair(kp, carry):
        tile_fn(lo + 2 * kp)
        tile_fn(lo + 2 * kp + 1)
        return carry

    lax.fori_loop(0, (hi - lo) // 2, pair, 0)
    pl.when((hi - lo) % 2 == 1)(lambda: tile_fn(hi - 1))


def _attn_a_kernel(qmin_ref, kmax_ref, t5s_ref, q_ref, k_ref, v_ref, mask_ref, posq_ref, posk_ref,
                   t5t_ref, o_ref, m_ref, l_ref, acc_ref, nbias_ref, *, consecutive):
    tq, kc, nb = ATT_A_TQ, IDX_KC, ATT_A_KB
    tk = nb * kc
    g = pl.program_id(0)
    i = pl.program_id(1)

    def head_bias(r, bucket):
        row = t5t_ref[pl.ds(g * A_GROUP + r, 1), :] * LOG2E
        tbl = jnp.broadcast_to(row, (tq, LANE))
        return jnp.concatenate([_lane_gather(tbl, bucket[:, c * LANE:(c + 1) * LANE])
                                for c in range(tk // LANE)], axis=1)

    if consecutive:
        @pl.when(i == 0)
        def _():
            rel0 = (lax.broadcasted_iota(jnp.int32, (tq, tk), 1)
                    - lax.broadcasted_iota(jnp.int32, (tq, tk), 0))
            for d in range(ATT_A_NEAR):
                bucket = _t5_bucket(rel0 + (d - ATT_A_NEAR + 1) * kc)
                for r in range(A_GROUP):
                    nbias_ref[d, r] = head_bias(r, bucket)

    m_ref[...] = jnp.full(m_ref.shape, NEG_BIG, F32)
    l_ref[...] = jnp.zeros(l_ref.shape, F32)
    acc_ref[...] = jnp.zeros(acc_ref.shape, F32)
    nkt = ((i + 1) * (tq // kc) + nb - 1) // nb
    qmin = qmin_ref[i]

    def is_far(kt):
        ktc = jnp.minimum(kt, nkt - 1) * nb
        kmax = kmax_ref[ktc]
        for b in range(1, nb):
            kmax = jnp.maximum(kmax, kmax_ref[ktc + b])
        return (kt < nkt) & ((qmin - kmax) >= T5_FAR)

    n_far = lax.while_loop(is_far, lambda kt: kt + 1, jnp.int32(0))

    def tile(kt, bias_of, shift_of):
        k0 = pl.multiple_of(kt * tk, tk)
        k_t = k_ref[pl.ds(k0, tk), :]
        v_t = v_ref[pl.ds(k0, tk), :]
        base = jnp.concatenate([mask_ref[kt * nb + b] for b in range(nb)], axis=1)
        for r in range(A_GROUP):
            rows = slice(r * tq, (r + 1) * tq)
            q_r = q_ref[:, r * A_HEAD_DIM:(r + 1) * A_HEAD_DIM]
            s = lax.dot_general(q_r, k_t, NT_DIMS, preferred_element_type=F32) + bias_of(r, base)
            _softmax_step(s, v_t, m_ref, l_ref, acc_ref, rows, tk, shift_of(r))

    def far_tile(kt):
        tile(kt, lambda r, base: base,
             lambda r: t5s_ref[T5_BUCKETS // 2 - 1, g * A_GROUP + r] * LOG2E)

    def near_tile(kt):
        if consecutive:
            d = kt * nb - i * (tq // kc) + (ATT_A_NEAR - 1)
            tile(kt, lambda r, base: base + nbias_ref[d, r], lambda r: None)
        else:
            pk = jnp.concatenate([posk_ref[kt * nb + b] for b in range(nb)], axis=1)
            bucket = _t5_bucket(pk - posq_ref[...])
            tile(kt, lambda r, base: base + head_bias(r, bucket), lambda r: None)

    _for_tiles_in_pairs(0, n_far, far_tile)
    _for_tiles_in_pairs(n_far, nkt, near_tile)
    for r in range(A_GROUP):
        rows = slice(r * tq, (r + 1) * tq)
        o_ref[:, r * A_HEAD_DIM:(r + 1) * A_HEAD_DIM] = (acc_ref[rows, :] / l_ref[rows, :]).astype(o_ref.dtype)


def dsa_attention(qkv, mask, pos, t5_table, *, consecutive):
    s = qkv.shape[0]
    tq, tk = ATT_A_TQ, IDX_KC
    assert (s // tk) % ATT_A_KB == 0
    gw = A_GROUP * A_HEAD_DIM
    qmin = pos.reshape(s // tq, tq).min(axis=1)
    kmax = pos.reshape(s // tk, tk).max(axis=1)
    t5t = jnp.zeros((A_HEADS, LANE), F32).at[:, :T5_BUCKETS].set(t5_table.T)
    kblk0 = (A_HEADS * A_HEAD_DIM) // A_HEAD_DIM
    smem = pl.BlockSpec(memory_space=pltpu.SMEM)
    nbias_shape = (ATT_A_NEAR, A_GROUP, tq, ATT_A_KB * tk) if consecutive else (1, 1, 8, LANE)
    return pl.pallas_call(
        functools.partial(_attn_a_kernel, consecutive=consecutive),
        name="dsa_attention",
        grid=(A_KV_HEADS, s // tq),
        in_specs=[
            smem, smem, smem,
            pl.BlockSpec((tq, gw), lambda g, i: (i, g)),
            pl.BlockSpec((s, A_HEAD_DIM), lambda g, i: (0, kblk0 + g)),
            pl.BlockSpec((s, A_HEAD_DIM), lambda g, i: (0, kblk0 + A_KV_HEADS + g)),
            pl.BlockSpec((s // tk, tq, tk), lambda g, i: (0, i, 0)),
            pl.BlockSpec((tq, 1), lambda g, i: (i, 0)),
            pl.BlockSpec((s // tk, 1, tk), lambda g, i: (0, 0, 0)),
            pl.BlockSpec((A_HEADS, LANE), lambda g, i: (0, 0)),
        ],
        out_specs=pl.BlockSpec((tq, gw), lambda g, i: (i, g)),
        out_shape=jax.ShapeDtypeStruct((s, A_HEADS * A_HEAD_DIM), BF16),
        scratch_shapes=[
            pltpu.VMEM((A_GROUP * tq, LANE), F32),
            pltpu.VMEM((A_GROUP * tq, LANE), F32),
            pltpu.VMEM((A_GROUP * tq, A_HEAD_DIM), F32),
            pltpu.VMEM(nbias_shape, F32),
        ],
        compiler_params=_params(("parallel", "arbitrary")),
    )(qmin, kmax, t5_table, qkv, qkv, qkv, mask, pos.reshape(s, 1), pos.reshape(s // tk, 1, tk), t5t)


ATT_B_TQ = 256
ATT_B_NKB = 3
ATT_B_LG = 2


def _attn_b_kernel(t256_ref, q_ref, k0_ref, k1_ref, k2_ref, v0_ref, v1_ref, v2_ref, posq_ref, posk_ref,
                   relt_ref, o_ref, bias_ref, *, consecutive):
    tq = ATT_B_TQ
    nh = LANE // B_HEAD_DIM
    gp = pl.program_id(0)
    i = pl.program_id(1)
    k_refs = (k0_ref, k1_ref, k2_ref)
    v_refs = (v0_ref, v1_ref, v2_ref)

    def build_bias():
        row_i = lax.broadcasted_iota(jnp.int32, (tq, tq), 0)
        col_i = lax.broadcasted_iota(jnp.int32, (tq, tq), 1)
        for j in range(ATT_B_NKB):
            back = (ATT_B_NKB - 1 - j) * tq
            if consecutive:
                rel = row_i - (col_i - back)
            else:
                rel = posq_ref[...] - posk_ref[jnp.maximum(i - (ATT_B_NKB - 1) + j, 0)]
            r = jnp.clip(rel, -B_REL_CLIP, B_REL_CLIP) + B_REL_CLIP
            dchunk = (row_i >> CHUNK_SHIFT) - ((col_i - back) >> CHUNK_SHIFT)
            band = (dchunk >= 0) & (dchunk <= B_PREV_CHUNKS)
            for hl in range(ATT_B_LG * nh):
                h = gp * (ATT_B_LG * nh) + hl
                row = relt_ref[pl.ds(h, 1), :] * LOG2E
                seg0 = jnp.broadcast_to(row[:, :LANE], (tq, LANE))
                seg1 = jnp.broadcast_to(row[:, LANE:2 * LANE], (tq, LANE))
                t256 = t256_ref[h] * LOG2E
                for c in range(tq // LANE):
                    cs = slice(c * LANE, (c + 1) * LANE)
                    rc = r[:, cs]
                    lo = rc & (LANE - 1)
                    bias = jnp.where(rc < LANE, _lane_gather(seg0, lo),
                                     jnp.where(rc < 2 * LANE, _lane_gather(seg1, lo), t256))
                    bias_ref[hl, :, j * tq + c * LANE:j * tq + (c + 1) * LANE] = jnp.where(band[:, cs], bias, -jnp.inf)

    if consecutive:
        pl.when(i == 0)(build_bias)
    else:
        build_bias()

    lane_head = lax.broadcasted_iota(jnp.int32, (1, LANE), 1) >> CHUNK_SHIFT
    for lg in range(ATT_B_LG):
        lanes = slice(lg * LANE, (lg + 1) * LANE)
        qp = q_ref[:, lanes]
        out = jnp.zeros((tq, LANE), F32)
        for hh in range(nh):
            mine = lane_head == hh
            qm = jnp.where(mine, qp, jnp.zeros_like(qp))
            parts = []
            for j in range(ATT_B_NKB):
                sj = lax.dot_general(qm, k_refs[j][:, lanes], NT_DIMS, preferred_element_type=F32)
                sj = sj + bias_ref[lg * nh + hh, :, j * tq:(j + 1) * tq]
                if j < ATT_B_NKB - 1:
                    sj = sj + jnp.where(i - (ATT_B_NKB - 1) + j >= 0, 0.0, -jnp.inf)
                parts.append(sj)
            s = jnp.concatenate(parts, axis=1)
            p = jnp.exp2(s - jnp.max(s, axis=1, keepdims=True)).astype(BF16)
            acc = jnp.zeros((tq, LANE), F32)
            for j in range(ATT_B_NKB):
                vj = v_refs[j][:, lanes]
                vm = jnp.where(mine, vj, jnp.ones_like(vj))
                acc = acc + jnp.dot(p[:, j * tq:(j + 1) * tq], vm, preferred_element_type=F32)
            rowsum = pltpu.roll(acc, B_HEAD_DIM, 1)
            out = out + jnp.where(mine, acc / rowsum, 0.0)
        o_ref[:, lanes] = out.astype(o_ref.dtype)


def band_attention(qkv, pos, rel_table, *, consecutive):
    s = qkv.shape[0]
    tq = ATT_B_TQ
    hw = B_HEADS * B_HEAD_DIM
    bw = ATT_B_LG * LANE
    ngrp = hw // bw
    assert LANE // B_HEAD_DIM == 2
    nrel = 2 * B_REL_CLIP + 1
    relt = jnp.zeros((B_HEADS, 3 * LANE), F32).at[:, :nrel].set(rel_table.T)
    t256 = rel_table[nrel - 1]

    def kv_spec(j, base):
        return pl.BlockSpec((tq, bw), lambda gp, i: (jnp.maximum(i - (ATT_B_NKB - 1) + j, 0), base + gp))

    return pl.pallas_call(
        functools.partial(_attn_b_kernel, consecutive=consecutive),
        name="band_attention",
        grid=(ngrp, s // tq),
        in_specs=[
            pl.BlockSpec(memory_space=pltpu.SMEM),
            pl.BlockSpec((tq, bw), lambda gp, i: (i, gp)),
            kv_spec(0, ngrp), kv_spec(1, ngrp), kv_spec(2, ngrp),
            kv_spec(0, 2 * ngrp), kv_spec(1, 2 * ngrp), kv_spec(2, 2 * ngrp),
            pl.BlockSpec((tq, 1), lambda gp, i: (i, 0)),
            pl.BlockSpec((s // tq, 1, tq), lambda gp, i: (0, 0, 0)),
            pl.BlockSpec((B_HEADS, 3 * LANE), lambda gp, i: (0, 0)),
        ],
        out_specs=pl.BlockSpec((tq, bw), lambda gp, i: (i, gp)),
        out_shape=jax.ShapeDtypeStruct((s, hw), BF16),
        scratch_shapes=[pltpu.VMEM((ATT_B_LG * (LANE // B_HEAD_DIM), tq, ATT_B_NKB * tq), F32)],
        compiler_params=_params(("parallel", "arbitrary")),
    )(t256, qkv, qkv, qkv, qkv, qkv, qkv, qkv, pos.reshape(s, 1), pos.reshape(s // tq, 1, tq), relt)


ATT_C_T = 512
ATT_C_HEADS = 2


def _attn_c_kernel(q_ref, kn_ref, kr_ref, v_ref, o_ref, m_ref, l_ref, acc_ref):
    t, nh = ATT_C_T, ATT_C_HEADS
    iq = pl.program_id(1)
    m_ref[...] = jnp.full(m_ref.shape, NEG_BIG, F32)
    l_ref[...] = jnp.zeros(l_ref.shape, F32)
    acc_ref[...] = jnp.zeros(acc_ref.shape, F32)

    def tile(kt, diag):
        k0 = pl.multiple_of(kt * t, t)
        kr_t = kr_ref[pl.ds(k0, t), :]
        for hh in range(nh):
            kcat = jnp.concatenate([kn_ref[pl.ds(k0, t), hh * C_NOPE:(hh + 1) * C_NOPE], kr_t], axis=1)
            s = lax.dot_general(q_ref[:, hh * 2 * LANE:(hh + 1) * 2 * LANE], kcat, NT_DIMS,
                                preferred_element_type=F32)
            if diag:
                qc = lax.broadcasted_iota(jnp.int32, (t, t), 0) >> CHUNK_SHIFT
                kc = lax.broadcasted_iota(jnp.int32, (t, t), 1) >> CHUNK_SHIFT
                s = jnp.where(kc <= qc, s, -jnp.inf)
            _softmax_step(s, v_ref[pl.ds(k0, t), hh * C_V:(hh + 1) * C_V], m_ref, l_ref, acc_ref,
                          slice(hh * t, (hh + 1) * t), t)

    _for_tiles_in_pairs(0, iq, lambda kt: tile(kt, False))
    tile(iq, True)
    for hh in range(nh):
        rows = slice(hh * t, (hh + 1) * t)
        o_ref[:, hh * C_V:(hh + 1) * C_V] = (acc_ref[rows, :] / l_ref[rows, :]).astype(o_ref.dtype)


def mla_attention(qcat, kv, kr):
    s = qcat.shape[0]
    t, nh = ATT_C_T, ATT_C_HEADS
    assert s % t == 0
    return pl.pallas_call(
        _attn_c_kernel,
        name="mla_attention",
        grid=(C_HEADS // nh, s // t),
        in_specs=[
            pl.BlockSpec((t, nh * 2 * LANE), lambda h, i: (i, h)),
            pl.BlockSpec((s, nh * C_NOPE), lambda h, i: (0, h)),
            pl.BlockSpec((s, LANE), lambda h, i: (0, 0)),
            pl.BlockSpec((s, nh * C_V), lambda h, i: (0, C_HEADS // nh + h)),
        ],
        out_specs=pl.BlockSpec((t, nh * C_V), lambda h, i: (i, h)),
        out_shape=jax.ShapeDtypeStruct((s, C_HEADS * C_V), BF16),
        scratch_shapes=[pltpu.VMEM((nh * t, LANE), F32), pltpu.VMEM((nh * t, LANE), F32),
                        pltpu.VMEM((nh * t, C_V), F32)],
        compiler_params=_params(("parallel", "arbitrary")),
    )(qcat, kv, kr, kv)


def _rope_tables(pos):
    half = ROPE_DIM // 2
    inv = ROPE_BASE ** (-jnp.arange(half, dtype=F32) * 2.0 / ROPE_DIM)
    ang = pos.astype(F32)[:, None] * inv
    cos, sin = jnp.cos(ang), jnp.sin(ang)
    s = pos.shape[0]
    z = jnp.zeros((s, half), F32)
    c = jnp.concatenate([cos, cos, jnp.ones((s, LANE - ROPE_DIM), F32)], axis=1)
    s1 = jnp.concatenate([-sin, z, z, z], axis=1)
    s2 = jnp.concatenate([z, sin, z, z], axis=1)
    return c, s1, s2


def _query_scale(n_query, n_total, scale):
    return jnp.concatenate([jnp.full((n_query,), scale * LOG2E, F32), jnp.ones((n_total - n_query,), F32)])


def _by_position_layout(consecutive, fn, *args):
    return lax.cond(consecutive, functools.partial(fn, consecutive=True),
                    functools.partial(fn, consecutive=False), *args)


def _mixer_a(h, gain, pos, consecutive, tables, w_in, w_out, layer, t5_table):
    s = h.shape[0]
    d = h.shape[1]
    nq = A_HEADS * A_HEAD_DIM
    nkv = A_KV_HEADS * A_HEAD_DIM
    nqi = IDX_HEADS * IDX_DIM
    o_qkv = nq + 2 * nkv
    o_ki = o_qkv + nqi
    tn = 1024
    assert o_qkv % tn == 0 and nqi % tn == 0
    qkv = norm_mm(h, 0, d, gain, w_in, layer=layer, n=o_qkv, tn=tn, out_dtype=BF16, name="a_qkv_proj",
                  col_scale=_query_scale(nq, o_qkv, A_HEAD_DIM ** -0.5))
    qi = norm_mm(h, 0, d, gain, w_in, layer=layer, n=nqi, col0=o_qkv // tn, tn=tn, out_dtype=BF16,
                 rope=(True,) * (tn // LANE), tables=tables, name="a_idxq_proj")
    w_kw = jnp.zeros((d, 2 * LANE), BF16).at[:, :IDX_DIM + IDX_HEADS].set(w_in[layer, :, o_ki:].astype(BF16))
    kiwi = norm_mm(h, 0, d, gain, w_kw, tn=2 * LANE, out_dtype=F32, rope=(True, False), tables=tables,
                   name="a_idxk_proj")
    mask = dsa_indexer(qi, kiwi, topk=min(IDX_TOPK_MAX, s // 4))
    o = _by_position_layout(consecutive, dsa_attention, qkv, mask, pos, t5_table)
    return mm_residual(o, w_out, h, layer=layer, tn=w_out.shape[-1])


def _mixer_b(h, gain, pos, consecutive, w_in, rel_table, w_out, layer):
    d = h.shape[1]
    n = w_in.shape[-1]
    qkv = norm_mm(h, 0, d, gain, w_in, layer=layer, tn=1024, out_dtype=BF16, name="b_qkv_proj",
                  col_scale=_query_scale(n // 3, n, B_HEAD_DIM ** -0.5))
    o = _by_position_layout(consecutive, band_attention, qkv, pos, rel_table)
    return mm_residual(o, w_out, h, layer=layer, tn=w_out.shape[-1])


def _mixer_c(h, gain, tables, w_down, g_q, g_kv, w_uq, w_ukv, w_out, layer):
    d = h.shape[1]
    lq = g_q.shape[0]
    lkv = g_kv.shape[0]
    n_down = lq + lkv + LANE
    wd = jnp.zeros((d, n_down), BF16).at[:, :w_down.shape[1]].set(w_down.astype(BF16))
    down = norm_mm(h, 0, d, gain, wd, tn=n_down, out_dtype=F32,
                   rope=(False,) * ((lq + lkv) // LANE) + (True,), tables=tables, name="c_down_proj")
    wq = w_uq.astype(BF16).reshape(lq, C_HEADS, C_NOPE + C_ROPE)
    wq = jnp.pad(wq, ((0, 0), (0, 0), (0, 2 * LANE - C_NOPE - C_ROPE))).reshape(lq, C_HEADS * 2 * LANE)
    nqc = wq.shape[1]
    qcat = norm_mm(down, 0, lq, g_q, wq, tn=nqc, out_dtype=BF16, rope=(False, True) * C_HEADS, tables=tables,
                   name="c_uq_proj", col_scale=_query_scale(nqc, nqc, (C_NOPE + C_ROPE) ** -0.5))
    wkv = w_ukv.astype(BF16).reshape(lkv, C_HEADS, 2, C_NOPE).transpose(0, 2, 1, 3).reshape(lkv, -1)
    assert lq == lkv
    kv = norm_mm(down, 1, lkv, g_kv, wkv, tn=wkv.shape[1], out_dtype=BF16, name="c_ukv_proj")
    kr = down[:, lq + lkv:].astype(BF16)
    o = mla_attention(qcat, kv, kr)
    return mm_residual(o, w_out, h, layer=layer, tn=w_out.shape[-1])


def kernel(x, p, positions, t5_table, a_w_in, a_w_out, b_w_in, b_rel_table, b_w_out, c_w_down, c_q_norm,
           c_kv_norm, c_w_uq, c_w_ukv, c_w_out, attn_norm, ffn_norm, ffn_w_in, ffn_w_out, ple_norm,
           ple_w_gate, ple_w_proj, final_norm):
    assert x.shape[0] == 1
    depth = attn_norm.shape[0]
    h = x[0]
    pos = positions[0]
    tables = _rope_tables(pos)
    consecutive = jnp.all(pos[1:] - pos[:-1] == 1)
    a_w_in, a_w_out, b_w_in, b_w_out, c_w_out, ffn_w_in, ffn_w_out, ple_w_gate, ple_w_proj = (
        w.astype(BF16) for w in (a_w_in, a_w_out, b_w_in, b_w_out, c_w_out, ffn_w_in, ffn_w_out, ple_w_gate,
                                 ple_w_proj))
    for i in range(depth):
        j, kind = divmod(i, 3)
        if kind == 0:
            h = _mixer_a(h, attn_norm[i], pos, consecutive, tables, a_w_in, a_w_out, j, t5_table)
        elif kind == 1:
            h = _mixer_b(h, attn_norm[i], pos, consecutive, b_w_in, b_rel_table[j], b_w_out, j)
        else:
            h = _mixer_c(h, attn_norm[i], tables, c_w_down[j], c_q_norm[j], c_kv_norm[j], c_w_uq[j],
                         c_w_ukv[j], c_w_out, j)
        act = ffn_in(h, ffn_norm[i], ffn_w_in, layer=i)
        h = mm_residual(act, ffn_w_out, h, layer=i, tn=1024)
        h = ple(h, ple_norm[i], ple_w_gate, p, ple_w_proj, layer=i)
    return final_rms_norm(h, final_norm)[None]
```

```python
import functools
import math

import jax
import jax.numpy as jnp
from jax import lax
from jax.experimental import pallas as pl
from jax.experimental.pallas import tpu as pltpu

LANE = 128
VMEM_LIMIT_BYTES = 56 * 1024 * 1024

CHUNK = 64
CHUNK_SHIFT = CHUNK.bit_length() - 1
EPS = 1e-6
ROPE_BASE = 10000.0
ROPE_DIM = 64
A_HEADS = 16
A_KV_HEADS = 4
A_GROUP = 4
A_HEAD_DIM = 128
IDX_HEADS = 16
IDX_DIM = 128
IDX_TOPK_MAX = 256
T5_BUCKETS = 32
T5_MAX_DISTANCE = 1024
_T5_EXACT = T5_BUCKETS // 4
_T5_LAST = _T5_EXACT * (T5_MAX_DISTANCE / _T5_EXACT) ** ((_T5_EXACT - 1) / _T5_EXACT)
T5_FAR = math.ceil(_T5_LAST / LANE) * LANE
assert T5_FAR == 640
B_HEADS = 32
B_HEAD_DIM = 64
B_PREV_CHUNKS = 8
B_REL_CLIP = 128
C_HEADS = 16
C_NOPE = 128
C_ROPE = 64
C_V = 128
NEG_BIG = -1e30
INT_MIN = -(2 ** 31)
LOG2E = math.log2(math.e)

F32 = jnp.float32
BF16 = jnp.bfloat16
NT_DIMS = (((1,), (1,)), ((), ()))


def _params(sem):
    return pltpu.CompilerParams(dimension_semantics=sem, vmem_limit_bytes=VMEM_LIMIT_BYTES)


def _rms(x, g):
    ms = jnp.mean(x * x, axis=-1, keepdims=True)
    return (x * lax.rsqrt(ms + EPS)) * g


def _lane_gather(table, idx):
    return jnp.take_along_axis(table, idx, axis=1, mode="promise_in_bounds")


def _rope_group(y, c, s1, s2):
    return y * c + pltpu.roll(y, 96, 1) * s1 + pltpu.roll(y, 32, 1) * s2


def _norm_mm_kernel(*refs, rope, scaled):
    x_ref, g_ref, w_ref = refs[:3]
    rest = list(refs[3:])
    cs_ref = rest.pop(0) if scaled else None
    c_ref, s1_ref, s2_ref = (rest.pop(0), rest.pop(0), rest.pop(0)) if rope is not None else (None,) * 3
    o_ref, xn_ref = rest

    @pl.when(pl.program_id(1) == 0)
    def _():
        xn_ref[...] = _rms(x_ref[...], g_ref[...]).astype(BF16)

    y = jnp.dot(xn_ref[...], w_ref[...], preferred_element_type=F32)
    if scaled:
        y = y * cs_ref[...]
    if rope is None:
        o_ref[...] = y.astype(o_ref.dtype)
    else:
        c, s1, s2 = c_ref[...], s1_ref[...], s2_ref[...]
        for gi, on in enumerate(rope):
            sl = slice(gi * LANE, (gi + 1) * LANE)
            yg = y[:, sl]
            if on:
                yg = _rope_group(yg, c, s1, s2)
            o_ref[:, sl] = yg.astype(o_ref.dtype)


def _w_spec(w, layer, k, tn, col0=0):
    if w.ndim == 2:
        return pl.BlockSpec((k, tn), lambda i, j: (0, col0 + j))
    return pl.BlockSpec((None, k, tn), lambda i, j: (layer, 0, col0 + j))


def norm_mm(x, x_col, kx, gain, w, *, tn, out_dtype, name, n=None, layer=None, col0=0, tm=512, rope=None,
            tables=None, col_scale=None):
    s = x.shape[0]
    n = w.shape[-1] if n is None else n
    tm = min(tm, s)
    in_specs = [
        pl.BlockSpec((tm, kx), lambda i, j: (i, x_col)),
        pl.BlockSpec((1, kx), lambda i, j: (0, 0)),
        _w_spec(w, layer, kx, tn, col0),
    ]
    args = [x, gain.reshape(1, kx), w]
    if col_scale is not None:
        in_specs.append(pl.BlockSpec((1, tn), lambda i, j: (0, j)))
        args.append(col_scale.reshape(1, n))
    if rope is not None:
        assert len(rope) == tn // LANE
        in_specs += [pl.BlockSpec((tm, LANE), lambda i, j: (i, 0))] * 3
        args += list(tables)
    return pl.pallas_call(
        functools.partial(_norm_mm_kernel, rope=rope, scaled=col_scale is not None),
        grid=(s // tm, n // tn),
        in_specs=in_specs,
        out_specs=pl.BlockSpec((tm, tn), lambda i, j: (i, j)),
        out_shape=jax.ShapeDtypeStruct((s, n), out_dtype),
        scratch_shapes=[pltpu.VMEM((tm, kx), BF16)],
        compiler_params=_params(("parallel", "arbitrary")),
        name=name,
    )(*args)


def _mm_res_kernel(x_ref, w_ref, r_ref, o_ref):
    o_ref[...] = r_ref[...] + jnp.dot(x_ref[...], w_ref[...], preferred_element_type=F32)


def mm_residual(x, w, res, *, layer=None, tm=512, tn=512):
    s, k = x.shape
    n = w.shape[-1]
    tm = min(tm, s)
    return pl.pallas_call(
        _mm_res_kernel,
        name="mm_residual",
        grid=(s // tm, n // tn),
        in_specs=[
            pl.BlockSpec((tm, k), lambda i, j: (i, 0)),
            _w_spec(w, layer, k, tn),
            pl.BlockSpec((tm, tn), lambda i, j: (i, j)),
        ],
        out_specs=pl.BlockSpec((tm, tn), lambda i, j: (i, j)),
        out_shape=jax.ShapeDtypeStruct((s, n), F32),
        compiler_params=_params(("parallel", "arbitrary")),
    )(x, w, res)


def _ffn_in_kernel(x_ref, g_ref, wg_ref, wu_ref, o_ref, xn_ref):
    @pl.when(pl.program_id(1) == 0)
    def _():
        xn_ref[...] = _rms(x_ref[...], g_ref[...]).astype(BF16)

    xn = xn_ref[...]
    a = jnp.dot(xn, wg_ref[...], preferred_element_type=F32)
    u = jnp.dot(xn, wu_ref[...], preferred_element_type=F32)
    o_ref[...] = (a * jax.nn.sigmoid(a) * u).astype(o_ref.dtype)


def ffn_in(h, gain, w_in, *, layer=None, tm=1024, tn=512):
    s, d = h.shape
    f = w_in.shape[-1] // 2
    tm = min(tm, s)
    nj = f // tn
    return pl.pallas_call(
        _ffn_in_kernel,
        name="ffn_in",
        grid=(s // tm, nj),
        in_specs=[
            pl.BlockSpec((tm, d), lambda i, j: (i, 0)),
            pl.BlockSpec((1, d), lambda i, j: (0, 0)),
            _w_spec(w_in, layer, d, tn),
            _w_spec(w_in, layer, d, tn, nj),
        ],
        out_specs=pl.BlockSpec((tm, tn), lambda i, j: (i, j)),
        out_shape=jax.ShapeDtypeStruct((s, f), BF16),
        scratch_shapes=[pltpu.VMEM((tm, d), BF16)],
        compiler_params=_params(("parallel", "arbitrary")),
    )(h, gain.reshape(1, d), w_in, w_in)


def _ple_kernel(x_ref, g_ref, wg_ref, p_ref, wp_ref, h_ref, o_ref, xn_ref, pb_ref):
    @pl.when(pl.program_id(1) == 0)
    def _():
        xn_ref[...] = _rms(x_ref[...], g_ref[...]).astype(BF16)
        pb_ref[...] = p_ref[...].astype(BF16)

    gate = jax.nn.sigmoid(jnp.dot(xn_ref[...], wg_ref[...], preferred_element_type=F32))
    proj = jnp.dot(pb_ref[...], wp_ref[...], preferred_element_type=F32)
    o_ref[...] = h_ref[...] + gate * proj


def ple(h, gain, w_gate, p, w_proj, *, layer, tm=512, tn=2048):
    s, d = h.shape
    pd = p.shape[-1]
    tm = min(tm, s)
    return pl.pallas_call(
        _ple_kernel,
        name="ple",
        grid=(s // tm, d // tn),
        in_specs=[
            pl.BlockSpec((tm, d), lambda i, j: (i, 0)),
            pl.BlockSpec((1, d), lambda i, j: (0, 0)),
            _w_spec(w_gate, layer, d, tn),
            pl.BlockSpec((None, None, tm, pd), lambda i, j: (layer, 0, i, 0)),
            _w_spec(w_proj, layer, pd, tn),
            pl.BlockSpec((tm, tn), lambda i, j: (i, j)),
        ],
        out_specs=pl.BlockSpec((tm, tn), lambda i, j: (i, j)),
        out_shape=jax.ShapeDtypeStruct((s, d), F32),
        scratch_shapes=[pltpu.VMEM((tm, d), BF16), pltpu.VMEM((tm, pd), BF16)],
        compiler_params=_params(("parallel", "arbitrary")),
    )(h, gain.reshape(1, d), w_gate, p, w_proj, h)


def _final_norm_kernel(x_ref, g_ref, o_ref):
    o_ref[...] = _rms(x_ref[...], g_ref[...])


def final_rms_norm(h, gain, *, tm=512):
    s, d = h.shape
    tm = min(tm, s)
    return pl.pallas_call(
        _final_norm_kernel,
        name="final_norm",
        grid=(s // tm,),
        in_specs=[pl.BlockSpec((tm, d), lambda i: (i, 0)), pl.BlockSpec((1, d), lambda i: (0, 0))],
        out_specs=pl.BlockSpec((tm, d), lambda i: (i, 0)),
        out_shape=jax.ShapeDtypeStruct((s, d), F32),
        compiler_params=_params(("parallel",)),
    )(h, gain.reshape(1, d))


IDX_TQ = 256
IDX_KC = 256


def _orderable(x):
    b = pltpu.bitcast(x, jnp.int32)
    return jnp.where(b < 0, b ^ jnp.int32(0x7FFFFFFF), b)


def _indexer_kernel(qi_ref, ki_ref, wi_ref, mask_ref, keys_ref, *, topk, n_kblocks):
    tq, kc = IDX_TQ, IDX_KC
    i = pl.program_id(0)
    nkc = (i + 1) * (tq // kc)
    n_idx_bits = (n_kblocks * kc - 1).bit_length()
    w_t = (wi_ref[...] * (IDX_HEADS ** -0.5 * IDX_DIM ** -0.5)).T
    qchunk = (i * tq + lax.broadcasted_iota(jnp.int32, (kc, tq), 1)) >> CHUNK_SHIFT
    krow = lax.broadcasted_iota(jnp.int32, (kc, tq), 0)

    def valid_of(c):
        return ((c * kc + krow) >> CHUNK_SHIFT) <= qchunk

    def score_body(c, carry):
        k0 = pl.multiple_of(c * kc, kc)
        kblk = ki_ref[pl.ds(k0, kc), :].astype(BF16)
        acc = jnp.zeros((kc, tq), F32)
        for h in range(IDX_HEADS):
            s = lax.dot_general(kblk, qi_ref[:, h * IDX_DIM:(h + 1) * IDX_DIM], NT_DIMS,
                                preferred_element_type=F32)
            acc = acc + jnp.maximum(s, 0.0) * w_t[h:h + 1, :]
        keys_ref[pl.ds(k0, kc), :] = jnp.where(valid_of(c), _orderable(acc), jnp.int32(INT_MIN))
        return carry

    lax.fori_loop(0, nkc, score_body, 0)

    @pl.when(nkc % 2 == 1)
    def _():
        keys_ref[pl.ds(pl.multiple_of(nkc * kc, kc), kc), :] = jnp.full((kc, tq), INT_MIN, jnp.int32)

    def count(pred_of):
        nacc = 4

        def body(cp, cnt):
            for c in (2 * cp, 2 * cp + 1):
                k0 = pl.multiple_of(c * kc, kc)
                hit = pred_of(c, keys_ref[pl.ds(k0, kc), :]).astype(jnp.int32)
                cnt = cnt + hit.reshape(kc // (8 * nacc), nacc, 8, tq).sum(axis=0)
            return cnt
        cnt = lax.fori_loop(0, (nkc + 1) // 2, body, jnp.zeros((nacc, 8, tq), jnp.int32))
        return cnt.sum(axis=0).sum(axis=0, keepdims=True)

    def bit_step(b, state):
        thr, cnt = state
        cand = thr + lax.shift_left(jnp.int32(1), 31 - b)
        c_cand = count(lambda c, keys: keys >= cand)
        take = c_cand >= topk
        return jnp.where(take, cand, thr), jnp.where(take, c_cand, cnt)

    thr, cnt = lax.fori_loop(
        0, 32, bit_step,
        (jnp.full((1, tq), INT_MIN, jnp.int32), jnp.full((1, tq), nkc * kc, jnp.int32)))

    def tie_limit():
        need = topk - count(lambda c, keys: keys > thr)
        def bit_body(b, x):
            cand = x + lax.shift_left(jnp.int32(1), n_idx_bits - 1 - b)
            below = count(lambda c, keys: (keys == thr) & ((c * kc + krow) < cand))
            return jnp.where(below < need, cand, x)
        return lax.fori_loop(0, n_idx_bits, bit_body, jnp.zeros((1, tq), jnp.int32))

    def write_masks(selected):
        def mask_body(c, carry):
            k0 = pl.multiple_of(c * kc, kc)
            sel = selected(c, keys_ref[pl.ds(k0, kc), :]) & valid_of(c)
            mask_ref[c] = jnp.where(sel, 0.0, -jnp.inf).astype(F32).T
            return carry
        lax.fori_loop(0, nkc, mask_body, 0)

    has_ties = jnp.max(jnp.where((cnt > topk) & (thr > INT_MIN), 1, 0)) > 0

    @pl.when(has_ties)
    def _():
        tie_last = tie_limit()
        write_masks(lambda c, keys: (keys > thr) | ((keys == thr) & ((c * kc + krow) <= tie_last)))

    @pl.when(jnp.logical_not(has_ties))
    def _():
        write_masks(lambda c, keys: keys >= thr)

    def fill_body(c, carry):
        mask_ref[c] = jnp.full((tq, kc), -jnp.inf, F32)
        return carry

    lax.fori_loop(nkc, n_kblocks, fill_body, 0)


def dsa_indexer(qi, kiwi, *, topk):
    s = qi.shape[0]
    tq, kc = IDX_TQ, IDX_KC
    nkb = s // kc
    assert s % tq == 0 and nkb % 2 == 0
    return pl.pallas_call(
        functools.partial(_indexer_kernel, topk=topk, n_kblocks=nkb),
        name="dsa_indexer",
        grid=(s // tq,),
        in_specs=[
            pl.BlockSpec((tq, IDX_HEADS * IDX_DIM), lambda i: (i, 0)),
            pl.BlockSpec((s, IDX_DIM), lambda i: (0, 0)),
            pl.BlockSpec((tq, LANE), lambda i: (i, 1)),
        ],
        out_specs=pl.BlockSpec((nkb, tq, kc), lambda i: (0, i, 0)),
        out_shape=jax.ShapeDtypeStruct((nkb, s, kc), F32),
        scratch_shapes=[pltpu.VMEM((s, tq), jnp.int32)],
        compiler_params=_params(("parallel",)),
    )(qi, kiwi, kiwi)


ATT_A_TQ = 256
ATT_A_KB = 2
ATT_A_NEAR = (T5_FAR + ATT_A_KB * IDX_KC - 2) // IDX_KC + 1


def _t5_bucket(rel):
    nb = T5_BUCKETS // 2
    max_exact = nb // 2
    ret = jnp.where(rel > 0, nb, 0)
    n = jnp.abs(rel)
    nf = jnp.maximum(n, 1).astype(F32)
    large = max_exact + (jnp.log(nf / max_exact) / math.log(T5_MAX_DISTANCE / max_exact)
                         * (nb - max_exact)).astype(jnp.int32)
    large = jnp.minimum(large, nb - 1)
    return ret + jnp.where(n < max_exact, n, large)


def _softmax_step(s, v, m_ref, l_ref, acc_ref, rows, tk, shift=None):
    d = v.shape[1]
    m_prev = m_ref[rows, :]
    mx = jnp.max(s, axis=1, keepdims=True)
    if shift is not None:
        mx = mx + shift
    m_next = jnp.maximum(m_prev, mx)
    pivot = m_next if shift is None else m_next - shift
    p = jnp.exp2(s - jnp.concatenate([pivot] * (tk // LANE), axis=1)).astype(BF16)
    alpha = jnp.exp2(m_prev - m_next)
    v_ones = jnp.concatenate([v, jnp.ones((tk, LANE), BF16)], axis=1)
    pv = jnp.dot(p, v_ones, preferred_element_type=F32)
    m_ref[rows, :] = m_next
    l_ref[rows, :] = alpha * l_ref[rows, :] + pv[:, d:]
    acc_ref[rows, :] = acc_ref[rows, :] * alpha + pv[:, :d]


def _for_tiles_in_pairs(lo, hi, tile_fn, per_trip=(2, 1)):
    assert per_trip[-1] == 1
    start = lo
    for size in per_trip:
        def group(kp, carry, size=size, start=start):
            for u in range(size):
                tile_fn(start + size * kp + u)
            return carry

        n = (hi - start) // size
        lax.fori_loop(0, n, group, 0)
        start = start + n * size


def _attn_a_kernel(qmin_ref, kmax_ref, t5s_ref, q_ref, k_ref, v_ref, mask_ref, posq_ref, posk_ref,
                   t5t_ref, o_ref, m_ref, l_ref, acc_ref, nbias_ref, *, consecutive):
    tq, kc, nb = ATT_A_TQ, IDX_KC, ATT_A_KB
    tk = nb * kc
    g = pl.program_id(0)
    i = pl.program_id(1)

    def head_bias(r, bucket):
        row = t5t_ref[pl.ds(g * A_GROUP + r, 1), :] * LOG2E
        tbl = jnp.broadcast_to(row, (tq, LANE))
        return jnp.concatenate([_lane_gather(tbl, bucket[:, c * LANE:(c + 1) * LANE])
                                for c in range(tk // LANE)], axis=1)

    if consecutive:
        @pl.when(i == 0)
        def _():
            rel0 = (lax.broadcasted_iota(jnp.int32, (tq, tk), 1)
                    - lax.broadcasted_iota(jnp.int32, (tq, tk), 0))
            for d in range(ATT_A_NEAR):
                bucket = _t5_bucket(rel0 + (d - ATT_A_NEAR + 1) * kc)
                for r in range(A_GROUP):
                    nbias_ref[d, r] = head_bias(r, bucket)

    m_ref[...] = jnp.full(m_ref.shape, NEG_BIG, F32)
    l_ref[...] = jnp.zeros(l_ref.shape, F32)
    acc_ref[...] = jnp.zeros(acc_ref.shape, F32)
    nkt = ((i + 1) * (tq // kc) + nb - 1) // nb
    qmin = qmin_ref[i]

    def is_far(kt):
        ktc = jnp.minimum(kt, nkt - 1) * nb
        kmax = kmax_ref[ktc]
        for b in range(1, nb):
            kmax = jnp.maximum(kmax, kmax_ref[ktc + b])
        return (kt < nkt) & ((qmin - kmax) >= T5_FAR)

    n_far = lax.while_loop(is_far, lambda kt: kt + 1, jnp.int32(0))

    def tile(kt, bias_of, shift_of):
        k0 = pl.multiple_of(kt * tk, tk)
        k_t = k_ref[pl.ds(k0, tk), :]
        v_t = v_ref[pl.ds(k0, tk), :]
        base = jnp.concatenate([mask_ref[kt * nb + b] for b in range(nb)], axis=1)
        for r in range(A_GROUP):
            rows = slice(r * tq, (r + 1) * tq)
            q_r = q_ref[:, r * A_HEAD_DIM:(r + 1) * A_HEAD_DIM]
            s = lax.dot_general(q_r, k_t, NT_DIMS, preferred_element_type=F32) + bias_of(r, base)
            _softmax_step(s, v_t, m_ref, l_ref, acc_ref, rows, tk, shift_of(r))

    def far_tile(kt):
        tile(kt, lambda r, base: base,
             lambda r: t5s_ref[T5_BUCKETS // 2 - 1, g * A_GROUP + r] * LOG2E)

    def near_tile(kt):
        if consecutive:
            d = kt * nb - i * (tq // kc) + (ATT_A_NEAR - 1)
            tile(kt, lambda r, base: base + nbias_ref[d, r], lambda r: None)
        else:
            pk = jnp.concatenate([posk_ref[kt * nb + b] for b in range(nb)], axis=1)
            bucket = _t5_bucket(pk - posq_ref[...])
            tile(kt, lambda r, base: base + head_bias(r, bucket), lambda r: None)

    _for_tiles_in_pairs(0, n_far, far_tile, per_trip=(4, 2, 1))
    _for_tiles_in_pairs(n_far, nkt, near_tile)
    for r in range(A_GROUP):
        rows = slice(r * tq, (r + 1) * tq)
        o_ref[:, r * A_HEAD_DIM:(r + 1) * A_HEAD_DIM] = (acc_ref[rows, :] / l_ref[rows, :]).astype(o_ref.dtype)


def dsa_attention(qkv, mask, pos, t5_table, *, consecutive):
    s = qkv.shape[0]
    tq, tk = ATT_A_TQ, IDX_KC
    assert (s // tk) % ATT_A_KB == 0
    gw = A_GROUP * A_HEAD_DIM
    qmin = pos.reshape(s // tq, tq).min(axis=1)
    kmax = pos.reshape(s // tk, tk).max(axis=1)
    t5t = jnp.zeros((A_HEADS, LANE), F32).at[:, :T5_BUCKETS].set(t5_table.T)
    kblk0 = (A_HEADS * A_HEAD_DIM) // A_HEAD_DIM
    smem = pl.BlockSpec(memory_space=pltpu.SMEM)
    nbias_shape = (ATT_A_NEAR, A_GROUP, tq, ATT_A_KB * tk) if consecutive else (1, 1, 8, LANE)
    return pl.pallas_call(
        functools.partial(_attn_a_kernel, consecutive=consecutive),
        name="dsa_attention",
        grid=(A_KV_HEADS, s // tq),
        in_specs=[
            smem, smem, smem,
            pl.BlockSpec((tq, gw), lambda g, i: (i, g)),
            pl.BlockSpec((s, A_HEAD_DIM), lambda g, i: (0, kblk0 + g)),
            pl.BlockSpec((s, A_HEAD_DIM), lambda g, i: (0, kblk0 + A_KV_HEADS + g)),
            pl.BlockSpec((s // tk, tq, tk), lambda g, i: (0, i, 0)),
            pl.BlockSpec((tq, 1), lambda g, i: (i, 0)),
            pl.BlockSpec((s // tk, 1, tk), lambda g, i: (0, 0, 0)),
            pl.BlockSpec((A_HEADS, LANE), lambda g, i: (0, 0)),
        ],
        out_specs=pl.BlockSpec((tq, gw), lambda g, i: (i, g)),
        out_shape=jax.ShapeDtypeStruct((s, A_HEADS * A_HEAD_DIM), BF16),
        scratch_shapes=[
            pltpu.VMEM((A_GROUP * tq, LANE), F32),
            pltpu.VMEM((A_GROUP * tq, LANE), F32),
            pltpu.VMEM((A_GROUP * tq, A_HEAD_DIM), F32),
            pltpu.VMEM(nbias_shape, F32),
        ],
        compiler_params=_params(("parallel", "arbitrary")),
    )(qmin, kmax, t5_table, qkv, qkv, qkv, mask, pos.reshape(s, 1), pos.reshape(s // tk, 1, tk), t5t)


ATT_B_TQ = 256
ATT_B_NKB = 3
ATT_B_LG = 2


def _attn_b_kernel(t256_ref, q_ref, k0_ref, k1_ref, k2_ref, v0_ref, v1_ref, v2_ref, posq_ref, posk_ref,
                   relt_ref, o_ref, bias_ref, *, consecutive):
    tq = ATT_B_TQ
    nh = LANE // B_HEAD_DIM
    gp = pl.program_id(0)
    i = pl.program_id(1)
    k_refs = (k0_ref, k1_ref, k2_ref)
    v_refs = (v0_ref, v1_ref, v2_ref)

    def build_bias():
        row_i = lax.broadcasted_iota(jnp.int32, (tq, tq), 0)
        col_i = lax.broadcasted_iota(jnp.int32, (tq, tq), 1)
        for j in range(ATT_B_NKB):
            back = (ATT_B_NKB - 1 - j) * tq
            if consecutive:
                rel = row_i - (col_i - back)
            else:
                rel = posq_ref[...] - posk_ref[jnp.maximum(i - (ATT_B_NKB - 1) + j, 0)]
            r = jnp.clip(rel, -B_REL_CLIP, B_REL_CLIP) + B_REL_CLIP
            dchunk = (row_i >> CHUNK_SHIFT) - ((col_i - back) >> CHUNK_SHIFT)
            band = (dchunk >= 0) & (dchunk <= B_PREV_CHUNKS)
            for hl in range(ATT_B_LG * nh):
                h = gp * (ATT_B_LG * nh) + hl
                row = relt_ref[pl.ds(h, 1), :] * LOG2E
                seg0 = jnp.broadcast_to(row[:, :LANE], (tq, LANE))
                seg1 = jnp.broadcast_to(row[:, LANE:2 * LANE], (tq, LANE))
                t256 = t256_ref[h] * LOG2E
                for c in range(tq // LANE):
                    cs = slice(c * LANE, (c + 1) * LANE)
                    rc = r[:, cs]
                    lo = rc & (LANE - 1)
                    bias = jnp.where(rc < LANE, _lane_gather(seg0, lo),
                                     jnp.where(rc < 2 * LANE, _lane_gather(seg1, lo), t256))
                    bias_ref[hl, :, j * tq + c * LANE:j * tq + (c + 1) * LANE] = jnp.where(band[:, cs], bias, -jnp.inf)

    if consecutive:
        pl.when(i == 0)(build_bias)
    else:
        build_bias()

    lane_head = lax.broadcasted_iota(jnp.int32, (1, LANE), 1) >> CHUNK_SHIFT
    for lg in range(ATT_B_LG):
        lanes = slice(lg * LANE, (lg + 1) * LANE)
        qp = q_ref[:, lanes]
        out = jnp.zeros((tq, LANE), F32)
        for hh in range(nh):
            mine = lane_head == hh
            qm = jnp.where(mine, qp, jnp.zeros_like(qp))
            parts = []
            for j in range(ATT_B_NKB):
                sj = lax.dot_general(qm, k_refs[j][:, lanes], NT_DIMS, preferred_element_type=F32)
                sj = sj + bias_ref[lg * nh + hh, :, j * tq:(j + 1) * tq]
                if j < ATT_B_NKB - 1:
                    sj = sj + jnp.where(i - (ATT_B_NKB - 1) + j >= 0, 0.0, -jnp.inf)
                parts.append(sj)
            s = jnp.concatenate(parts, axis=1)
            p = jnp.exp2(s - jnp.max(s, axis=1, keepdims=True)).astype(BF16)
            acc = jnp.zeros((tq, LANE), F32)
            for j in range(ATT_B_NKB):
                vj = v_refs[j][:, lanes]
                vm = jnp.where(mine, vj, jnp.ones_like(vj))
                acc = acc + jnp.dot(p[:, j * tq:(j + 1) * tq], vm, preferred_element_type=F32)
            rowsum = pltpu.roll(acc, B_HEAD_DIM, 1)
            out = out + jnp.where(mine, acc / rowsum, 0.0)
        o_ref[:, lanes] = out.astype(o_ref.dtype)


def band_attention(qkv, pos, rel_table, *, consecutive):
    s = qkv.shape[0]
    tq = ATT_B_TQ
    hw = B_HEADS * B_HEAD_DIM
    bw = ATT_B_LG * LANE
    ngrp = hw // bw
    assert LANE // B_HEAD_DIM == 2
    nrel = 2 * B_REL_CLIP + 1
    relt = jnp.zeros((B_HEADS, 3 * LANE), F32).at[:, :nrel].set(rel_table.T)
    t256 = rel_table[nrel - 1]

    def kv_spec(j, base):
        return pl.BlockSpec((tq, bw), lambda gp, i: (jnp.maximum(i - (ATT_B_NKB - 1) + j, 0), base + gp))

    return pl.pallas_call(
        functools.partial(_attn_b_kernel, consecutive=consecutive),
        name="band_attention",
        grid=(ngrp, s // tq),
        in_specs=[
            pl.BlockSpec(memory_space=pltpu.SMEM),
            pl.BlockSpec((tq, bw), lambda gp, i: (i, gp)),
            kv_spec(0, ngrp), kv_spec(1, ngrp), kv_spec(2, ngrp),
            kv_spec(0, 2 * ngrp), kv_spec(1, 2 * ngrp), kv_spec(2, 2 * ngrp),
            pl.BlockSpec((tq, 1), lambda gp, i: (i, 0)),
            pl.BlockSpec((s // tq, 1, tq), lambda gp, i: (0, 0, 0)),
            pl.BlockSpec((B_HEADS, 3 * LANE), lambda gp, i: (0, 0)),
        ],
        out_specs=pl.BlockSpec((tq, bw), lambda gp, i: (i, gp)),
        out_shape=jax.ShapeDtypeStruct((s, hw), BF16),
        scratch_shapes=[pltpu.VMEM((ATT_B_LG * (LANE // B_HEAD_DIM), tq, ATT_B_NKB * tq), F32)],
        compiler_params=_params(("parallel", "arbitrary")),
    )(t256, qkv, qkv, qkv, qkv, qkv, qkv, qkv, pos.reshape(s, 1), pos.reshape(s // tq, 1, tq), relt)


ATT_C_T = 512
ATT_C_HEADS = 4


def _attn_c_kernel(q_ref, kn_ref, kr_ref, v_ref, o_ref, m_ref, l_ref, acc_ref):
    t, nh = ATT_C_T, ATT_C_HEADS
    iq = pl.program_id(1)
    m_ref[...] = jnp.full(m_ref.shape, NEG_BIG, F32)
    l_ref[...] = jnp.zeros(l_ref.shape, F32)
    acc_ref[...] = jnp.zeros(acc_ref.shape, F32)

    def tile(kt, diag):
        k0 = pl.multiple_of(kt * t, t)
        kr_t = kr_ref[pl.ds(k0, t), :]
        for hh in range(nh):
            kcat = jnp.concatenate([kn_ref[pl.ds(k0, t), hh * C_NOPE:(hh + 1) * C_NOPE], kr_t], axis=1)
            s = lax.dot_general(q_ref[:, hh * 2 * LANE:(hh + 1) * 2 * LANE], kcat, NT_DIMS,
                                preferred_element_type=F32)
            if diag:
                qc = lax.broadcasted_iota(jnp.int32, (t, t), 0) >> CHUNK_SHIFT
                kc = lax.broadcasted_iota(jnp.int32, (t, t), 1) >> CHUNK_SHIFT
                s = jnp.where(kc <= qc, s, -jnp.inf)
            _softmax_step(s, v_ref[pl.ds(k0, t), hh * C_V:(hh + 1) * C_V], m_ref, l_ref, acc_ref,
                          slice(hh * t, (hh + 1) * t), t)

    _for_tiles_in_pairs(0, iq, lambda kt: tile(kt, False), per_trip=(4, 2, 1))
    tile(iq, True)
    for hh in range(nh):
        rows = slice(hh * t, (hh + 1) * t)
        o_ref[:, hh * C_V:(hh + 1) * C_V] = (acc_ref[rows, :] / l_ref[rows, :]).astype(o_ref.dtype)


def mla_attention(qcat, kv, kr):
    s = qcat.shape[0]
    t, nh = ATT_C_T, ATT_C_HEADS
    assert s % t == 0
    return pl.pallas_call(
        _attn_c_kernel,
        name="mla_attention",
        grid=(C_HEADS // nh, s // t),
        in_specs=[
            pl.BlockSpec((t, nh * 2 * LANE), lambda h, i: (i, h)),
            pl.BlockSpec((s, nh * C_NOPE), lambda h, i: (0, h)),
            pl.BlockSpec((s, LANE), lambda h, i: (0, 0)),
            pl.BlockSpec((s, nh * C_V), lambda h, i: (0, C_HEADS // nh + h)),
        ],
        out_specs=pl.BlockSpec((t, nh * C_V), lambda h, i: (i, h)),
        out_shape=jax.ShapeDtypeStruct((s, C_HEADS * C_V), BF16),
        scratch_shapes=[pltpu.VMEM((nh * t, LANE), F32), pltpu.VMEM((nh * t, LANE), F32),
                        pltpu.VMEM((nh * t, C_V), F32)],
        compiler_params=_params(("parallel", "arbitrary")),
    )(qcat, kv, kr, kv)


def _rope_tables(pos):
    half = ROPE_DIM // 2
    inv = ROPE_BASE ** (-jnp.arange(half, dtype=F32) * 2.0 / ROPE_DIM)
    ang = pos.astype(F32)[:, None] * inv
    cos, sin = jnp.cos(ang), jnp.sin(ang)
    s = pos.shape[0]
    z = jnp.zeros((s, half), F32)
    c = jnp.concatenate([cos, cos, jnp.ones((s, LANE - ROPE_DIM), F32)], axis=1)
    s1 = jnp.concatenate([-sin, z, z, z], axis=1)
    s2 = jnp.concatenate([z, sin, z, z], axis=1)
    return c, s1, s2


def _query_scale(n_query, n_total, scale):
    return jnp.concatenate([jnp.full((n_query,), scale * LOG2E, F32), jnp.ones((n_total - n_query,), F32)])


def _by_position_layout(consecutive, fn, *args):
    return lax.cond(consecutive, functools.partial(fn, consecutive=True),
                    functools.partial(fn, consecutive=False), *args)


def _mixer_a(h, gain, pos, consecutive, tables, w_in, w_out, layer, t5_table):
    s = h.shape[0]
    d = h.shape[1]
    nq = A_HEADS * A_HEAD_DIM
    nkv = A_KV_HEADS * A_HEAD_DIM
    nqi = IDX_HEADS * IDX_DIM
    o_qkv = nq + 2 * nkv
    o_ki = o_qkv + nqi
    tn = 1024
    assert o_qkv % tn == 0 and nqi % tn == 0
    qkv = norm_mm(h, 0, d, gain, w_in, layer=layer, n=o_qkv, tn=tn, out_dtype=BF16, name="a_qkv_proj",
                  col_scale=_query_scale(nq, o_qkv, A_HEAD_DIM ** -0.5))
    qi = norm_mm(h, 0, d, gain, w_in, layer=layer, n=nqi, col0=o_qkv // tn, tn=tn, out_dtype=BF16,
                 rope=(True,) * (tn // LANE), tables=tables, name="a_idxq_proj")
    w_kw = jnp.zeros((d, 2 * LANE), BF16).at[:, :IDX_DIM + IDX_HEADS].set(w_in[layer, :, o_ki:].astype(BF16))
    kiwi = norm_mm(h, 0, d, gain, w_kw, tn=2 * LANE, out_dtype=F32, rope=(True, False), tables=tables,
                   name="a_idxk_proj")
    mask = dsa_indexer(qi, kiwi, topk=min(IDX_TOPK_MAX, s // 4))
    o = _by_position_layout(consecutive, dsa_attention, qkv, mask, pos, t5_table)
    return mm_residual(o, w_out, h, layer=layer, tn=w_out.shape[-1])


def _mixer_b(h, gain, pos, consecutive, w_in, rel_table, w_out, layer):
    d = h.shape[1]
    n = w_in.shape[-1]
    qkv = norm_mm(h, 0, d, gain, w_in, layer=layer, tn=1024, out_dtype=BF16, name="b_qkv_proj",
                  col_scale=_query_scale(n // 3, n, B_HEAD_DIM ** -0.5))
    o = _by_position_layout(consecutive, band_attention, qkv, pos, rel_table)
    return mm_residual(o, w_out, h, layer=layer, tn=w_out.shape[-1])


def _mixer_c(h, gain, tables, w_down, g_q, g_kv, w_uq, w_ukv, w_out, layer):
    d = h.shape[1]
    lq = g_q.shape[0]
    lkv = g_kv.shape[0]
    n_down = lq + lkv + LANE
    wd = jnp.zeros((d, n_down), BF16).at[:, :w_down.shape[1]].set(w_down.astype(BF16))
    down = norm_mm(h, 0, d, gain, wd, tn=n_down, out_dtype=F32,
                   rope=(False,) * ((lq + lkv) // LANE) + (True,), tables=tables, name="c_down_proj")
    wq = w_uq.astype(BF16).reshape(lq, C_HEADS, C_NOPE + C_ROPE)
    wq = jnp.pad(wq, ((0, 0), (0, 0), (0, 2 * LANE - C_NOPE - C_ROPE))).reshape(lq, C_HEADS * 2 * LANE)
    nqc = wq.shape[1]
    qcat = norm_mm(down, 0, lq, g_q, wq, tn=nqc, out_dtype=BF16, rope=(False, True) * C_HEADS, tables=tables,
                   name="c_uq_proj", col_scale=_query_scale(nqc, nqc, (C_NOPE + C_ROPE) ** -0.5))
    wkv = w_ukv.astype(BF16).reshape(lkv, C_HEADS, 2, C_NOPE).transpose(0, 2, 1, 3).reshape(lkv, -1)
    assert lq == lkv
    kv = norm_mm(down, 1, lkv, g_kv, wkv, tn=wkv.shape[1], out_dtype=BF16, name="c_ukv_proj")
    kr = down[:, lq + lkv:].astype(BF16)
    o = mla_attention(qcat, kv, kr)
    return mm_residual(o, w_out, h, layer=layer, tn=w_out.shape[-1])


def kernel(x, p, positions, t5_table, a_w_in, a_w_out, b_w_in, b_rel_table, b_w_out, c_w_down, c_q_norm,
           c_kv_norm, c_w_uq, c_w_ukv, c_w_out, attn_norm, ffn_norm, ffn_w_in, ffn_w_out, ple_norm,
           ple_w_gate, ple_w_proj, final_norm):
    assert x.shape[0] == 1
    depth = attn_norm.shape[0]
    h = x[0]
    pos = positions[0]
    tables = _rope_tables(pos)
    consecutive = jnp.all(pos[1:] - pos[:-1] == 1)
    a_w_in, a_w_out, b_w_in, b_w_out, c_w_out, ffn_w_in, ffn_w_out, ple_w_gate, ple_w_proj = (
        w.astype(BF16) for w in (a_w_in, a_w_out, b_w_in, b_w_out, c_w_out, ffn_w_in, ffn_w_out, ple_w_gate,
                                 ple_w_proj))
    for i in range(depth):
        j, kind = divmod(i, 3)
        if kind == 0:
            h = _mixer_a(h, attn_norm[i], pos, consecutive, tables, a_w_in, a_w_out, j, t5_table)
        elif kind == 1:
            h = _mixer_b(h, attn_norm[i], pos, consecutive, b_w_in, b_rel_table[j], b_w_out, j)
        else:
            h = _mixer_c(h, attn_norm[i], tables, c_w_down[j], c_q_norm[j], c_kv_norm[j], c_w_uq[j],
                         c_w_ukv[j], c_w_out, j)
        act = ffn_in(h, ffn_norm[i], ffn_w_in, layer=i)
        h = mm_residual(act, ffn_w_out, h, layer=i, tn=1024)
        h = ple(h, ple_norm[i], ple_w_gate, p, ple_w_proj, layer=i)
    return final_rms_norm(h, final_norm)[None]
```

```python
import functools
import math

import jax
import jax.numpy as jnp
from jax import lax
from jax.experimental import pallas as pl
from jax.experimental.pallas import tpu as pltpu

LANE = 128
VMEM_LIMIT_BYTES = 56 * 1024 * 1024

CHUNK = 64
CHUNK_SHIFT = CHUNK.bit_length() - 1
EPS = 1e-6
ROPE_BASE = 10000.0
ROPE_DIM = 64
A_HEADS = 16
A_KV_HEADS = 4
A_GROUP = 4
A_HEAD_DIM = 128
IDX_HEADS = 16
IDX_DIM = 128
IDX_TOPK_MAX = 256
T5_BUCKETS = 32
T5_MAX_DISTANCE = 1024
_T5_EXACT = T5_BUCKETS // 4
_T5_LAST = _T5_EXACT * (T5_MAX_DISTANCE / _T5_EXACT) ** ((_T5_EXACT - 1) / _T5_EXACT)
T5_FAR = math.ceil(_T5_LAST / LANE) * LANE
assert T5_FAR == 640
B_HEADS = 32
B_HEAD_DIM = 64
B_PREV_CHUNKS = 8
B_REL_CLIP = 128
C_HEADS = 16
C_NOPE = 128
C_ROPE = 64
C_V = 128
NEG_BIG = -1e30
INT_MIN = -(2 ** 31)
LOG2E = math.log2(math.e)

F32 = jnp.float32
BF16 = jnp.bfloat16
NT_DIMS = (((1,), (1,)), ((), ()))


def _params(sem):
    return pltpu.CompilerParams(dimension_semantics=sem, vmem_limit_bytes=VMEM_LIMIT_BYTES)


def _rms(x, g):
    ms = jnp.mean(x * x, axis=-1, keepdims=True)
    return (x * lax.rsqrt(ms + EPS)) * g


def _lane_gather(table, idx):
    return jnp.take_along_axis(table, idx, axis=1, mode="promise_in_bounds")


def _rope_group(y, c, s1, s2):
    return y * c + pltpu.roll(y, 96, 1) * s1 + pltpu.roll(y, 32, 1) * s2


def _norm_mm_kernel(*refs, rope, scaled):
    x_ref, g_ref, w_ref = refs[:3]
    rest = list(refs[3:])
    cs_ref = rest.pop(0) if scaled else None
    c_ref, s1_ref, s2_ref = (rest.pop(0), rest.pop(0), rest.pop(0)) if rope is not None else (None,) * 3
    o_ref, xn_ref = rest

    @pl.when(pl.program_id(1) == 0)
    def _():
        xn_ref[...] = _rms(x_ref[...], g_ref[...]).astype(BF16)

    y = jnp.dot(xn_ref[...], w_ref[...], preferred_element_type=F32)
    if scaled:
        y = y * cs_ref[...]
    if rope is None:
        o_ref[...] = y.astype(o_ref.dtype)
    else:
        c, s1, s2 = c_ref[...], s1_ref[...], s2_ref[...]
        for gi, on in enumerate(rope):
            sl = slice(gi * LANE, (gi + 1) * LANE)
            yg = y[:, sl]
            if on:
                yg = _rope_group(yg, c, s1, s2)
            o_ref[:, sl] = yg.astype(o_ref.dtype)


def _w_spec(w, layer, k, tn, col0=0):
    if w.ndim == 2:
        return pl.BlockSpec((k, tn), lambda i, j: (0, col0 + j))
    return pl.BlockSpec((None, k, tn), lambda i, j: (layer, 0, col0 + j))


def norm_mm(x, x_col, kx, gain, w, *, tn, out_dtype, name, n=None, layer=None, col0=0, tm=512, rope=None,
            tables=None, col_scale=None):
    s = x.shape[0]
    n = w.shape[-1] if n is None else n
    tm = min(tm, s)
    in_specs = [
        pl.BlockSpec((tm, kx), lambda i, j: (i, x_col)),
        pl.BlockSpec((1, kx), lambda i, j: (0, 0)),
        _w_spec(w, layer, kx, tn, col0),
    ]
    args = [x, gain.reshape(1, kx), w]
    if col_scale is not None:
        in_specs.append(pl.BlockSpec((1, tn), lambda i, j: (0, j)))
        args.append(col_scale.reshape(1, n))
    if rope is not None:
        assert len(rope) == tn // LANE
        in_specs += [pl.BlockSpec((tm, LANE), lambda i, j: (i, 0))] * 3
        args += list(tables)
    return pl.pallas_call(
        functools.partial(_norm_mm_kernel, rope=rope, scaled=col_scale is not None),
        grid=(s // tm, n // tn),
        in_specs=in_specs,
        out_specs=pl.BlockSpec((tm, tn), lambda i, j: (i, j)),
        out_shape=jax.ShapeDtypeStruct((s, n), out_dtype),
        scratch_shapes=[pltpu.VMEM((tm, kx), BF16)],
        compiler_params=_params(("parallel", "arbitrary")),
        name=name,
    )(*args)


def _mm_res_kernel(x_ref, w_ref, r_ref, o_ref):
    o_ref[...] = r_ref[...] + jnp.dot(x_ref[...], w_ref[...], preferred_element_type=F32)


def mm_residual(x, w, res, *, layer=None, tm=512, tn=512):
    s, k = x.shape
    n = w.shape[-1]
    tm = min(tm, s)
    return pl.pallas_call(
        _mm_res_kernel,
        name="mm_residual",
        grid=(s // tm, n // tn),
        in_specs=[
            pl.BlockSpec((tm, k), lambda i, j: (i, 0)),
            _w_spec(w, layer, k, tn),
            pl.BlockSpec((tm, tn), lambda i, j: (i, j)),
        ],
        out_specs=pl.BlockSpec((tm, tn), lambda i, j: (i, j)),
        out_shape=jax.ShapeDtypeStruct((s, n), F32),
        compiler_params=_params(("parallel", "arbitrary")),
    )(x, w, res)


def _ffn_in_kernel(x_ref, g_ref, wg_ref, wu_ref, o_ref, xn_ref):
    @pl.when(pl.program_id(1) == 0)
    def _():
        xn_ref[...] = _rms(x_ref[...], g_ref[...]).astype(BF16)

    xn = xn_ref[...]
    a = jnp.dot(xn, wg_ref[...], preferred_element_type=F32)
    u = jnp.dot(xn, wu_ref[...], preferred_element_type=F32)
    o_ref[...] = (a * jax.nn.sigmoid(a) * u).astype(o_ref.dtype)


def ffn_in(h, gain, w_in, *, layer=None, tm=1024, tn=512):
    s, d = h.shape
    f = w_in.shape[-1] // 2
    tm = min(tm, s)
    nj = f // tn
    return pl.pallas_call(
        _ffn_in_kernel,
        name="ffn_in",
        grid=(s // tm, nj),
        in_specs=[
            pl.BlockSpec((tm, d), lambda i, j: (i, 0)),
            pl.BlockSpec((1, d), lambda i, j: (0, 0)),
            _w_spec(w_in, layer, d, tn),
            _w_spec(w_in, layer, d, tn, nj),
        ],
        out_specs=pl.BlockSpec((tm, tn), lambda i, j: (i, j)),
        out_shape=jax.ShapeDtypeStruct((s, f), BF16),
        scratch_shapes=[pltpu.VMEM((tm, d), BF16)],
        compiler_params=_params(("parallel", "arbitrary")),
    )(h, gain.reshape(1, d), w_in, w_in)


def _ple_kernel(x_ref, g_ref, wg_ref, p_ref, wp_ref, h_ref, o_ref, xn_ref, pb_ref):
    @pl.when(pl.program_id(1) == 0)
    def _():
        xn_ref[...] = _rms(x_ref[...], g_ref[...]).astype(BF16)
        pb_ref[...] = p_ref[...].astype(BF16)

    gate = jax.nn.sigmoid(jnp.dot(xn_ref[...], wg_ref[...], preferred_element_type=F32))
    proj = jnp.dot(pb_ref[...], wp_ref[...], preferred_element_type=F32)
    o_ref[...] = h_ref[...] + gate * proj


def ple(h, gain, w_gate, p, w_proj, *, layer, tm=512, tn=2048):
    s, d = h.shape
    pd = p.shape[-1]
    tm = min(tm, s)
    return pl.pallas_call(
        _ple_kernel,
        name="ple",
        grid=(s // tm, d // tn),
        in_specs=[
            pl.BlockSpec((tm, d), lambda i, j: (i, 0)),
            pl.BlockSpec((1, d), lambda i, j: (0, 0)),
            _w_spec(w_gate, layer, d, tn),
            pl.BlockSpec((None, None, tm, pd), lambda i, j: (layer, 0, i, 0)),
            _w_spec(w_proj, layer, pd, tn),
            pl.BlockSpec((tm, tn), lambda i, j: (i, j)),
        ],
        out_specs=pl.BlockSpec((tm, tn), lambda i, j: (i, j)),
        out_shape=jax.ShapeDtypeStruct((s, d), F32),
        scratch_shapes=[pltpu.VMEM((tm, d), BF16), pltpu.VMEM((tm, pd), BF16)],
        compiler_params=_params(("parallel", "arbitrary")),
    )(h, gain.reshape(1, d), w_gate, p, w_proj, h)


def _final_norm_kernel(x_ref, g_ref, o_ref):
    o_ref[...] = _rms(x_ref[...], g_ref[...])


def final_rms_norm(h, gain, *, tm=512):
    s, d = h.shape
    tm = min(tm, s)
    return pl.pallas_call(
        _final_norm_kernel,
        name="final_norm",
        grid=(s // tm,),
        in_specs=[pl.BlockSpec((tm, d), lambda i: (i, 0)), pl.BlockSpec((1, d), lambda i: (0, 0))],
        out_specs=pl.BlockSpec((tm, d), lambda i: (i, 0)),
        out_shape=jax.ShapeDtypeStruct((s, d), F32),
        compiler_params=_params(("parallel",)),
    )(h, gain.reshape(1, d))


IDX_TQ = 256
IDX_KC = 256


def _orderable(x):
    b = pltpu.bitcast(x, jnp.int32)
    return jnp.where(b < 0, b ^ jnp.int32(0x7FFFFFFF), b)


def _indexer_kernel(qi_ref, ki_ref, wi_ref, mask_ref, keys_ref, *, topk, n_kblocks):
    tq, kc = IDX_TQ, IDX_KC
    i = pl.program_id(0)
    nkc = (i + 1) * (tq // kc)
    n_idx_bits = (n_kblocks * kc - 1).bit_length()
    w_t = (wi_ref[...] * (IDX_HEADS ** -0.5 * IDX_DIM ** -0.5)).T
    qchunk = (i * tq + lax.broadcasted_iota(jnp.int32, (kc, tq), 1)) >> CHUNK_SHIFT
    krow = lax.broadcasted_iota(jnp.int32, (kc, tq), 0)

    def valid_of(c):
        return ((c * kc + krow) >> CHUNK_SHIFT) <= qchunk

    def score_chunk(c):
        k0 = pl.multiple_of(c * kc, kc)
        kblk = ki_ref[pl.ds(k0, kc), :].astype(BF16)
        acc = jnp.zeros((kc, tq), F32)
        for h in range(IDX_HEADS):
            s = lax.dot_general(kblk, qi_ref[:, h * IDX_DIM:(h + 1) * IDX_DIM], NT_DIMS,
                                preferred_element_type=F32)
            acc = acc + jnp.maximum(s, 0.0) * w_t[h:h + 1, :]
        keys_ref[pl.ds(k0, kc), :] = jnp.where(valid_of(c), _orderable(acc), jnp.int32(INT_MIN))

    _for_tiles_in_pairs(0, nkc, score_chunk, per_trip=(4, 2, 1))

    @pl.when(nkc % 2 == 1)
    def _():
        keys_ref[pl.ds(pl.multiple_of(nkc * kc, kc), kc), :] = jnp.full((kc, tq), INT_MIN, jnp.int32)

    def count(pred_of):
        nacc = 4

        def body(cp, cnt):
            for c in (2 * cp, 2 * cp + 1):
                k0 = pl.multiple_of(c * kc, kc)
                hit = pred_of(c, keys_ref[pl.ds(k0, kc), :]).astype(jnp.int32)
                cnt = cnt + hit.reshape(kc // (8 * nacc), nacc, 8, tq).sum(axis=0)
            return cnt
        cnt = lax.fori_loop(0, (nkc + 1) // 2, body, jnp.zeros((nacc, 8, tq), jnp.int32))
        return cnt.sum(axis=0).sum(axis=0, keepdims=True)

    def bit_step(b, state):
        thr, cnt = state
        cand = thr + lax.shift_left(jnp.int32(1), 31 - b)
        c_cand = count(lambda c, keys: keys >= cand)
        take = c_cand >= topk
        return jnp.where(take, cand, thr), jnp.where(take, c_cand, cnt)

    thr, cnt = lax.fori_loop(
        0, 32, bit_step,
        (jnp.full((1, tq), INT_MIN, jnp.int32), jnp.full((1, tq), nkc * kc, jnp.int32)))

    def tie_limit():
        need = topk - count(lambda c, keys: keys > thr)
        def bit_body(b, x):
            cand = x + lax.shift_left(jnp.int32(1), n_idx_bits - 1 - b)
            below = count(lambda c, keys: (keys == thr) & ((c * kc + krow) < cand))
            return jnp.where(below < need, cand, x)
        return lax.fori_loop(0, n_idx_bits, bit_body, jnp.zeros((1, tq), jnp.int32))

    def write_masks(selected):
        def mask_body(c, carry):
            k0 = pl.multiple_of(c * kc, kc)
            sel = selected(c, keys_ref[pl.ds(k0, kc), :]) & valid_of(c)
            mask_ref[c] = jnp.where(sel, 0.0, -jnp.inf).astype(F32).T
            return carry
        lax.fori_loop(0, nkc, mask_body, 0)

    has_ties = jnp.max(jnp.where((cnt > topk) & (thr > INT_MIN), 1, 0)) > 0

    @pl.when(has_ties)
    def _():
        tie_last = tie_limit()
        write_masks(lambda c, keys: (keys > thr) | ((keys == thr) & ((c * kc + krow) <= tie_last)))

    @pl.when(jnp.logical_not(has_ties))
    def _():
        write_masks(lambda c, keys: keys >= thr)

    def fill_body(c, carry):
        mask_ref[c] = jnp.full((tq, kc), -jnp.inf, F32)
        return carry

    lax.fori_loop(nkc, n_kblocks, fill_body, 0)


def dsa_indexer(qi, kiwi, *, topk):
    s = qi.shape[0]
    tq, kc = IDX_TQ, IDX_KC
    nkb = s // kc
    assert s % tq == 0 and nkb % 2 == 0
    return pl.pallas_call(
        functools.partial(_indexer_kernel, topk=topk, n_kblocks=nkb),
        name="dsa_indexer",
        grid=(s // tq,),
        in_specs=[
            pl.BlockSpec((tq, IDX_HEADS * IDX_DIM), lambda i: (i, 0)),
            pl.BlockSpec((s, IDX_DIM), lambda i: (0, 0)),
            pl.BlockSpec((tq, LANE), lambda i: (i, 1)),
        ],
        out_specs=pl.BlockSpec((nkb, tq, kc), lambda i: (0, i, 0)),
        out_shape=jax.ShapeDtypeStruct((nkb, s, kc), F32),
        scratch_shapes=[pltpu.VMEM((s, tq), jnp.int32)],
        compiler_params=_params(("parallel",)),
    )(qi, kiwi, kiwi)


ATT_A_TQ = 256
ATT_A_KB = 2
ATT_A_NEAR = (T5_FAR + ATT_A_KB * IDX_KC - 2) // IDX_KC + 1


def _t5_bucket(rel):
    nb = T5_BUCKETS // 2
    max_exact = nb // 2
    ret = jnp.where(rel > 0, nb, 0)
    n = jnp.abs(rel)
    nf = jnp.maximum(n, 1).astype(F32)
    large = max_exact + (jnp.log(nf / max_exact) / math.log(T5_MAX_DISTANCE / max_exact)
                         * (nb - max_exact)).astype(jnp.int32)
    large = jnp.minimum(large, nb - 1)
    return ret + jnp.where(n < max_exact, n, large)


def _softmax_step(s, v, m_ref, l_ref, acc_ref, rows, tk, shift=None):
    d = v.shape[1]
    m_prev = m_ref[rows, :]
    mx = jnp.max(s, axis=1, keepdims=True)
    if shift is not None:
        mx = mx + shift
    m_next = jnp.maximum(m_prev, mx)
    pivot = m_next if shift is None else m_next - shift
    p = jnp.exp2(s - jnp.concatenate([pivot] * (tk // LANE), axis=1)).astype(BF16)
    alpha = jnp.exp2(m_prev - m_next)
    v_ones = jnp.concatenate([v, jnp.ones((tk, LANE), BF16)], axis=1)
    pv = jnp.dot(p, v_ones, preferred_element_type=F32)
    m_ref[rows, :] = m_next
    l_ref[rows, :] = alpha * l_ref[rows, :] + pv[:, d:]
    acc_ref[rows, :] = acc_ref[rows, :] * alpha + pv[:, :d]


def _for_tiles_in_pairs(lo, hi, tile_fn, per_trip=(2, 1)):
    assert per_trip[-1] == 1
    start = lo
    for size in per_trip:
        def group(kp, carry, size=size, start=start):
            for u in range(size):
                tile_fn(start + size * kp + u)
            return carry

        n = (hi - start) // size
        lax.fori_loop(0, n, group, 0)
        start = start + n * size


def _attn_a_kernel(qmin_ref, kmax_ref, t5s_ref, q_ref, k_ref, v_ref, mask_ref, posq_ref, posk_ref,
                   t5t_ref, o_ref, m_ref, l_ref, acc_ref, nbias_ref, *, consecutive):
    tq, kc, nb = ATT_A_TQ, IDX_KC, ATT_A_KB
    tk = nb * kc
    g = pl.program_id(0)
    i = pl.program_id(1)

    def head_bias(r, bucket):
        row = t5t_ref[pl.ds(g * A_GROUP + r, 1), :] * LOG2E
        tbl = jnp.broadcast_to(row, (tq, LANE))
        return jnp.concatenate([_lane_gather(tbl, bucket[:, c * LANE:(c + 1) * LANE])
                                for c in range(tk // LANE)], axis=1)

    if consecutive:
        @pl.when(i == 0)
        def _():
            rel0 = (lax.broadcasted_iota(jnp.int32, (tq, tk), 1)
                    - lax.broadcasted_iota(jnp.int32, (tq, tk), 0))
            for d in range(ATT_A_NEAR):
                bucket = _t5_bucket(rel0 + (d - ATT_A_NEAR + 1) * kc)
                for r in range(A_GROUP):
                    nbias_ref[d, r] = head_bias(r, bucket)

    m_ref[...] = jnp.full(m_ref.shape, NEG_BIG, F32)
    l_ref[...] = jnp.zeros(l_ref.shape, F32)
    acc_ref[...] = jnp.zeros(acc_ref.shape, F32)
    nkt = ((i + 1) * (tq // kc) + nb - 1) // nb
    qmin = qmin_ref[i]

    def is_far(kt):
        ktc = jnp.minimum(kt, nkt - 1) * nb
        kmax = kmax_ref[ktc]
        for b in range(1, nb):
            kmax = jnp.maximum(kmax, kmax_ref[ktc + b])
        return (kt < nkt) & ((qmin - kmax) >= T5_FAR)

    n_far = lax.while_loop(is_far, lambda kt: kt + 1, jnp.int32(0))

    def tile(kt, bias_of, shift_of):
        k0 = pl.multiple_of(kt * tk, tk)
        k_t = k_ref[pl.ds(k0, tk), :]
        v_t = v_ref[pl.ds(k0, tk), :]
        base = jnp.concatenate([mask_ref[kt * nb + b] for b in range(nb)], axis=1)
        for r in range(A_GROUP):
            rows = slice(r * tq, (r + 1) * tq)
            q_r = q_ref[:, r * A_HEAD_DIM:(r + 1) * A_HEAD_DIM]
            s = lax.dot_general(q_r, k_t, NT_DIMS, preferred_element_type=F32) + bias_of(r, base)
            _softmax_step(s, v_t, m_ref, l_ref, acc_ref, rows, tk, shift_of(r))

    def far_tile(kt):
        tile(kt, lambda r, base: base,
             lambda r: t5s_ref[T5_BUCKETS // 2 - 1, g * A_GROUP + r] * LOG2E)

    def near_tile(kt):
        if consecutive:
            d = kt * nb - i * (tq // kc) + (ATT_A_NEAR - 1)
            tile(kt, lambda r, base: base + nbias_ref[d, r], lambda r: None)
        else:
            pk = jnp.concatenate([posk_ref[kt * nb + b] for b in range(nb)], axis=1)
            bucket = _t5_bucket(pk - posq_ref[...])
            tile(kt, lambda r, base: base + head_bias(r, bucket), lambda r: None)

    _for_tiles_in_pairs(0, n_far, far_tile, per_trip=(4, 2, 1))
    _for_tiles_in_pairs(n_far, nkt, near_tile)
    for r in range(A_GROUP):
        rows = slice(r * tq, (r + 1) * tq)
        o_ref[:, r * A_HEAD_DIM:(r + 1) * A_HEAD_DIM] = (acc_ref[rows, :] / l_ref[rows, :]).astype(o_ref.dtype)


def dsa_attention(qkv, mask, pos, t5_table, *, consecutive):
    s = qkv.shape[0]
    tq, tk = ATT_A_TQ, IDX_KC
    assert (s // tk) % ATT_A_KB == 0
    gw = A_GROUP * A_HEAD_DIM
    qmin = pos.reshape(s // tq, tq).min(axis=1)
    kmax = pos.reshape(s // tk, tk).max(axis=1)
    t5t = jnp.zeros((A_HEADS, LANE), F32).at[:, :T5_BUCKETS].set(t5_table.T)
    kblk0 = (A_HEADS * A_HEAD_DIM) // A_HEAD_DIM
    smem = pl.BlockSpec(memory_space=pltpu.SMEM)
    nbias_shape = (ATT_A_NEAR, A_GROUP, tq, ATT_A_KB * tk) if consecutive else (1, 1, 8, LANE)
    return pl.pallas_call(
        functools.partial(_attn_a_kernel, consecutive=consecutive),
        name="dsa_attention",
        grid=(A_KV_HEADS, s // tq),
        in_specs=[
            smem, smem, smem,
            pl.BlockSpec((tq, gw), lambda g, i: (i, g)),
            pl.BlockSpec((s, A_HEAD_DIM), lambda g, i: (0, kblk0 + g)),
            pl.BlockSpec((s, A_HEAD_DIM), lambda g, i: (0, kblk0 + A_KV_HEADS + g)),
            pl.BlockSpec((s // tk, tq, tk), lambda g, i: (0, i, 0)),
            pl.BlockSpec((tq, 1), lambda g, i: (i, 0)),
            pl.BlockSpec((s // tk, 1, tk), lambda g, i: (0, 0, 0)),
            pl.BlockSpec((A_HEADS, LANE), lambda g, i: (0, 0)),
        ],
        out_specs=pl.BlockSpec((tq, gw), lambda g, i: (i, g)),
        out_shape=jax.ShapeDtypeStruct((s, A_HEADS * A_HEAD_DIM), BF16),
        scratch_shapes=[
            pltpu.VMEM((A_GROUP * tq, LANE), F32),
            pltpu.VMEM((A_GROUP * tq, LANE), F32),
            pltpu.VMEM((A_GROUP * tq, A_HEAD_DIM), F32),
            pltpu.VMEM(nbias_shape, F32),
        ],
        compiler_params=_params(("parallel", "arbitrary")),
    )(qmin, kmax, t5_table, qkv, qkv, qkv, mask, pos.reshape(s, 1), pos.reshape(s // tk, 1, tk), t5t)


ATT_B_TQ = 256
ATT_B_NKB = 3
ATT_B_LG = 4


def _attn_b_kernel(t256_ref, q_ref, k0_ref, k1_ref, k2_ref, v0_ref, v1_ref, v2_ref, posq_ref, posk_ref,
                   relt_ref, o_ref, bias_ref, *, consecutive):
    tq = ATT_B_TQ
    nh = LANE // B_HEAD_DIM
    gp = pl.program_id(0)
    i = pl.program_id(1)
    k_refs = (k0_ref, k1_ref, k2_ref)
    v_refs = (v0_ref, v1_ref, v2_ref)

    def build_bias():
        row_i = lax.broadcasted_iota(jnp.int32, (tq, tq), 0)
        col_i = lax.broadcasted_iota(jnp.int32, (tq, tq), 1)
        for j in range(ATT_B_NKB):
            back = (ATT_B_NKB - 1 - j) * tq
            if consecutive:
                rel = row_i - (col_i - back)
            else:
                rel = posq_ref[...] - posk_ref[jnp.maximum(i - (ATT_B_NKB - 1) + j, 0)]
            r = jnp.clip(rel, -B_REL_CLIP, B_REL_CLIP) + B_REL_CLIP
            dchunk = (row_i >> CHUNK_SHIFT) - ((col_i - back) >> CHUNK_SHIFT)
            band = (dchunk >= 0) & (dchunk <= B_PREV_CHUNKS)
            for hl in range(ATT_B_LG * nh):
                h = gp * (ATT_B_LG * nh) + hl
                row = relt_ref[pl.ds(h, 1), :] * LOG2E
                seg0 = jnp.broadcast_to(row[:, :LANE], (tq, LANE))
                seg1 = jnp.broadcast_to(row[:, LANE:2 * LANE], (tq, LANE))
                t256 = t256_ref[h] * LOG2E
                for c in range(tq // LANE):
                    cs = slice(c * LANE, (c + 1) * LANE)
                    rc = r[:, cs]
                    lo = rc & (LANE - 1)
                    bias = jnp.where(rc < LANE, _lane_gather(seg0, lo),
                                     jnp.where(rc < 2 * LANE, _lane_gather(seg1, lo), t256))
                    bias_ref[hl, :, j * tq + c * LANE:j * tq + (c + 1) * LANE] = jnp.where(band[:, cs], bias, -jnp.inf)

    if consecutive:
        pl.when(i == 0)(build_bias)
    else:
        build_bias()

    lane_head = lax.broadcasted_iota(jnp.int32, (1, LANE), 1) >> CHUNK_SHIFT
    for lg in range(ATT_B_LG):
        lanes = slice(lg * LANE, (lg + 1) * LANE)
        qp = q_ref[:, lanes]
        out = jnp.zeros((tq, LANE), F32)
        for hh in range(nh):
            mine = lane_head == hh
            qm = jnp.where(mine, qp, jnp.zeros_like(qp))
            parts = []
            for j in range(ATT_B_NKB):
                sj = lax.dot_general(qm, k_refs[j][:, lanes], NT_DIMS, preferred_element_type=F32)
                sj = sj + bias_ref[lg * nh + hh, :, j * tq:(j + 1) * tq]
                if j < ATT_B_NKB - 1:
                    sj = sj + jnp.where(i - (ATT_B_NKB - 1) + j >= 0, 0.0, -jnp.inf)
                parts.append(sj)
            s = jnp.concatenate(parts, axis=1)
            p = jnp.exp2(s - jnp.max(s, axis=1, keepdims=True)).astype(BF16)
            acc = jnp.zeros((tq, LANE), F32)
            for j in range(ATT_B_NKB):
                vj = v_refs[j][:, lanes]
                vm = jnp.where(mine, vj, jnp.ones_like(vj))
                acc = acc + jnp.dot(p[:, j * tq:(j + 1) * tq], vm, preferred_element_type=F32)
            rowsum = pltpu.roll(acc, B_HEAD_DIM, 1)
            out = out + jnp.where(mine, acc / rowsum, 0.0)
        o_ref[:, lanes] = out.astype(o_ref.dtype)


def band_attention(qkv, pos, rel_table, *, consecutive):
    s = qkv.shape[0]
    tq = ATT_B_TQ
    hw = B_HEADS * B_HEAD_DIM
    bw = ATT_B_LG * LANE
    ngrp = hw // bw
    assert LANE // B_HEAD_DIM == 2
    nrel = 2 * B_REL_CLIP + 1
    relt = jnp.zeros((B_HEADS, 3 * LANE), F32).at[:, :nrel].set(rel_table.T)
    t256 = rel_table[nrel - 1]

    def kv_spec(j, base):
        return pl.BlockSpec((tq, bw), lambda gp, i: (jnp.maximum(i - (ATT_B_NKB - 1) + j, 0), base + gp))

    return pl.pallas_call(
        functools.partial(_attn_b_kernel, consecutive=consecutive),
        name="band_attention",
        grid=(ngrp, s // tq),
        in_specs=[
            pl.BlockSpec(memory_space=pltpu.SMEM),
            pl.BlockSpec((tq, bw), lambda gp, i: (i, gp)),
            kv_spec(0, ngrp), kv_spec(1, ngrp), kv_spec(2, ngrp),
            kv_spec(0, 2 * ngrp), kv_spec(1, 2 * ngrp), kv_spec(2, 2 * ngrp),
            pl.BlockSpec((tq, 1), lambda gp, i: (i, 0)),
            pl.BlockSpec((s // tq, 1, tq), lambda gp, i: (0, 0, 0)),
            pl.BlockSpec((B_HEADS, 3 * LANE), lambda gp, i: (0, 0)),
        ],
        out_specs=pl.BlockSpec((tq, bw), lambda gp, i: (i, gp)),
        out_shape=jax.ShapeDtypeStruct((s, hw), BF16),
        scratch_shapes=[pltpu.VMEM((ATT_B_LG * (LANE // B_HEAD_DIM), tq, ATT_B_NKB * tq), F32)],
        compiler_params=_params(("parallel", "arbitrary")),
    )(t256, qkv, qkv, qkv, qkv, qkv, qkv, qkv, pos.reshape(s, 1), pos.reshape(s // tq, 1, tq), relt)


ATT_C_T = 512
ATT_C_HEADS = 4


def _attn_c_kernel(q_ref, kn_ref, kr_ref, v_ref, o_ref, m_ref, l_ref, acc_ref):
    t, nh = ATT_C_T, ATT_C_HEADS
    iq = pl.program_id(1)
    m_ref[...] = jnp.full(m_ref.shape, NEG_BIG, F32)
    l_ref[...] = jnp.zeros(l_ref.shape, F32)
    acc_ref[...] = jnp.zeros(acc_ref.shape, F32)

    def tile(kt, diag):
        k0 = pl.multiple_of(kt * t, t)
        kr_t = kr_ref[pl.ds(k0, t), :]
        for hh in range(nh):
            kcat = jnp.concatenate([kn_ref[pl.ds(k0, t), hh * C_NOPE:(hh + 1) * C_NOPE], kr_t], axis=1)
            s = lax.dot_general(q_ref[:, hh * 2 * LANE:(hh + 1) * 2 * LANE], kcat, NT_DIMS,
                                preferred_element_type=F32)
            if diag:
                qc = lax.broadcasted_iota(jnp.int32, (t, t), 0) >> CHUNK_SHIFT
                kc = lax.broadcasted_iota(jnp.int32, (t, t), 1) >> CHUNK_SHIFT
                s = jnp.where(kc <= qc, s, -jnp.inf)
            _softmax_step(s, v_ref[pl.ds(k0, t), hh * C_V:(hh + 1) * C_V], m_ref, l_ref, acc_ref,
                          slice(hh * t, (hh + 1) * t), t)

    _for_tiles_in_pairs(0, iq, lambda kt: tile(kt, False), per_trip=(4, 2, 1))
    tile(iq, True)
    for hh in range(nh):
        rows = slice(hh * t, (hh + 1) * t)
        o_ref[:, hh * C_V:(hh + 1) * C_V] = (acc_ref[rows, :] / l_ref[rows, :]).astype(o_ref.dtype)


def mla_attention(qcat, kv, kr):
    s = qcat.shape[0]
    t, nh = ATT_C_T, ATT_C_HEADS
    assert s % t == 0
    return pl.pallas_call(
        _attn_c_kernel,
        name="mla_attention",
        grid=(C_HEADS // nh, s // t),
        in_specs=[
            pl.BlockSpec((t, nh * 2 * LANE), lambda h, i: (i, h)),
            pl.BlockSpec((s, nh * C_NOPE), lambda h, i: (0, h)),
            pl.BlockSpec((s, LANE), lambda h, i: (0, 0)),
            pl.BlockSpec((s, nh * C_V), lambda h, i: (0, C_HEADS // nh + h)),
        ],
        out_specs=pl.BlockSpec((t, nh * C_V), lambda h, i: (i, h)),
        out_shape=jax.ShapeDtypeStruct((s, C_HEADS * C_V), BF16),
        scratch_shapes=[pltpu.VMEM((nh * t, LANE), F32), pltpu.VMEM((nh * t, LANE), F32),
                        pltpu.VMEM((nh * t, C_V), F32)],
        compiler_params=_params(("parallel", "arbitrary")),
    )(qcat, kv, kr, kv)


def _rope_tables(pos):
    half = ROPE_DIM // 2
    inv = ROPE_BASE ** (-jnp.arange(half, dtype=F32) * 2.0 / ROPE_DIM)
    ang = pos.astype(F32)[:, None] * inv
    cos, sin = jnp.cos(ang), jnp.sin(ang)
    s = pos.shape[0]
    z = jnp.zeros((s, half), F32)
    c = jnp.concatenate([cos, cos, jnp.ones((s, LANE - ROPE_DIM), F32)], axis=1)
    s1 = jnp.concatenate([-sin, z, z, z], axis=1)
    s2 = jnp.concatenate([z, sin, z, z], axis=1)
    return c, s1, s2


def _query_scale(n_query, n_total, scale):
    return jnp.concatenate([jnp.full((n_query,), scale * LOG2E, F32), jnp.ones((n_total - n_query,), F32)])


def _by_position_layout(consecutive, fn, *args):
    return lax.cond(consecutive, functools.partial(fn, consecutive=True),
                    functools.partial(fn, consecutive=False), *args)


def _mixer_a(h, gain, pos, consecutive, tables, w_in, w_out, layer, t5_table):
    s = h.shape[0]
    d = h.shape[1]
    nq = A_HEADS * A_HEAD_DIM
    nkv = A_KV_HEADS * A_HEAD_DIM
    nqi = IDX_HEADS * IDX_DIM
    o_qkv = nq + 2 * nkv
    o_ki = o_qkv + nqi
    tn = 1024
    assert o_qkv % tn == 0 and nqi % tn == 0
    qkv = norm_mm(h, 0, d, gain, w_in, layer=layer, n=o_qkv, tn=tn, out_dtype=BF16, name="a_qkv_proj",
                  col_scale=_query_scale(nq, o_qkv, A_HEAD_DIM ** -0.5))
    qi = norm_mm(h, 0, d, gain, w_in, layer=layer, n=nqi, col0=o_qkv // tn, tn=tn, out_dtype=BF16,
                 rope=(True,) * (tn // LANE), tables=tables, name="a_idxq_proj")
    w_kw = jnp.zeros((d, 2 * LANE), BF16).at[:, :IDX_DIM + IDX_HEADS].set(w_in[layer, :, o_ki:].astype(BF16))
    kiwi = norm_mm(h, 0, d, gain, w_kw, tn=2 * LANE, out_dtype=F32, rope=(True, False), tables=tables,
                   name="a_idxk_proj")
    mask = dsa_indexer(qi, kiwi, topk=min(IDX_TOPK_MAX, s // 4))
    o = _by_position_layout(consecutive, dsa_attention, qkv, mask, pos, t5_table)
    return mm_residual(o, w_out, h, layer=layer, tn=w_out.shape[-1])


def _mixer_b(h, gain, pos, consecutive, w_in, rel_table, w_out, layer):
    d = h.shape[1]
    n = w_in.shape[-1]
    qkv = norm_mm(h, 0, d, gain, w_in, layer=layer, tn=1024, out_dtype=BF16, name="b_qkv_proj",
                  col_scale=_query_scale(n // 3, n, B_HEAD_DIM ** -0.5))
    o = _by_position_layout(consecutive, band_attention, qkv, pos, rel_table)
    return mm_residual(o, w_out, h, layer=layer, tn=w_out.shape[-1])


def _mixer_c(h, gain, tables, w_down, g_q, g_kv, w_uq, w_ukv, w_out, layer):
    d = h.shape[1]
    lq = g_q.shape[0]
    lkv = g_kv.shape[0]
    n_down = lq + lkv + LANE
    wd = jnp.zeros((d, n_down), BF16).at[:, :w_down.shape[1]].set(w_down.astype(BF16))
    down = norm_mm(h, 0, d, gain, wd, tn=n_down, out_dtype=F32,
                   rope=(False,) * ((lq + lkv) // LANE) + (True,), tables=tables, name="c_down_proj")
    wq = w_uq.astype(BF16).reshape(lq, C_HEADS, C_NOPE + C_ROPE)
    wq = jnp.pad(wq, ((0, 0), (0, 0), (0, 2 * LANE - C_NOPE - C_ROPE))).reshape(lq, C_HEADS * 2 * LANE)
    nqc = wq.shape[1]
    qcat = norm_mm(down, 0, lq, g_q, wq, tn=nqc, out_dtype=BF16, rope=(False, True) * C_HEADS, tables=tables,
                   name="c_uq_proj", col_scale=_query_scale(nqc, nqc, (C_NOPE + C_ROPE) ** -0.5))
    wkv = w_ukv.astype(BF16).reshape(lkv, C_HEADS, 2, C_NOPE).transpose(0, 2, 1, 3).reshape(lkv, -1)
    assert lq == lkv
    kv = norm_mm(down, 1, lkv, g_kv, wkv, tn=wkv.shape[1], out_dtype=BF16, name="c_ukv_proj")
    kr = down[:, lq + lkv:].astype(BF16)
    o = mla_attention(qcat, kv, kr)
    return mm_residual(o, w_out, h, layer=layer, tn=w_out.shape[-1])


def kernel(x, p, positions, t5_table, a_w_in, a_w_out, b_w_in, b_rel_table, b_w_out, c_w_down, c_q_norm,
           c_kv_norm, c_w_uq, c_w_ukv, c_w_out, attn_norm, ffn_norm, ffn_w_in, ffn_w_out, ple_norm,
           ple_w_gate, ple_w_proj, final_norm):
    assert x.shape[0] == 1
    depth = attn_norm.shape[0]
    h = x[0]
    pos = positions[0]
    tables = _rope_tables(pos)
    consecutive = jnp.all(pos[1:] - pos[:-1] == 1)
    a_w_in, a_w_out, b_w_in, b_w_out, c_w_out, ffn_w_in, ffn_w_out, ple_w_gate, ple_w_proj = (
        w.astype(BF16) for w in (a_w_in, a_w_out, b_w_in, b_w_out, c_w_out, ffn_w_in, ffn_w_out, ple_w_gate,
                                 ple_w_proj))
    for i in range(depth):
        j, kind = divmod(i, 3)
        if kind == 0:
            h = _mixer_a(h, attn_norm[i], pos, consecutive, tables, a_w_in, a_w_out, j, t5_table)
        elif kind == 1:
            h = _mixer_b(h, attn_norm[i], pos, consecutive, b_w_in, b_rel_table[j], b_w_out, j)
        else:
            h = _mixer_c(h, attn_norm[i], tables, c_w_down[j], c_q_norm[j], c_kv_norm[j], c_w_uq[j],
                         c_w_ukv[j], c_w_out, j)
        act = ffn_in(h, ffn_norm[i], ffn_w_in, layer=i)
        h = mm_residual(act, ffn_w_out, h, layer=i, tn=1024)
        h = ple(h, ple_norm[i], ple_w_gate, p, ple_w_proj, layer=i)
    return final_rms_norm(h, final_norm)[None]
```

```python
import functools
import math

import jax
import jax.numpy as jnp
from jax import lax
from jax.experimental import pallas as pl
from jax.experimental.pallas import tpu as pltpu

LANE = 128
VMEM_LIMIT_BYTES = 56 * 1024 * 1024

CHUNK = 64
CHUNK_SHIFT = CHUNK.bit_length() - 1
EPS = 1e-6
ROPE_BASE = 10000.0
ROPE_DIM = 64
A_HEADS = 16
A_KV_HEADS = 4
A_GROUP = 4
A_HEAD_DIM = 128
IDX_HEADS = 16
IDX_DIM = 128
IDX_TOPK_MAX = 256
T5_BUCKETS = 32
T5_MAX_DISTANCE = 1024
_T5_EXACT = T5_BUCKETS // 4
_T5_LAST = _T5_EXACT * (T5_MAX_DISTANCE / _T5_EXACT) ** ((_T5_EXACT - 1) / _T5_EXACT)
T5_FAR = math.ceil(_T5_LAST / LANE) * LANE
assert T5_FAR == 640
B_HEADS = 32
B_HEAD_DIM = 64
B_PREV_CHUNKS = 8
B_REL_CLIP = 128
C_HEADS = 16
C_NOPE = 128
C_ROPE = 64
C_V = 128
NEG_BIG = -1e30
INT_MIN = -(2 ** 31)
LOG2E = math.log2(math.e)

F32 = jnp.float32
BF16 = jnp.bfloat16
NT_DIMS = (((1,), (1,)), ((), ()))


def _params(sem):
    return pltpu.CompilerParams(dimension_semantics=sem, vmem_limit_bytes=VMEM_LIMIT_BYTES)


def _rms(x, g):
    ms = jnp.mean(x * x, axis=-1, keepdims=True)
    return (x * lax.rsqrt(ms + EPS)) * g


def _lane_gather(table, idx):
    return jnp.take_along_axis(table, idx, axis=1, mode="promise_in_bounds")


def _rope_group(y, c, s1, s2):
    return y * c + pltpu.roll(y, 96, 1) * s1 + pltpu.roll(y, 32, 1) * s2


def _norm_mm_kernel(*refs, rope, scaled):
    x_ref, g_ref, w_ref = refs[:3]
    rest = list(refs[3:])
    cs_ref = rest.pop(0) if scaled else None
    c_ref, s1_ref, s2_ref = (rest.pop(0), rest.pop(0), rest.pop(0)) if rope is not None else (None,) * 3
    o_ref, xn_ref = rest

    @pl.when(pl.program_id(1) == 0)
    def _():
        xn_ref[...] = _rms(x_ref[...], g_ref[...]).astype(BF16)

    y = jnp.dot(xn_ref[...], w_ref[...], preferred_element_type=F32)
    if scaled:
        y = y * cs_ref[...]
    if rope is None:
        o_ref[...] = y.astype(o_ref.dtype)
    else:
        c, s1, s2 = c_ref[...], s1_ref[...], s2_ref[...]
        for gi, on in enumerate(rope):
            sl = slice(gi * LANE, (gi + 1) * LANE)
            yg = y[:, sl]
            if on:
                yg = _rope_group(yg, c, s1, s2)
            o_ref[:, sl] = yg.astype(o_ref.dtype)


def _w_spec(w, layer, k, tn, col0=0):
    if w.ndim == 2:
        return pl.BlockSpec((k, tn), lambda i, j: (0, col0 + j))
    return pl.BlockSpec((None, k, tn), lambda i, j: (layer, 0, col0 + j))


def norm_mm(x, x_col, kx, gain, w, *, tn, out_dtype, name, n=None, layer=None, col0=0, tm=512, rope=None,
            tables=None, col_scale=None):
    s = x.shape[0]
    n = w.shape[-1] if n is None else n
    tm = min(tm, s)
    in_specs = [
        pl.BlockSpec((tm, kx), lambda i, j: (i, x_col)),
        pl.BlockSpec((1, kx), lambda i, j: (0, 0)),
        _w_spec(w, layer, kx, tn, col0),
    ]
    args = [x, gain.reshape(1, kx), w]
    if col_scale is not None:
        in_specs.append(pl.BlockSpec((1, tn), lambda i, j: (0, j)))
        args.append(col_scale.reshape(1, n))
    if rope is not None:
        assert len(rope) == tn // LANE
        in_specs += [pl.BlockSpec((tm, LANE), lambda i, j: (i, 0))] * 3
        args += list(tables)
    return pl.pallas_call(
        functools.partial(_norm_mm_kernel, rope=rope, scaled=col_scale is not None),
        grid=(s // tm, n // tn),
        in_specs=in_specs,
        out_specs=pl.BlockSpec((tm, tn), lambda i, j: (i, j)),
        out_shape=jax.ShapeDtypeStruct((s, n), out_dtype),
        scratch_shapes=[pltpu.VMEM((tm, kx), BF16)],
        compiler_params=_params(("parallel", "arbitrary")),
        name=name,
    )(*args)


def _mm_res_kernel(x_ref, w_ref, r_ref, o_ref):
    o_ref[...] = r_ref[...] + jnp.dot(x_ref[...], w_ref[...], preferred_element_type=F32)


def mm_residual(x, w, res, *, layer=None, tm=512, tn=512):
    s, k = x.shape
    n = w.shape[-1]
    tm = min(tm, s)
    return pl.pallas_call(
        _mm_res_kernel,
        name="mm_residual",
        grid=(s // tm, n // tn),
        in_specs=[
            pl.BlockSpec((tm, k), lambda i, j: (i, 0)),
            _w_spec(w, layer, k, tn),
            pl.BlockSpec((tm, tn), lambda i, j: (i, j)),
        ],
        out_specs=pl.BlockSpec((tm, tn), lambda i, j: (i, j)),
        out_shape=jax.ShapeDtypeStruct((s, n), F32),
        compiler_params=_params(("parallel", "arbitrary")),
    )(x, w, res)


def _ffn_in_kernel(x_ref, g_ref, wg_ref, wu_ref, o_ref, xn_ref):
    @pl.when(pl.program_id(1) == 0)
    def _():
        xn_ref[...] = _rms(x_ref[...], g_ref[...]).astype(BF16)

    xn = xn_ref[...]
    a = jnp.dot(xn, wg_ref[...], preferred_element_type=F32)
    u = jnp.dot(xn, wu_ref[...], preferred_element_type=F32)
    o_ref[...] = (a * jax.nn.sigmoid(a) * u).astype(o_ref.dtype)


def ffn_in(h, gain, w_in, *, layer=None, tm=1024, tn=512):
    s, d = h.shape
    f = w_in.shape[-1] // 2
    tm = min(tm, s)
    nj = f // tn
    return pl.pallas_call(
        _ffn_in_kernel,
        name="ffn_in",
        grid=(s // tm, nj),
        in_specs=[
            pl.BlockSpec((tm, d), lambda i, j: (i, 0)),
            pl.BlockSpec((1, d), lambda i, j: (0, 0)),
            _w_spec(w_in, layer, d, tn),
            _w_spec(w_in, layer, d, tn, nj),
        ],
        out_specs=pl.BlockSpec((tm, tn), lambda i, j: (i, j)),
        out_shape=jax.ShapeDtypeStruct((s, f), BF16),
        scratch_shapes=[pltpu.VMEM((tm, d), BF16)],
        compiler_params=_params(("parallel", "arbitrary")),
    )(h, gain.reshape(1, d), w_in, w_in)


def _ple_kernel(x_ref, g_ref, wg_ref, p_ref, wp_ref, h_ref, o_ref, xn_ref, pb_ref):
    @pl.when(pl.program_id(1) == 0)
    def _():
        xn_ref[...] = _rms(x_ref[...], g_ref[...]).astype(BF16)
        pb_ref[...] = p_ref[...].astype(BF16)

    gate = jax.nn.sigmoid(jnp.dot(xn_ref[...], wg_ref[...], preferred_element_type=F32))
    proj = jnp.dot(pb_ref[...], wp_ref[...], preferred_element_type=F32)
    o_ref[...] = h_ref[...] + gate * proj


def ple(h, gain, w_gate, p, w_proj, *, layer, tm=512, tn=2048):
    s, d = h.shape
    pd = p.shape[-1]
    tm = min(tm, s)
    return pl.pallas_call(
        _ple_kernel,
        name="ple",
        grid=(s // tm, d // tn),
        in_specs=[
            pl.BlockSpec((tm, d), lambda i, j: (i, 0)),
            pl.BlockSpec((1, d), lambda i, j: (0, 0)),
            _w_spec(w_gate, layer, d, tn),
            pl.BlockSpec((None, None, tm, pd), lambda i, j: (layer, 0, i, 0)),
            _w_spec(w_proj, layer, pd, tn),
            pl.BlockSpec((tm, tn), lambda i, j: (i, j)),
        ],
        out_specs=pl.BlockSpec((tm, tn), lambda i, j: (i, j)),
        out_shape=jax.ShapeDtypeStruct((s, d), F32),
        scratch_shapes=[pltpu.VMEM((tm, d), BF16), pltpu.VMEM((tm, pd), BF16)],
        compiler_params=_params(("parallel", "arbitrary")),
    )(h, gain.reshape(1, d), w_gate, p, w_proj, h)


def _final_norm_kernel(x_ref, g_ref, o_ref):
    o_ref[...] = _rms(x_ref[...], g_ref[...])


def final_rms_norm(h, gain, *, tm=512):
    s, d = h.shape
    tm = min(tm, s)
    return pl.pallas_call(
        _final_norm_kernel,
        name="final_norm",
        grid=(s // tm,),
        in_specs=[pl.BlockSpec((tm, d), lambda i: (i, 0)), pl.BlockSpec((1, d), lambda i: (0, 0))],
        out_specs=pl.BlockSpec((tm, d), lambda i: (i, 0)),
        out_shape=jax.ShapeDtypeStruct((s, d), F32),
        compiler_params=_params(("parallel",)),
    )(h, gain.reshape(1, d))


IDX_TQ = 256
IDX_KC = 256


def _orderable(x):
    b = pltpu.bitcast(x, jnp.int32)
    return jnp.where(b < 0, b ^ jnp.int32(0x7FFFFFFF), b)


def _indexer_kernel(qi_ref, ki_ref, wi_ref, mask_ref, keys_ref, *, topk, n_kblocks):
    tq, kc = IDX_TQ, IDX_KC
    i = pl.program_id(0)
    nkc = (i + 1) * (tq // kc)
    n_idx_bits = (n_kblocks * kc - 1).bit_length()
    w_t = (wi_ref[...] * (IDX_HEADS ** -0.5 * IDX_DIM ** -0.5)).T
    qchunk = (i * tq + lax.broadcasted_iota(jnp.int32, (kc, tq), 1)) >> CHUNK_SHIFT
    krow = lax.broadcasted_iota(jnp.int32, (kc, tq), 0)

    def valid_of(c):
        return ((c * kc + krow) >> CHUNK_SHIFT) <= qchunk

    def score_chunk(c):
        k0 = pl.multiple_of(c * kc, kc)
        kblk = ki_ref[pl.ds(k0, kc), :].astype(BF16)
        acc = jnp.zeros((kc, tq), F32)
        for h in range(IDX_HEADS):
            s = lax.dot_general(kblk, qi_ref[:, h * IDX_DIM:(h + 1) * IDX_DIM], NT_DIMS,
                                preferred_element_type=F32)
            acc = acc + jnp.maximum(s, 0.0) * w_t[h:h + 1, :]
        keys_ref[pl.ds(k0, kc), :] = jnp.where(valid_of(c), _orderable(acc), jnp.int32(INT_MIN))

    _for_tiles_in_pairs(0, nkc, score_chunk, per_trip=(4, 2, 1))

    @pl.when(nkc % 2 == 1)
    def _():
        keys_ref[pl.ds(pl.multiple_of(nkc * kc, kc), kc), :] = jnp.full((kc, tq), INT_MIN, jnp.int32)

    def count(pred_of):
        nacc = 4

        def body(cp, cnt):
            for c in (2 * cp, 2 * cp + 1):
                k0 = pl.multiple_of(c * kc, kc)
                hit = pred_of(c, keys_ref[pl.ds(k0, kc), :]).astype(jnp.int32)
                cnt = cnt + hit.reshape(kc // (8 * nacc), nacc, 8, tq).sum(axis=0)
            return cnt
        cnt = lax.fori_loop(0, (nkc + 1) // 2, body, jnp.zeros((nacc, 8, tq), jnp.int32))
        return cnt.sum(axis=0).sum(axis=0, keepdims=True)

    def bit_step(b, state):
        thr, cnt = state
        cand = thr + lax.shift_left(jnp.int32(1), 31 - b)
        c_cand = count(lambda c, keys: keys >= cand)
        take = c_cand >= topk
        return jnp.where(take, cand, thr), jnp.where(take, c_cand, cnt)

    thr, cnt = lax.fori_loop(
        0, 32, bit_step,
        (jnp.full((1, tq), INT_MIN, jnp.int32), jnp.full((1, tq), nkc * kc, jnp.int32)))

    def tie_limit():
        need = topk - count(lambda c, keys: keys > thr)
        def bit_body(b, x):
            cand = x + lax.shift_left(jnp.int32(1), n_idx_bits - 1 - b)
            below = count(lambda c, keys: (keys == thr) & ((c * kc + krow) < cand))
            return jnp.where(below < need, cand, x)
        return lax.fori_loop(0, n_idx_bits, bit_body, jnp.zeros((1, tq), jnp.int32))

    def write_masks(selected):
        def mask_body(c, carry):
            k0 = pl.multiple_of(c * kc, kc)
            sel = selected(c, keys_ref[pl.ds(k0, kc), :]) & valid_of(c)
            mask_ref[c] = jnp.where(sel, 0.0, -jnp.inf).astype(F32).T
            return carry
        lax.fori_loop(0, nkc, mask_body, 0)

    has_ties = jnp.max(jnp.where((cnt > topk) & (thr > INT_MIN), 1, 0)) > 0

    @pl.when(has_ties)
    def _():
        tie_last = tie_limit()
        write_masks(lambda c, keys: (keys > thr) | ((keys == thr) & ((c * kc + krow) <= tie_last)))

    @pl.when(jnp.logical_not(has_ties))
    def _():
        write_masks(lambda c, keys: keys >= thr)

    def fill_body(c, carry):
        mask_ref[c] = jnp.full((tq, kc), -jnp.inf, F32)
        return carry

    lax.fori_loop(nkc, n_kblocks, fill_body, 0)


def dsa_indexer(qi, kiwi, *, topk):
    s = qi.shape[0]
    tq, kc = IDX_TQ, IDX_KC
    nkb = s // kc
    assert s % tq == 0 and nkb % 2 == 0
    return pl.pallas_call(
        functools.partial(_indexer_kernel, topk=topk, n_kblocks=nkb),
        name="dsa_indexer",
        grid=(s // tq,),
        in_specs=[
            pl.BlockSpec((tq, IDX_HEADS * IDX_DIM), lambda i: (i, 0)),
            pl.BlockSpec((s, IDX_DIM), lambda i: (0, 0)),
            pl.BlockSpec((tq, LANE), lambda i: (i, 1)),
        ],
        out_specs=pl.BlockSpec((nkb, tq, kc), lambda i: (0, i, 0)),
        out_shape=jax.ShapeDtypeStruct((nkb, s, kc), F32),
        scratch_shapes=[pltpu.VMEM((s, tq), jnp.int32)],
        compiler_params=_params(("parallel",)),
    )(qi, kiwi, kiwi)


ATT_A_TQ = 256
ATT_A_KB = 2
ATT_A_NEAR = (T5_FAR + ATT_A_KB * IDX_KC - 2) // IDX_KC + 1


def _t5_bucket(rel):
    nb = T5_BUCKETS // 2
    max_exact = nb // 2
    ret = jnp.where(rel > 0, nb, 0)
    n = jnp.abs(rel)
    nf = jnp.maximum(n, 1).astype(F32)
    large = max_exact + (jnp.log(nf / max_exact) / math.log(T5_MAX_DISTANCE / max_exact)
                         * (nb - max_exact)).astype(jnp.int32)
    large = jnp.minimum(large, nb - 1)
    return ret + jnp.where(n < max_exact, n, large)


def _softmax_step(s, v, m_ref, l_ref, acc_ref, rows, tk, shift=None):
    d = v.shape[1]
    m_prev = m_ref[rows, :]
    mx = jnp.max(s, axis=1, keepdims=True)
    if shift is not None:
        mx = mx + shift
    m_next = jnp.maximum(m_prev, mx)
    pivot = m_next if shift is None else m_next - shift
    p = jnp.exp2(s - jnp.concatenate([pivot] * (tk // LANE), axis=1)).astype(BF16)
    alpha = jnp.exp2(m_prev - m_next)
    v_ones = jnp.concatenate([v, jnp.ones((tk, LANE), BF16)], axis=1)
    pv = jnp.dot(p, v_ones, preferred_element_type=F32)
    m_ref[rows, :] = m_next
    l_ref[rows, :] = alpha * l_ref[rows, :] + pv[:, d:]
    acc_ref[rows, :] = acc_ref[rows, :] * alpha + pv[:, :d]


def _for_tiles_in_pairs(lo, hi, tile_fn, per_trip=(2, 1)):
    assert per_trip[-1] == 1
    start = lo
    for size in per_trip:
        def group(kp, carry, size=size, start=start):
            for u in range(size):
                tile_fn(start + size * kp + u)
            return carry

        n = (hi - start) // size
        lax.fori_loop(0, n, group, 0)
        start = start + n * size


def _attn_a_kernel(qmin_ref, kmax_ref, t5s_ref, q_ref, k_ref, v_ref, mask_ref, posq_ref, posk_ref,
                   t5t_ref, o_ref, m_ref, l_ref, acc_ref, nbias_ref, *, consecutive):
    tq, kc, nb = ATT_A_TQ, IDX_KC, ATT_A_KB
    tk = nb * kc
    g = pl.program_id(0)
    i = pl.program_id(1)

    def head_bias(r, bucket):
        row = t5t_ref[pl.ds(g * A_GROUP + r, 1), :] * LOG2E
        tbl = jnp.broadcast_to(row, (tq, LANE))
        return jnp.concatenate([_lane_gather(tbl, bucket[:, c * LANE:(c + 1) * LANE])
                                for c in range(tk // LANE)], axis=1)

    if consecutive:
        @pl.when(i == 0)
        def _():
            rel0 = (lax.broadcasted_iota(jnp.int32, (tq, tk), 1)
                    - lax.broadcasted_iota(jnp.int32, (tq, tk), 0))
            for d in range(ATT_A_NEAR):
                bucket = _t5_bucket(rel0 + (d - ATT_A_NEAR + 1) * kc)
                for r in range(A_GROUP):
                    nbias_ref[d, r] = head_bias(r, bucket)

    m_ref[...] = jnp.full(m_ref.shape, NEG_BIG, F32)
    l_ref[...] = jnp.zeros(l_ref.shape, F32)
    acc_ref[...] = jnp.zeros(acc_ref.shape, F32)
    nkt = ((i + 1) * (tq // kc) + nb - 1) // nb
    qmin = qmin_ref[i]

    def is_far(kt):
        ktc = jnp.minimum(kt, nkt - 1) * nb
        kmax = kmax_ref[ktc]
        for b in range(1, nb):
            kmax = jnp.maximum(kmax, kmax_ref[ktc + b])
        return (kt < nkt) & ((qmin - kmax) >= T5_FAR)

    n_far = lax.while_loop(is_far, lambda kt: kt + 1, jnp.int32(0))

    def tile(kt, bias_of, shift_of):
        k0 = pl.multiple_of(kt * tk, tk)
        k_t = k_ref[pl.ds(k0, tk), :]
        v_t = v_ref[pl.ds(k0, tk), :]
        base = jnp.concatenate([mask_ref[kt * nb + b] for b in range(nb)], axis=1)
        for r in range(A_GROUP):
            rows = slice(r * tq, (r + 1) * tq)
            q_r = q_ref[:, r * A_HEAD_DIM:(r + 1) * A_HEAD_DIM]
            s = lax.dot_general(q_r, k_t, NT_DIMS, preferred_element_type=F32) + bias_of(r, base)
            _softmax_step(s, v_t, m_ref, l_ref, acc_ref, rows, tk, shift_of(r))

    def far_tile(kt):
        tile(kt, lambda r, base: base,
             lambda r: t5s_ref[T5_BUCKETS // 2 - 1, g * A_GROUP + r] * LOG2E)

    def near_tile(kt):
        if consecutive:
            d = kt * nb - i * (tq // kc) + (ATT_A_NEAR - 1)
            tile(kt, lambda r, base: base + nbias_ref[d, r], lambda r: None)
        else:
            pk = jnp.concatenate([posk_ref[kt * nb + b] for b in range(nb)], axis=1)
            bucket = _t5_bucket(pk - posq_ref[...])
            tile(kt, lambda r, base: base + head_bias(r, bucket), lambda r: None)

    _for_tiles_in_pairs(0, n_far, far_tile, per_trip=(4, 2, 1))
    _for_tiles_in_pairs(n_far, nkt, near_tile)
    for r in range(A_GROUP):
        rows = slice(r * tq, (r + 1) * tq)
        o_ref[:, r * A_HEAD_DIM:(r + 1) * A_HEAD_DIM] = (acc_ref[rows, :] / l_ref[rows, :]).astype(o_ref.dtype)


def dsa_attention(qkv, mask, pos, t5_table, *, consecutive):
    s = qkv.shape[0]
    tq, tk = ATT_A_TQ, IDX_KC
    assert (s // tk) % ATT_A_KB == 0
    gw = A_GROUP * A_HEAD_DIM
    qmin = pos.reshape(s // tq, tq).min(axis=1)
    kmax = pos.reshape(s // tk, tk).max(axis=1)
    t5t = jnp.zeros((A_HEADS, LANE), F32).at[:, :T5_BUCKETS].set(t5_table.T)
    kblk0 = (A_HEADS * A_HEAD_DIM) // A_HEAD_DIM
    smem = pl.BlockSpec(memory_space=pltpu.SMEM)
    nbias_shape = (ATT_A_NEAR, A_GROUP, tq, ATT_A_KB * tk) if consecutive else (1, 1, 8, LANE)
    return pl.pallas_call(
        functools.partial(_attn_a_kernel, consecutive=consecutive),
        name="dsa_attention",
        grid=(A_KV_HEADS, s // tq),
        in_specs=[
            smem, smem, smem,
            pl.BlockSpec((tq, gw), lambda g, i: (i, g)),
            pl.BlockSpec((s, A_HEAD_DIM), lambda g, i: (0, kblk0 + g)),
            pl.BlockSpec((s, A_HEAD_DIM), lambda g, i: (0, kblk0 + A_KV_HEADS + g)),
            pl.BlockSpec((s // tk, tq, tk), lambda g, i: (0, i, 0)),
            pl.BlockSpec((tq, 1), lambda g, i: (i, 0)),
            pl.BlockSpec((s // tk, 1, tk), lambda g, i: (0, 0, 0)),
            pl.BlockSpec((A_HEADS, LANE), lambda g, i: (0, 0)),
        ],
        out_specs=pl.BlockSpec((tq, gw), lambda g, i: (i, g)),
        out_shape=jax.ShapeDtypeStruct((s, A_HEADS * A_HEAD_DIM), BF16),
        scratch_shapes=[
            pltpu.VMEM((A_GROUP * tq, LANE), F32),
            pltpu.VMEM((A_GROUP * tq, LANE), F32),
            pltpu.VMEM((A_GROUP * tq, A_HEAD_DIM), F32),
            pltpu.VMEM(nbias_shape, F32),
        ],
        compiler_params=_params(("parallel", "arbitrary")),
    )(qmin, kmax, t5_table, qkv, qkv, qkv, mask, pos.reshape(s, 1), pos.reshape(s // tk, 1, tk), t5t)


ATT_B_TQ = 256
ATT_B_NKB = B_PREV_CHUNKS * CHUNK // ATT_B_TQ + 1
ATT_B_LG = 4


def _attn_b_kernel(t256_ref, q_ref, k0_ref, k1_ref, k2_ref, v0_ref, v1_ref, v2_ref, posq_ref, posk_ref,
                   relt_ref, o_ref, bias_ref, *, consecutive):
    tq = ATT_B_TQ
    nh = LANE // B_HEAD_DIM
    gp = pl.program_id(0)
    i = pl.program_id(1)
    k_refs = (k0_ref, k1_ref, k2_ref)
    v_refs = (v0_ref, v1_ref, v2_ref)

    def build_bias():
        row_i = lax.broadcasted_iota(jnp.int32, (tq, tq), 0)
        col_i = lax.broadcasted_iota(jnp.int32, (tq, tq), 1)
        for j in range(ATT_B_NKB):
            back = (ATT_B_NKB - 1 - j) * tq
            if consecutive:
                rel = row_i - (col_i - back)
            else:
                rel = posq_ref[...] - posk_ref[jnp.maximum(i - (ATT_B_NKB - 1) + j, 0)]
            r = jnp.clip(rel, -B_REL_CLIP, B_REL_CLIP) + B_REL_CLIP
            dchunk = (row_i >> CHUNK_SHIFT) - ((col_i - back) >> CHUNK_SHIFT)
            band = (dchunk >= 0) & (dchunk <= B_PREV_CHUNKS)
            for hl in range(ATT_B_LG * nh):
                h = gp * (ATT_B_LG * nh) + hl
                row = relt_ref[pl.ds(h, 1), :] * LOG2E
                seg0 = jnp.broadcast_to(row[:, :LANE], (tq, LANE))
                seg1 = jnp.broadcast_to(row[:, LANE:2 * LANE], (tq, LANE))
                t256 = t256_ref[h] * LOG2E
                for c in range(tq // LANE):
                    cs = slice(c * LANE, (c + 1) * LANE)
                    rc = r[:, cs]
                    lo = rc & (LANE - 1)
                    bias = jnp.where(rc < LANE, _lane_gather(seg0, lo),
                                     jnp.where(rc < 2 * LANE, _lane_gather(seg1, lo), t256))
                    bias_ref[hl, :, j * tq + c * LANE:j * tq + (c + 1) * LANE] = jnp.where(band[:, cs], bias, -jnp.inf)

    if consecutive:
        pl.when(i == 0)(build_bias)
    else:
        build_bias()

    lane_head = lax.broadcasted_iota(jnp.int32, (1, LANE), 1) // B_HEAD_DIM
    for lg in range(ATT_B_LG):
        lanes = slice(lg * LANE, (lg + 1) * LANE)
        qp = q_ref[:, lanes]
        out = jnp.zeros((tq, LANE), F32)
        for hh in range(nh):
            mine = lane_head == hh
            qm = jnp.where(mine, qp, jnp.zeros_like(qp))
            parts = []
            for j in range(ATT_B_NKB):
                sj = lax.dot_general(qm, k_refs[j][:, lanes], NT_DIMS, preferred_element_type=F32)
                sj = sj + bias_ref[lg * nh + hh, :, j * tq:(j + 1) * tq]
                if j < ATT_B_NKB - 1:
                    sj = sj + jnp.where(i - (ATT_B_NKB - 1) + j >= 0, 0.0, -jnp.inf)
                parts.append(sj)
            s = jnp.concatenate(parts, axis=1)
            p = jnp.exp2(s - jnp.max(s, axis=1, keepdims=True)).astype(BF16)
            acc = jnp.zeros((tq, LANE), F32)
            for j in range(ATT_B_NKB):
                vj = v_refs[j][:, lanes]
                vm = jnp.where(mine, vj, jnp.ones_like(vj))
                acc = acc + jnp.dot(p[:, j * tq:(j + 1) * tq], vm, preferred_element_type=F32)
            rowsum = pltpu.roll(acc, B_HEAD_DIM, 1)
            out = out + jnp.where(mine, acc / rowsum, 0.0)
        o_ref[:, lanes] = out.astype(o_ref.dtype)


def band_attention(qkv, pos, rel_table, *, consecutive):
    s = qkv.shape[0]
    tq = ATT_B_TQ
    hw = B_HEADS * B_HEAD_DIM
    bw = ATT_B_LG * LANE
    ngrp = hw // bw
    assert LANE // B_HEAD_DIM == 2 and ATT_B_NKB == 3
    nrel = 2 * B_REL_CLIP + 1
    relt = jnp.zeros((B_HEADS, 3 * LANE), F32).at[:, :nrel].set(rel_table.T)
    t256 = rel_table[nrel - 1]

    def kv_spec(j, base):
        return pl.BlockSpec((tq, bw), lambda gp, i: (jnp.maximum(i - (ATT_B_NKB - 1) + j, 0), base + gp))

    return pl.pallas_call(
        functools.partial(_attn_b_kernel, consecutive=consecutive),
        name="band_attention",
        grid=(ngrp, s // tq),
        in_specs=[
            pl.BlockSpec(memory_space=pltpu.SMEM),
            pl.BlockSpec((tq, bw), lambda gp, i: (i, gp)),
            kv_spec(0, ngrp), kv_spec(1, ngrp), kv_spec(2, ngrp),
            kv_spec(0, 2 * ngrp), kv_spec(1, 2 * ngrp), kv_spec(2, 2 * ngrp),
            pl.BlockSpec((tq, 1), lambda gp, i: (i, 0)),
            pl.BlockSpec((s // tq, 1, tq), lambda gp, i: (0, 0, 0)),
            pl.BlockSpec((B_HEADS, 3 * LANE), lambda gp, i: (0, 0)),
        ],
        out_specs=pl.BlockSpec((tq, bw), lambda gp, i: (i, gp)),
        out_shape=jax.ShapeDtypeStruct((s, hw), BF16),
        scratch_shapes=[pltpu.VMEM((ATT_B_LG * (LANE // B_HEAD_DIM), tq, ATT_B_NKB * tq), F32)],
        compiler_params=_params(("parallel", "arbitrary")),
    )(t256, qkv, qkv, qkv, qkv, qkv, qkv, qkv, pos.reshape(s, 1), pos.reshape(s // tq, 1, tq), relt)


ATT_C_T = 512
ATT_C_HEADS = 4


def _attn_c_kernel(q_ref, kn_ref, kr_ref, v_ref, o_ref, m_ref, l_ref, acc_ref):
    t, nh = ATT_C_T, ATT_C_HEADS
    iq = pl.program_id(1)
    m_ref[...] = jnp.full(m_ref.shape, NEG_BIG, F32)
    l_ref[...] = jnp.zeros(l_ref.shape, F32)
    acc_ref[...] = jnp.zeros(acc_ref.shape, F32)

    def tile(kt, diag):
        k0 = pl.multiple_of(kt * t, t)
        kr_t = kr_ref[pl.ds(k0, t), :]
        for hh in range(nh):
            kcat = jnp.concatenate([kn_ref[pl.ds(k0, t), hh * C_NOPE:(hh + 1) * C_NOPE], kr_t], axis=1)
            s = lax.dot_general(q_ref[:, hh * 2 * LANE:(hh + 1) * 2 * LANE], kcat, NT_DIMS,
                                preferred_element_type=F32)
            if diag:
                qc = lax.broadcasted_iota(jnp.int32, (t, t), 0) >> CHUNK_SHIFT
                kc = lax.broadcasted_iota(jnp.int32, (t, t), 1) >> CHUNK_SHIFT
                s = jnp.where(kc <= qc, s, -jnp.inf)
            _softmax_step(s, v_ref[pl.ds(k0, t), hh * C_V:(hh + 1) * C_V], m_ref, l_ref, acc_ref,
                          slice(hh * t, (hh + 1) * t), t)

    _for_tiles_in_pairs(0, iq, lambda kt: tile(kt, False), per_trip=(4, 2, 1))
    tile(iq, True)
    for hh in range(nh):
        rows = slice(hh * t, (hh + 1) * t)
        o_ref[:, hh * C_V:(hh + 1) * C_V] = (acc_ref[rows, :] / l_ref[rows, :]).astype(o_ref.dtype)


def mla_attention(qcat, kv, kr):
    s = qcat.shape[0]
    t, nh = ATT_C_T, ATT_C_HEADS
    assert s % t == 0
    return pl.pallas_call(
        _attn_c_kernel,
        name="mla_attention",
        grid=(C_HEADS // nh, s // t),
        in_specs=[
            pl.BlockSpec((t, nh * 2 * LANE), lambda h, i: (i, h)),
            pl.BlockSpec((s, nh * C_NOPE), lambda h, i: (0, h)),
            pl.BlockSpec((s, LANE), lambda h, i: (0, 0)),
            pl.BlockSpec((s, nh * C_V), lambda h, i: (0, C_HEADS // nh + h)),
        ],
        out_specs=pl.BlockSpec((t, nh * C_V), lambda h, i: (i, h)),
        out_shape=jax.ShapeDtypeStruct((s, C_HEADS * C_V), BF16),
        scratch_shapes=[pltpu.VMEM((nh * t, LANE), F32), pltpu.VMEM((nh * t, LANE), F32),
                        pltpu.VMEM((nh * t, C_V), F32)],
        compiler_params=_params(("parallel", "arbitrary")),
    )(qcat, kv, kr, kv)


def _rope_tables(pos):
    half = ROPE_DIM // 2
    inv = ROPE_BASE ** (-jnp.arange(half, dtype=F32) * 2.0 / ROPE_DIM)
    ang = pos.astype(F32)[:, None] * inv
    cos, sin = jnp.cos(ang), jnp.sin(ang)
    s = pos.shape[0]
    z = jnp.zeros((s, half), F32)
    c = jnp.concatenate([cos, cos, jnp.ones((s, LANE - ROPE_DIM), F32)], axis=1)
    s1 = jnp.concatenate([-sin, z, z, z], axis=1)
    s2 = jnp.concatenate([z, sin, z, z], axis=1)
    return c, s1, s2


def _query_scale(n_query, n_total, scale):
    return jnp.concatenate([jnp.full((n_query,), scale * LOG2E, F32), jnp.ones((n_total - n_query,), F32)])


def _by_position_layout(consecutive, fn, *args):
    return lax.cond(consecutive, functools.partial(fn, consecutive=True),
                    functools.partial(fn, consecutive=False), *args)


def _mixer_a(h, gain, pos, consecutive, tables, w_in, w_out, layer, t5_table):
    s = h.shape[0]
    d = h.shape[1]
    nq = A_HEADS * A_HEAD_DIM
    nkv = A_KV_HEADS * A_HEAD_DIM
    nqi = IDX_HEADS * IDX_DIM
    o_qkv = nq + 2 * nkv
    o_ki = o_qkv + nqi
    tn = 1024
    assert o_qkv % tn == 0 and nqi % tn == 0
    qkv = norm_mm(h, 0, d, gain, w_in, layer=layer, n=o_qkv, tn=o_qkv // 2, out_dtype=BF16, name="a_qkv_proj",
                  col_scale=_query_scale(nq, o_qkv, A_HEAD_DIM ** -0.5))
    qi = norm_mm(h, 0, d, gain, w_in, layer=layer, n=nqi, col0=o_qkv // tn, tn=tn, out_dtype=BF16,
                 rope=(True,) * (tn // LANE), tables=tables, name="a_idxq_proj")
    w_kw = jnp.zeros((d, 2 * LANE), BF16).at[:, :IDX_DIM + IDX_HEADS].set(w_in[layer, :, o_ki:].astype(BF16))
    kiwi = norm_mm(h, 0, d, gain, w_kw, tn=2 * LANE, out_dtype=F32, rope=(True, False), tables=tables,
                   name="a_idxk_proj")
    mask = dsa_indexer(qi, kiwi, topk=min(IDX_TOPK_MAX, s // 4))
    o = _by_position_layout(consecutive, dsa_attention, qkv, mask, pos, t5_table)
    return mm_residual(o, w_out, h, layer=layer, tn=w_out.shape[-1])


def _mixer_b(h, gain, pos, consecutive, w_in, rel_table, w_out, layer):
    d = h.shape[1]
    n = w_in.shape[-1]
    qkv = norm_mm(h, 0, d, gain, w_in, layer=layer, tn=2048, out_dtype=BF16, name="b_qkv_proj",
                  col_scale=_query_scale(n // 3, n, B_HEAD_DIM ** -0.5))
    o = _by_position_layout(consecutive, band_attention, qkv, pos, rel_table)
    return mm_residual(o, w_out, h, layer=layer, tn=w_out.shape[-1])


def _mixer_c(h, gain, tables, w_down, g_q, g_kv, w_uq, w_ukv, w_out, layer):
    d = h.shape[1]
    lq = g_q.shape[0]
    lkv = g_kv.shape[0]
    n_down = lq + lkv + LANE
    wd = jnp.zeros((d, n_down), BF16).at[:, :w_down.shape[1]].set(w_down.astype(BF16))
    down = norm_mm(h, 0, d, gain, wd, tn=n_down, out_dtype=F32,
                   rope=(False,) * ((lq + lkv) // LANE) + (True,), tables=tables, name="c_down_proj")
    wq = w_uq.astype(BF16).reshape(lq, C_HEADS, C_NOPE + C_ROPE)
    wq = jnp.pad(wq, ((0, 0), (0, 0), (0, 2 * LANE - C_NOPE - C_ROPE))).reshape(lq, C_HEADS * 2 * LANE)
    nqc = wq.shape[1]
    qcat = norm_mm(down, 0, lq, g_q, wq, tn=nqc, out_dtype=BF16, rope=(False, True) * C_HEADS, tables=tables,
                   name="c_uq_proj", col_scale=_query_scale(nqc, nqc, (C_NOPE + C_ROPE) ** -0.5))
    wkv = w_ukv.astype(BF16).reshape(lkv, C_HEADS, 2, C_NOPE).transpose(0, 2, 1, 3).reshape(lkv, -1)
    assert lq == lkv
    kv = norm_mm(down, 1, lkv, g_kv, wkv, tn=wkv.shape[1], out_dtype=BF16, name="c_ukv_proj")
    kr = down[:, lq + lkv:].astype(BF16)
    o = mla_attention(qcat, kv, kr)
    return mm_residual(o, w_out, h, layer=layer, tn=w_out.shape[-1])


def kernel(x, p, positions, t5_table, a_w_in, a_w_out, b_w_in, b_rel_table, b_w_out, c_w_down, c_q_norm,
           c_kv_norm, c_w_uq, c_w_ukv, c_w_out, attn_norm, ffn_norm, ffn_w_in, ffn_w_out, ple_norm,
           ple_w_gate, ple_w_proj, final_norm):
    assert x.shape[0] == 1
    depth = attn_norm.shape[0]
    h = x[0]
    pos = positions[0]
    tables = _rope_tables(pos)
    consecutive = jnp.all(pos[1:] - pos[:-1] == 1)
    a_w_in, a_w_out, b_w_in, b_w_out, c_w_out, ffn_w_in, ffn_w_out, ple_w_gate, ple_w_proj = (
        w.astype(BF16) for w in (a_w_in, a_w_out, b_w_in, b_w_out, c_w_out, ffn_w_in, ffn_w_out, ple_w_gate,
                                 ple_w_proj))
    for i in range(depth):
        j, kind = divmod(i, 3)
        if kind == 0:
            h = _mixer_a(h, attn_norm[i], pos, consecutive, tables, a_w_in, a_w_out, j, t5_table)
        elif kind == 1:
            h = _mixer_b(h, attn_norm[i], pos, consecutive, b_w_in, b_rel_table[j], b_w_out, j)
        else:
            h = _mixer_c(h, attn_norm[i], tables, c_w_down[j], c_q_norm[j], c_kv_norm[j], c_w_uq[j],
                         c_w_ukv[j], c_w_out, j)
        act = ffn_in(h, ffn_norm[i], ffn_w_in, layer=i)
        h = mm_residual(act, ffn_w_out, h, layer=i, tn=1024)
        h = ple(h, ple_norm[i], ple_w_gate, p, ple_w_proj, layer=i)
    return final_rms_norm(h, final_norm)[None]
```

```python
import functools
import math

import jax
import jax.numpy as jnp
from jax import lax
from jax.experimental import pallas as pl
from jax.experimental.pallas import tpu as pltpu

LANE = 128
VMEM_LIMIT_BYTES = 56 * 1024 * 1024

CHUNK = 64
CHUNK_SHIFT = CHUNK.bit_length() - 1
EPS = 1e-6
ROPE_BASE = 10000.0
ROPE_DIM = 64
A_HEADS = 16
A_KV_HEADS = 4
A_GROUP = 4
A_HEAD_DIM = 128
IDX_HEADS = 16
IDX_DIM = 128
IDX_TOPK_MAX = 256
T5_BUCKETS = 32
T5_MAX_DISTANCE = 1024
_T5_EXACT = T5_BUCKETS // 4
_T5_LAST = _T5_EXACT * (T5_MAX_DISTANCE / _T5_EXACT) ** ((_T5_EXACT - 1) / _T5_EXACT)
T5_FAR = math.ceil(_T5_LAST / LANE) * LANE
assert T5_FAR == 640
B_HEADS = 32
B_HEAD_DIM = 64
B_PREV_CHUNKS = 8
B_REL_CLIP = 128
C_HEADS = 16
C_NOPE = 128
C_ROPE = 64
C_V = 128
NEG_BIG = -1e30
INT_MIN = -(2 ** 31)
LOG2E = math.log2(math.e)

F32 = jnp.float32
BF16 = jnp.bfloat16
NT_DIMS = (((1,), (1,)), ((), ()))


def _params(sem):
    return pltpu.CompilerParams(dimension_semantics=sem, vmem_limit_bytes=VMEM_LIMIT_BYTES)


def _rms(x, g):
    ms = jnp.mean(x * x, axis=-1, keepdims=True)
    return (x * lax.rsqrt(ms + EPS)) * g


def _lane_gather(table, idx):
    return jnp.take_along_axis(table, idx, axis=1, mode="promise_in_bounds")


def _rope_group(y, c, s1, s2):
    return y * c + pltpu.roll(y, 96, 1) * s1 + pltpu.roll(y, 32, 1) * s2


def _norm_mm_kernel(*refs, rope, scaled):
    x_ref, g_ref, w_ref = refs[:3]
    rest = list(refs[3:])
    cs_ref = rest.pop(0) if scaled else None
    c_ref, s1_ref, s2_ref = (rest.pop(0), rest.pop(0), rest.pop(0)) if rope is not None else (None,) * 3
    (o_ref,) = rest

    xn = _rms(x_ref[...], g_ref[...]).astype(BF16)
    y = jnp.dot(xn, w_ref[...], preferred_element_type=F32)
    if scaled:
        y = y * cs_ref[...]
    if rope is None:
        o_ref[...] = y.astype(o_ref.dtype)
    else:
        c, s1, s2 = c_ref[...], s1_ref[...], s2_ref[...]
        for gi, on in enumerate(rope):
            sl = slice(gi * LANE, (gi + 1) * LANE)
            yg = y[:, sl]
            if on:
                yg = _rope_group(yg, c, s1, s2)
            o_ref[:, sl] = yg.astype(o_ref.dtype)


def _w_spec(w, layer, k, tn, col0=0):
    if w.ndim == 2:
        return pl.BlockSpec((k, tn), lambda i, j: (0, col0 + j))
    return pl.BlockSpec((None, k, tn), lambda i, j: (layer, 0, col0 + j))


def norm_mm(x, x_col, kx, gain, w, *, tn, out_dtype, name, n=None, layer=None, col0=0, tm=512, rope=None,
            tables=None, col_scale=None):
    s = x.shape[0]
    n = w.shape[-1] if n is None else n
    tm = min(tm, s)
    in_specs = [
        pl.BlockSpec((tm, kx), lambda i, j: (i, x_col)),
        pl.BlockSpec((1, kx), lambda i, j: (0, 0)),
        _w_spec(w, layer, kx, tn, col0),
    ]
    args = [x, gain.reshape(1, kx), w]
    if col_scale is not None:
        in_specs.append(pl.BlockSpec((1, tn), lambda i, j: (0, j)))
        args.append(col_scale.reshape(1, n))
    if rope is not None:
        assert len(rope) == tn // LANE
        in_specs += [pl.BlockSpec((tm, LANE), lambda i, j: (i, 0))] * 3
        args += list(tables)
    return pl.pallas_call(
        functools.partial(_norm_mm_kernel, rope=rope, scaled=col_scale is not None),
        grid=(s // tm, n // tn),
        in_specs=in_specs,
        out_specs=pl.BlockSpec((tm, tn), lambda i, j: (i, j)),
        out_shape=jax.ShapeDtypeStruct((s, n), out_dtype),
        compiler_params=_params(("parallel", "arbitrary")),
        name=name,
    )(*args)


def _mm_res_kernel(x_ref, w_ref, r_ref, o_ref):
    o_ref[...] = r_ref[...] + jnp.dot(x_ref[...], w_ref[...], preferred_element_type=F32)


def mm_residual(x, w, res, *, layer=None, tm=512, tn=512):
    s, k = x.shape
    n = w.shape[-1]
    tm = min(tm, s)
    return pl.pallas_call(
        _mm_res_kernel,
        name="mm_residual",
        grid=(s // tm, n // tn),
        in_specs=[
            pl.BlockSpec((tm, k), lambda i, j: (i, 0)),
            _w_spec(w, layer, k, tn),
            pl.BlockSpec((tm, tn), lambda i, j: (i, j)),
        ],
        out_specs=pl.BlockSpec((tm, tn), lambda i, j: (i, j)),
        out_shape=jax.ShapeDtypeStruct((s, n), F32),
        compiler_params=_params(("parallel", "arbitrary")),
    )(x, w, res)


def _ffn_in_kernel(x_ref, g_ref, wg_ref, wu_ref, o_ref, xn_ref):
    @pl.when(pl.program_id(1) == 0)
    def _():
        xn_ref[...] = _rms(x_ref[...], g_ref[...]).astype(BF16)

    xn = xn_ref[...]
    a = jnp.dot(xn, wg_ref[...], preferred_element_type=F32)
    u = jnp.dot(xn, wu_ref[...], preferred_element_type=F32)
    o_ref[...] = (a * jax.nn.sigmoid(a) * u).astype(o_ref.dtype)


def ffn_in(h, gain, w_in, *, layer=None, tm=1024, tn=512):
    s, d = h.shape
    f = w_in.shape[-1] // 2
    tm = min(tm, s)
    nj = f // tn
    return pl.pallas_call(
        _ffn_in_kernel,
        name="ffn_in",
        grid=(s // tm, nj),
        in_specs=[
            pl.BlockSpec((tm, d), lambda i, j: (i, 0)),
            pl.BlockSpec((1, d), lambda i, j: (0, 0)),
            _w_spec(w_in, layer, d, tn),
            _w_spec(w_in, layer, d, tn, nj),
        ],
        out_specs=pl.BlockSpec((tm, tn), lambda i, j: (i, j)),
        out_shape=jax.ShapeDtypeStruct((s, f), BF16),
        scratch_shapes=[pltpu.VMEM((tm, d), BF16)],
        compiler_params=_params(("parallel", "arbitrary")),
    )(h, gain.reshape(1, d), w_in, w_in)


def _ple_kernel(x_ref, g_ref, wg_ref, p_ref, wp_ref, h_ref, o_ref):
    xn = _rms(x_ref[...], g_ref[...]).astype(BF16)
    gate = jax.nn.sigmoid(jnp.dot(xn, wg_ref[...], preferred_element_type=F32))
    proj = jnp.dot(p_ref[...].astype(BF16), wp_ref[...], preferred_element_type=F32)
    o_ref[...] = h_ref[...] + gate * proj


def ple(h, gain, w_gate, p, w_proj, *, layer, tm=512, tn=2048):
    s, d = h.shape
    pd = p.shape[-1]
    tm = min(tm, s)
    return pl.pallas_call(
        _ple_kernel,
        name="ple",
        grid=(s // tm, d // tn),
        in_specs=[
            pl.BlockSpec((tm, d), lambda i, j: (i, 0)),
            pl.BlockSpec((1, d), lambda i, j: (0, 0)),
            _w_spec(w_gate, layer, d, tn),
            pl.BlockSpec((None, None, tm, pd), lambda i, j: (layer, 0, i, 0)),
            _w_spec(w_proj, layer, pd, tn),
            pl.BlockSpec((tm, tn), lambda i, j: (i, j)),
        ],
        out_specs=pl.BlockSpec((tm, tn), lambda i, j: (i, j)),
        out_shape=jax.ShapeDtypeStruct((s, d), F32),
        compiler_params=_params(("parallel", "arbitrary")),
    )(h, gain.reshape(1, d), w_gate, p, w_proj, h)


def _final_norm_kernel(x_ref, g_ref, o_ref):
    o_ref[...] = _rms(x_ref[...], g_ref[...])


def final_rms_norm(h, gain, *, tm=512):
    s, d = h.shape
    tm = min(tm, s)
    return pl.pallas_call(
        _final_norm_kernel,
        name="final_norm",
        grid=(s // tm,),
        in_specs=[pl.BlockSpec((tm, d), lambda i: (i, 0)), pl.BlockSpec((1, d), lambda i: (0, 0))],
        out_specs=pl.BlockSpec((tm, d), lambda i: (i, 0)),
        out_shape=jax.ShapeDtypeStruct((s, d), F32),
        compiler_params=_params(("parallel",)),
    )(h, gain.reshape(1, d))


IDX_TQ = 256
IDX_KC = 256


def _orderable(x):
    b = pltpu.bitcast(x, jnp.int32)
    return jnp.where(b < 0, b ^ jnp.int32(0x7FFFFFFF), b)


def _indexer_kernel(qi_ref, ki_ref, wi_ref, mask_ref, keys_ref, *, topk, n_kblocks):
    tq, kc = IDX_TQ, IDX_KC
    i = pl.program_id(0)
    nkc = (i + 1) * (tq // kc)
    n_idx_bits = (n_kblocks * kc - 1).bit_length()
    w_t = (wi_ref[...] * (IDX_HEADS ** -0.5 * IDX_DIM ** -0.5)).T
    qchunk = (i * tq + lax.broadcasted_iota(jnp.int32, (kc, tq), 1)) >> CHUNK_SHIFT
    krow = lax.broadcasted_iota(jnp.int32, (kc, tq), 0)

    def valid_of(c):
        return ((c * kc + krow) >> CHUNK_SHIFT) <= qchunk

    def score_chunk(c):
        k0 = pl.multiple_of(c * kc, kc)
        kblk = ki_ref[pl.ds(k0, kc), :].astype(BF16)
        acc = jnp.zeros((kc, tq), F32)
        for h in range(IDX_HEADS):
            s = lax.dot_general(kblk, qi_ref[:, h * IDX_DIM:(h + 1) * IDX_DIM], NT_DIMS,
                                preferred_element_type=F32)
            acc = acc + jnp.maximum(s, 0.0) * w_t[h:h + 1, :]
        keys_ref[pl.ds(k0, kc), :] = jnp.where(valid_of(c), _orderable(acc), jnp.int32(INT_MIN))

    _for_tiles_in_pairs(0, nkc, score_chunk, per_trip=(4, 2, 1))

    @pl.when(nkc % 2 == 1)
    def _():
        keys_ref[pl.ds(pl.multiple_of(nkc * kc, kc), kc), :] = jnp.full((kc, tq), INT_MIN, jnp.int32)

    def count(pred_of):
        nacc = 4

        def body(cp, cnt):
            for c in (2 * cp, 2 * cp + 1):
                k0 = pl.multiple_of(c * kc, kc)
                hit = pred_of(c, keys_ref[pl.ds(k0, kc), :]).astype(jnp.int32)
                cnt = cnt + hit.reshape(kc // (8 * nacc), nacc, 8, tq).sum(axis=0)
            return cnt
        cnt = lax.fori_loop(0, (nkc + 1) // 2, body, jnp.zeros((nacc, 8, tq), jnp.int32))
        return cnt.sum(axis=0).sum(axis=0, keepdims=True)

    def bit_step(b, state):
        thr, cnt = state
        cand = thr + lax.shift_left(jnp.int32(1), 31 - b)
        c_cand = count(lambda c, keys: keys >= cand)
        take = c_cand >= topk
        return jnp.where(take, cand, thr), jnp.where(take, c_cand, cnt)

    thr, cnt = lax.fori_loop(
        0, 32, bit_step,
        (jnp.full((1, tq), INT_MIN, jnp.int32), jnp.full((1, tq), nkc * kc, jnp.int32)))

    def tie_limit():
        need = topk - count(lambda c, keys: keys > thr)
        def bit_body(b, x):
            cand = x + lax.shift_left(jnp.int32(1), n_idx_bits - 1 - b)
            below = count(lambda c, keys: (keys == thr) & ((c * kc + krow) < cand))
            return jnp.where(below < need, cand, x)
        return lax.fori_loop(0, n_idx_bits, bit_body, jnp.zeros((1, tq), jnp.int32))

    def write_masks(selected):
        def mask_body(c, carry):
            k0 = pl.multiple_of(c * kc, kc)
            sel = selected(c, keys_ref[pl.ds(k0, kc), :]) & valid_of(c)
            mask_ref[c] = jnp.where(sel, 0.0, -jnp.inf).astype(F32).T
            return carry
        lax.fori_loop(0, nkc, mask_body, 0)

    has_ties = jnp.max(jnp.where((cnt > topk) & (thr > INT_MIN), 1, 0)) > 0

    @pl.when(has_ties)
    def _():
        tie_last = tie_limit()
        write_masks(lambda c, keys: (keys > thr) | ((keys == thr) & ((c * kc + krow) <= tie_last)))

    @pl.when(jnp.logical_not(has_ties))
    def _():
        write_masks(lambda c, keys: keys >= thr)

    def fill_body(c, carry):
        mask_ref[c] = jnp.full((tq, kc), -jnp.inf, F32)
        return carry

    lax.fori_loop(nkc, n_kblocks, fill_body, 0)


def dsa_indexer(qi, kiwi, *, topk):
    s = qi.shape[0]
    tq, kc = IDX_TQ, IDX_KC
    nkb = s // kc
    assert s % tq == 0 and nkb % 2 == 0
    return pl.pallas_call(
        functools.partial(_indexer_kernel, topk=topk, n_kblocks=nkb),
        name="dsa_indexer",
        grid=(s // tq,),
        in_specs=[
            pl.BlockSpec((tq, IDX_HEADS * IDX_DIM), lambda i: (i, 0)),
            pl.BlockSpec((s, IDX_DIM), lambda i: (0, 0)),
            pl.BlockSpec((tq, LANE), lambda i: (i, 1)),
        ],
        out_specs=pl.BlockSpec((nkb, tq, kc), lambda i: (0, i, 0)),
        out_shape=jax.ShapeDtypeStruct((nkb, s, kc), F32),
        scratch_shapes=[pltpu.VMEM((s, tq), jnp.int32)],
        compiler_params=_params(("parallel",)),
    )(qi, kiwi, kiwi)


ATT_A_TQ = 256
ATT_A_KB = 2
ATT_A_NEAR = (T5_FAR + ATT_A_KB * IDX_KC - 2) // IDX_KC + 1


def _t5_bucket(rel):
    nb = T5_BUCKETS // 2
    max_exact = nb // 2
    ret = jnp.where(rel > 0, nb, 0)
    n = jnp.abs(rel)
    nf = jnp.maximum(n, 1).astype(F32)
    large = max_exact + (jnp.log(nf / max_exact) / math.log(T5_MAX_DISTANCE / max_exact)
                         * (nb - max_exact)).astype(jnp.int32)
    large = jnp.minimum(large, nb - 1)
    return ret + jnp.where(n < max_exact, n, large)


def _softmax_step(s, v, m_ref, l_ref, acc_ref, rows, tk, shift=None):
    d = v.shape[1]
    m_prev = m_ref[rows, :]
    mx = jnp.max(s, axis=1, keepdims=True)
    if shift is not None:
        mx = mx + shift
    m_next = jnp.maximum(m_prev, mx)
    pivot = m_next if shift is None else m_next - shift
    p = jnp.exp2(s - jnp.concatenate([pivot] * (tk // LANE), axis=1)).astype(BF16)
    alpha = jnp.exp2(m_prev - m_next)
    v_ones = jnp.concatenate([v, jnp.ones((tk, LANE), BF16)], axis=1)
    pv = jnp.dot(p, v_ones, preferred_element_type=F32)
    m_ref[rows, :] = m_next
    l_ref[rows, :] = alpha * l_ref[rows, :] + pv[:, d:]
    acc_ref[rows, :] = acc_ref[rows, :] * alpha + pv[:, :d]


def _for_tiles_in_pairs(lo, hi, tile_fn, per_trip=(2, 1)):
    assert per_trip[-1] == 1
    start = lo
    for size in per_trip:
        def group(kp, carry, size=size, start=start):
            for u in range(size):
                tile_fn(start + size * kp + u)
            return carry

        n = (hi - start) // size
        lax.fori_loop(0, n, group, 0)
        start = start + n * size


def _attn_a_kernel(qmin_ref, kmax_ref, t5s_ref, q_ref, k_ref, v_ref, mask_ref, posq_ref, posk_ref,
                   t5t_ref, o_ref, m_ref, l_ref, acc_ref, nbias_ref, *, consecutive):
    tq, kc, nb = ATT_A_TQ, IDX_KC, ATT_A_KB
    tk = nb * kc
    g = pl.program_id(0)
    i = pl.program_id(1)

    def head_bias(r, bucket):
        row = t5t_ref[pl.ds(g * A_GROUP + r, 1), :] * LOG2E
        tbl = jnp.broadcast_to(row, (tq, LANE))
        return jnp.concatenate([_lane_gather(tbl, bucket[:, c * LANE:(c + 1) * LANE])
                                for c in range(tk // LANE)], axis=1)

    if consecutive:
        @pl.when(i == 0)
        def _():
            rel0 = (lax.broadcasted_iota(jnp.int32, (tq, tk), 1)
                    - lax.broadcasted_iota(jnp.int32, (tq, tk), 0))
            for d in range(ATT_A_NEAR):
                bucket = _t5_bucket(rel0 + (d - ATT_A_NEAR + 1) * kc)
                for r in range(A_GROUP):
                    nbias_ref[d, r] = head_bias(r, bucket)

    m_ref[...] = jnp.full(m_ref.shape, NEG_BIG, F32)
    l_ref[...] = jnp.zeros(l_ref.shape, F32)
    acc_ref[...] = jnp.zeros(acc_ref.shape, F32)
    nkt = ((i + 1) * (tq // kc) + nb - 1) // nb
    qmin = qmin_ref[i]

    def is_far(kt):
        ktc = jnp.minimum(kt, nkt - 1) * nb
        kmax = kmax_ref[ktc]
        for b in range(1, nb):
            kmax = jnp.maximum(kmax, kmax_ref[ktc + b])
        return (kt < nkt) & ((qmin - kmax) >= T5_FAR)

    n_far = lax.while_loop(is_far, lambda kt: kt + 1, jnp.int32(0))

    def tile(kt, bias_of, shift_of):
        k0 = pl.multiple_of(kt * tk, tk)
        k_t = k_ref[pl.ds(k0, tk), :]
        v_t = v_ref[pl.ds(k0, tk), :]
        base = jnp.concatenate([mask_ref[kt * nb + b] for b in range(nb)], axis=1)
        for r in range(A_GROUP):
            rows = slice(r * tq, (r + 1) * tq)
            q_r = q_ref[:, r * A_HEAD_DIM:(r + 1) * A_HEAD_DIM]
            s = lax.dot_general(q_r, k_t, NT_DIMS, preferred_element_type=F32) + bias_of(r, base)
            _softmax_step(s, v_t, m_ref, l_ref, acc_ref, rows, tk, shift_of(r))

    def far_tile(kt):
        tile(kt, lambda r, base: base,
             lambda r: t5s_ref[T5_BUCKETS // 2 - 1, g * A_GROUP + r] * LOG2E)

    def near_tile(kt):
        if consecutive:
            d = kt * nb - i * (tq // kc) + (ATT_A_NEAR - 1)
            tile(kt, lambda r, base: base + nbias_ref[d, r], lambda r: None)
        else:
            pk = jnp.concatenate([posk_ref[kt * nb + b] for b in range(nb)], axis=1)
            bucket = _t5_bucket(pk - posq_ref[...])
            tile(kt, lambda r, base: base + head_bias(r, bucket), lambda r: None)

    _for_tiles_in_pairs(0, n_far, far_tile, per_trip=(4, 2, 1))
    _for_tiles_in_pairs(n_far, nkt, near_tile)
    for r in range(A_GROUP):
        rows = slice(r * tq, (r + 1) * tq)
        o_ref[:, r * A_HEAD_DIM:(r + 1) * A_HEAD_DIM] = (acc_ref[rows, :] / l_ref[rows, :]).astype(o_ref.dtype)


def dsa_attention(qkv, mask, pos, t5_table, *, consecutive):
    s = qkv.shape[0]
    tq, tk = ATT_A_TQ, IDX_KC
    assert (s // tk) % ATT_A_KB == 0
    gw = A_GROUP * A_HEAD_DIM
    qmin = pos.reshape(s // tq, tq).min(axis=1)
    kmax = pos.reshape(s // tk, tk).max(axis=1)
    t5t = jnp.zeros((A_HEADS, LANE), F32).at[:, :T5_BUCKETS].set(t5_table.T)
    kblk0 = (A_HEADS * A_HEAD_DIM) // A_HEAD_DIM
    smem = pl.BlockSpec(memory_space=pltpu.SMEM)
    nbias_shape = (ATT_A_NEAR, A_GROUP, tq, ATT_A_KB * tk) if consecutive else (1, 1, 8, LANE)
    return pl.pallas_call(
        functools.partial(_attn_a_kernel, consecutive=consecutive),
        name="dsa_attention",
        grid=(A_KV_HEADS, s // tq),
        in_specs=[
            smem, smem, smem,
            pl.BlockSpec((tq, gw), lambda g, i: (i, g)),
            pl.BlockSpec((s, A_HEAD_DIM), lambda g, i: (0, kblk0 + g)),
            pl.BlockSpec((s, A_HEAD_DIM), lambda g, i: (0, kblk0 + A_KV_HEADS + g)),
            pl.BlockSpec((s // tk, tq, tk), lambda g, i: (0, i, 0)),
            pl.BlockSpec((tq, 1), lambda g, i: (i, 0)),
            pl.BlockSpec((s // tk, 1, tk), lambda g, i: (0, 0, 0)),
            pl.BlockSpec((A_HEADS, LANE), lambda g, i: (0, 0)),
        ],
        out_specs=pl.BlockSpec((tq, gw), lambda g, i: (i, g)),
        out_shape=jax.ShapeDtypeStruct((s, A_HEADS * A_HEAD_DIM), BF16),
        scratch_shapes=[
            pltpu.VMEM((A_GROUP * tq, LANE), F32),
            pltpu.VMEM((A_GROUP * tq, LANE), F32),
            pltpu.VMEM((A_GROUP * tq, A_HEAD_DIM), F32),
            pltpu.VMEM(nbias_shape, F32),
        ],
        compiler_params=_params(("parallel", "arbitrary")),
    )(qmin, kmax, t5_table, qkv, qkv, qkv, mask, pos.reshape(s, 1), pos.reshape(s // tk, 1, tk), t5t)


ATT_B_TQ = 256
ATT_B_NKB = B_PREV_CHUNKS * CHUNK // ATT_B_TQ + 1
ATT_B_LG = 4


def _attn_b_kernel(t256_ref, q_ref, k0_ref, k1_ref, k2_ref, v0_ref, v1_ref, v2_ref, posq_ref, posk_ref,
                   relt_ref, o_ref, bias_ref, *, consecutive):
    tq = ATT_B_TQ
    nh = LANE // B_HEAD_DIM
    gp = pl.program_id(0)
    i = pl.program_id(1)
    k_refs = (k0_ref, k1_ref, k2_ref)
    v_refs = (v0_ref, v1_ref, v2_ref)

    def build_bias():
        row_i = lax.broadcasted_iota(jnp.int32, (tq, tq), 0)
        col_i = lax.broadcasted_iota(jnp.int32, (tq, tq), 1)
        for j in range(ATT_B_NKB):
            back = (ATT_B_NKB - 1 - j) * tq
            if consecutive:
                rel = row_i - (col_i - back)
            else:
                rel = posq_ref[...] - posk_ref[jnp.maximum(i - (ATT_B_NKB - 1) + j, 0)]
            r = jnp.clip(rel, -B_REL_CLIP, B_REL_CLIP) + B_REL_CLIP
            dchunk = (row_i >> CHUNK_SHIFT) - ((col_i - back) >> CHUNK_SHIFT)
            band = (dchunk >= 0) & (dchunk <= B_PREV_CHUNKS)
            for hl in range(ATT_B_LG * nh):
                h = gp * (ATT_B_LG * nh) + hl
                row = relt_ref[pl.ds(h, 1), :] * LOG2E
                seg0 = jnp.broadcast_to(row[:, :LANE], (tq, LANE))
                seg1 = jnp.broadcast_to(row[:, LANE:2 * LANE], (tq, LANE))
                t256 = t256_ref[h] * LOG2E
                for c in range(tq // LANE):
                    cs = slice(c * LANE, (c + 1) * LANE)
                    rc = r[:, cs]
                    lo = rc & (LANE - 1)
                    bias = jnp.where(rc < LANE, _lane_gather(seg0, lo),
                                     jnp.where(rc < 2 * LANE, _lane_gather(seg1, lo), t256))
                    bias_ref[hl, :, j * tq + c * LANE:j * tq + (c + 1) * LANE] = jnp.where(band[:, cs], bias, -jnp.inf)

    if consecutive:
        pl.when(i == 0)(build_bias)
    else:
        build_bias()

    lane_head = lax.broadcasted_iota(jnp.int32, (1, LANE), 1) // B_HEAD_DIM
    for lg in range(ATT_B_LG):
        lanes = slice(lg * LANE, (lg + 1) * LANE)
        qp = q_ref[:, lanes]
        out = jnp.zeros((tq, LANE), F32)
        for hh in range(nh):
            mine = lane_head == hh
            qm = jnp.where(mine, qp, jnp.zeros_like(qp))
            parts = []
            for j in range(ATT_B_NKB):
                sj = lax.dot_general(qm, k_refs[j][:, lanes], NT_DIMS, preferred_element_type=F32)
                sj = sj + bias_ref[lg * nh + hh, :, j * tq:(j + 1) * tq]
                if j < ATT_B_NKB - 1:
                    sj = sj + jnp.where(i - (ATT_B_NKB - 1) + j >= 0, 0.0, -jnp.inf)
                parts.append(sj)
            s = jnp.concatenate(parts, axis=1)
            p = jnp.exp2(s - jnp.max(s, axis=1, keepdims=True)).astype(BF16)
            acc = jnp.zeros((tq, LANE), F32)
            for j in range(ATT_B_NKB):
                vj = v_refs[j][:, lanes]
                vm = jnp.where(mine, vj, jnp.ones_like(vj))
                acc = acc + jnp.dot(p[:, j * tq:(j + 1) * tq], vm, preferred_element_type=F32)
            rowsum = pltpu.roll(acc, B_HEAD_DIM, 1)
            out = out + jnp.where(mine, acc / rowsum, 0.0)
        o_ref[:, lanes] = out.astype(o_ref.dtype)


def band_attention(qkv, pos, rel_table, *, consecutive):
    s = qkv.shape[0]
    tq = ATT_B_TQ
    hw = B_HEADS * B_HEAD_DIM
    bw = ATT_B_LG * LANE
    ngrp = hw // bw
    assert LANE // B_HEAD_DIM == 2 and ATT_B_NKB == 3
    nrel = 2 * B_REL_CLIP + 1
    relt = jnp.zeros((B_HEADS, 3 * LANE), F32).at[:, :nrel].set(rel_table.T)
    t256 = rel_table[nrel - 1]

    def kv_spec(j, base):
        return pl.BlockSpec((tq, bw), lambda gp, i: (jnp.maximum(i - (ATT_B_NKB - 1) + j, 0), base + gp))

    return pl.pallas_call(
        functools.partial(_attn_b_kernel, consecutive=consecutive),
        name="band_attention",
        grid=(ngrp, s // tq),
        in_specs=[
            pl.BlockSpec(memory_space=pltpu.SMEM),
            pl.BlockSpec((tq, bw), lambda gp, i: (i, gp)),
            kv_spec(0, ngrp), kv_spec(1, ngrp), kv_spec(2, ngrp),
            kv_spec(0, 2 * ngrp), kv_spec(1, 2 * ngrp), kv_spec(2, 2 * ngrp),
            pl.BlockSpec((tq, 1), lambda gp, i: (i, 0)),
            pl.BlockSpec((s // tq, 1, tq), lambda gp, i: (0, 0, 0)),
            pl.BlockSpec((B_HEADS, 3 * LANE), lambda gp, i: (0, 0)),
        ],
        out_specs=pl.BlockSpec((tq, bw), lambda gp, i: (i, gp)),
        out_shape=jax.ShapeDtypeStruct((s, hw), BF16),
        scratch_shapes=[pltpu.VMEM((ATT_B_LG * (LANE // B_HEAD_DIM), tq, ATT_B_NKB * tq), F32)],
        compiler_params=_params(("parallel", "arbitrary")),
    )(t256, qkv, qkv, qkv, qkv, qkv, qkv, qkv, pos.reshape(s, 1), pos.reshape(s // tq, 1, tq), relt)


ATT_C_T = 512
ATT_C_HEADS = 4


def _attn_c_kernel(q_ref, kn_ref, kr_ref, v_ref, o_ref, m_ref, l_ref, acc_ref):
    t, nh = ATT_C_T, ATT_C_HEADS
    iq = pl.program_id(1)
    m_ref[...] = jnp.full(m_ref.shape, NEG_BIG, F32)
    l_ref[...] = jnp.zeros(l_ref.shape, F32)
    acc_ref[...] = jnp.zeros(acc_ref.shape, F32)

    def tile(kt, diag):
        k0 = pl.multiple_of(kt * t, t)
        kr_t = kr_ref[pl.ds(k0, t), :]
        for hh in range(nh):
            kcat = jnp.concatenate([kn_ref[pl.ds(k0, t), hh * C_NOPE:(hh + 1) * C_NOPE], kr_t], axis=1)
            s = lax.dot_general(q_ref[:, hh * 2 * LANE:(hh + 1) * 2 * LANE], kcat, NT_DIMS,
                                preferred_element_type=F32)
            if diag:
                qc = lax.broadcasted_iota(jnp.int32, (t, t), 0) >> CHUNK_SHIFT
                kc = lax.broadcasted_iota(jnp.int32, (t, t), 1) >> CHUNK_SHIFT
                s = jnp.where(kc <= qc, s, -jnp.inf)
            _softmax_step(s, v_ref[pl.ds(k0, t), hh * C_V:(hh + 1) * C_V], m_ref, l_ref, acc_ref,
                          slice(hh * t, (hh + 1) * t), t)

    _for_tiles_in_pairs(0, iq, lambda kt: tile(kt, False), per_trip=(4, 2, 1))
    tile(iq, True)
    for hh in range(nh):
        rows = slice(hh * t, (hh + 1) * t)
        o_ref[:, hh * C_V:(hh + 1) * C_V] = (acc_ref[rows, :] / l_ref[rows, :]).astype(o_ref.dtype)


def mla_attention(qcat, kv, kr):
    s = qcat.shape[0]
    t, nh = ATT_C_T, ATT_C_HEADS
    assert s % t == 0
    return pl.pallas_call(
        _attn_c_kernel,
        name="mla_attention",
        grid=(C_HEADS // nh, s // t),
        in_specs=[
            pl.BlockSpec((t, nh * 2 * LANE), lambda h, i: (i, h)),
            pl.BlockSpec((s, nh * C_NOPE), lambda h, i: (0, h)),
            pl.BlockSpec((s, LANE), lambda h, i: (0, 0)),
            pl.BlockSpec((s, nh * C_V), lambda h, i: (0, C_HEADS // nh + h)),
        ],
        out_specs=pl.BlockSpec((t, nh * C_V), lambda h, i: (i, h)),
        out_shape=jax.ShapeDtypeStruct((s, C_HEADS * C_V), BF16),
        scratch_shapes=[pltpu.VMEM((nh * t, LANE), F32), pltpu.VMEM((nh * t, LANE), F32),
                        pltpu.VMEM((nh * t, C_V), F32)],
        compiler_params=_params(("parallel", "arbitrary")),
    )(qcat, kv, kr, kv)


def _rope_tables(pos):
    half = ROPE_DIM // 2
    inv = ROPE_BASE ** (-jnp.arange(half, dtype=F32) * 2.0 / ROPE_DIM)
    ang = pos.astype(F32)[:, None] * inv
    cos, sin = jnp.cos(ang), jnp.sin(ang)
    s = pos.shape[0]
    z = jnp.zeros((s, half), F32)
    c = jnp.concatenate([cos, cos, jnp.ones((s, LANE - ROPE_DIM), F32)], axis=1)
    s1 = jnp.concatenate([-sin, z, z, z], axis=1)
    s2 = jnp.concatenate([z, sin, z, z], axis=1)
    return c, s1, s2


def _query_scale(n_query, n_total, scale):
    return jnp.concatenate([jnp.full((n_query,), scale * LOG2E, F32), jnp.ones((n_total - n_query,), F32)])


def _by_position_layout(consecutive, fn, *args):
    return lax.cond(consecutive, functools.partial(fn, consecutive=True),
                    functools.partial(fn, consecutive=False), *args)


def _mixer_a(h, gain, pos, consecutive, tables, w_in, w_out, layer, t5_table):
    s = h.shape[0]
    d = h.shape[1]
    nq = A_HEADS * A_HEAD_DIM
    nkv = A_KV_HEADS * A_HEAD_DIM
    nqi = IDX_HEADS * IDX_DIM
    o_qkv = nq + 2 * nkv
    o_ki = o_qkv + nqi
    tn = 1024
    assert o_qkv % tn == 0 and nqi % tn == 0
    qkv = norm_mm(h, 0, d, gain, w_in, layer=layer, n=o_qkv, tn=o_qkv // 2, out_dtype=BF16, name="a_qkv_proj",
                  col_scale=_query_scale(nq, o_qkv, A_HEAD_DIM ** -0.5))
    qi = norm_mm(h, 0, d, gain, w_in, layer=layer, n=nqi, col0=o_qkv // tn, tn=tn, out_dtype=BF16,
                 rope=(True,) * (tn // LANE), tables=tables, name="a_idxq_proj")
    w_kw = jnp.zeros((d, 2 * LANE), BF16).at[:, :IDX_DIM + IDX_HEADS].set(w_in[layer, :, o_ki:].astype(BF16))
    kiwi = norm_mm(h, 0, d, gain, w_kw, tn=2 * LANE, out_dtype=F32, rope=(True, False), tables=tables,
                   name="a_idxk_proj")
    mask = dsa_indexer(qi, kiwi, topk=min(IDX_TOPK_MAX, s // 4))
    o = _by_position_layout(consecutive, dsa_attention, qkv, mask, pos, t5_table)
    return mm_residual(o, w_out, h, layer=layer, tn=w_out.shape[-1])


def _mixer_b(h, gain, pos, consecutive, w_in, rel_table, w_out, layer):
    d = h.shape[1]
    n = w_in.shape[-1]
    qkv = norm_mm(h, 0, d, gain, w_in, layer=layer, tn=2048, out_dtype=BF16, name="b_qkv_proj",
                  col_scale=_query_scale(n // 3, n, B_HEAD_DIM ** -0.5))
    o = _by_position_layout(consecutive, band_attention, qkv, pos, rel_table)
    return mm_residual(o, w_out, h, layer=layer, tn=w_out.shape[-1])


def _mixer_c(h, gain, tables, w_down, g_q, g_kv, w_uq, w_ukv, w_out, layer):
    d = h.shape[1]
    lq = g_q.shape[0]
    lkv = g_kv.shape[0]
    n_down = lq + lkv + LANE
    wd = jnp.zeros((d, n_down), BF16).at[:, :w_down.shape[1]].set(w_down.astype(BF16))
    down = norm_mm(h, 0, d, gain, wd, tn=n_down, out_dtype=F32,
                   rope=(False,) * ((lq + lkv) // LANE) + (True,), tables=tables, name="c_down_proj")
    wq = w_uq.astype(BF16).reshape(lq, C_HEADS, C_NOPE + C_ROPE)
    wq = jnp.pad(wq, ((0, 0), (0, 0), (0, 2 * LANE - C_NOPE - C_ROPE))).reshape(lq, C_HEADS * 2 * LANE)
    nqc = wq.shape[1]
    qcat = norm_mm(down, 0, lq, g_q, wq, tn=nqc, out_dtype=BF16, rope=(False, True) * C_HEADS, tables=tables,
                   name="c_uq_proj", col_scale=_query_scale(nqc, nqc, (C_NOPE + C_ROPE) ** -0.5))
    wkv = w_ukv.astype(BF16).reshape(lkv, C_HEADS, 2, C_NOPE).transpose(0, 2, 1, 3).reshape(lkv, -1)
    assert lq == lkv
    kv = norm_mm(down, 1, lkv, g_kv, wkv, tn=wkv.shape[1], out_dtype=BF16, name="c_ukv_proj")
    kr = down[:, lq + lkv:].astype(BF16)
    o = mla_attention(qcat, kv, kr)
    return mm_residual(o, w_out, h, layer=layer, tn=w_out.shape[-1])


def kernel(x, p, positions, t5_table, a_w_in, a_w_out, b_w_in, b_rel_table, b_w_out, c_w_down, c_q_norm,
           c_kv_norm, c_w_uq, c_w_ukv, c_w_out, attn_norm, ffn_norm, ffn_w_in, ffn_w_out, ple_norm,
           ple_w_gate, ple_w_proj, final_norm):
    assert x.shape[0] == 1
    depth = attn_norm.shape[0]
    h = x[0]
    pos = positions[0]
    tables = _rope_tables(pos)
    consecutive = jnp.all(pos[1:] - pos[:-1] == 1)
    a_w_in, a_w_out, b_w_in, b_w_out, c_w_out, ffn_w_in, ffn_w_out, ple_w_gate, ple_w_proj = (
        w.astype(BF16) for w in (a_w_in, a_w_out, b_w_in, b_w_out, c_w_out, ffn_w_in, ffn_w_out, ple_w_gate,
                                 ple_w_proj))
    for i in range(depth):
        j, kind = divmod(i, 3)
        if kind == 0:
            h = _mixer_a(h, attn_norm[i], pos, consecutive, tables, a_w_in, a_w_out, j, t5_table)
        elif kind == 1:
            h = _mixer_b(h, attn_norm[i], pos, consecutive, b_w_in, b_rel_table[j], b_w_out, j)
        else:
            h = _mixer_c(h, attn_norm[i], tables, c_w_down[j], c_q_norm[j], c_kv_norm[j], c_w_uq[j],
                         c_w_ukv[j], c_w_out, j)
        act = ffn_in(h, ffn_norm[i], ffn_w_in, layer=i)
        h = mm_residual(act, ffn_w_out, h, layer=i, tn=1024)
        h = ple(h, ple_norm[i], ple_w_gate, p, ple_w_proj, layer=i)
    return final_rms_norm(h, final_norm)[None]
```

```python
import functools
import math

import jax
import jax.numpy as jnp
from jax import lax
from jax.experimental import pallas as pl
from jax.experimental.pallas import tpu as pltpu

LANE = 128
VMEM_LIMIT_BYTES = 56 * 1024 * 1024

CHUNK = 64
CHUNK_SHIFT = CHUNK.bit_length() - 1
EPS = 1e-6
ROPE_BASE = 10000.0
ROPE_DIM = 64
A_HEADS = 16
A_KV_HEADS = 4
A_GROUP = 4
A_HEAD_DIM = 128
IDX_HEADS = 16
IDX_DIM = 128
IDX_TOPK_MAX = 256
T5_BUCKETS = 32
T5_MAX_DISTANCE = 1024
_T5_EXACT = T5_BUCKETS // 4
_T5_LAST = _T5_EXACT * (T5_MAX_DISTANCE / _T5_EXACT) ** ((_T5_EXACT - 1) / _T5_EXACT)
T5_FAR = math.ceil(_T5_LAST / LANE) * LANE
assert T5_FAR == 640
B_HEADS = 32
B_HEAD_DIM = 64
B_PREV_CHUNKS = 8
B_REL_CLIP = 128
C_HEADS = 16
C_NOPE = 128
C_ROPE = 64
C_V = 128
NEG_BIG = -1e30
INT_MIN = -(2 ** 31)
LOG2E = math.log2(math.e)

F32 = jnp.float32
BF16 = jnp.bfloat16
NT_DIMS = (((1,), (1,)), ((), ()))


def _params(sem):
    return pltpu.CompilerParams(dimension_semantics=sem, vmem_limit_bytes=VMEM_LIMIT_BYTES)


def _rms(x, g):
    ms = jnp.mean(x * x, axis=-1, keepdims=True)
    return (x * lax.rsqrt(ms + EPS)) * g


def _lane_gather(table, idx):
    return jnp.take_along_axis(table, idx, axis=1, mode="promise_in_bounds")


def _rope_group(y, c, s1, s2):
    return y * c + pltpu.roll(y, 96, 1) * s1 + pltpu.roll(y, 32, 1) * s2


def _norm_mm_kernel(*refs, rope, scaled):
    x_ref, g_ref, w_ref = refs[:3]
    rest = list(refs[3:])
    cs_ref = rest.pop(0) if scaled else None
    c_ref, s1_ref, s2_ref = (rest.pop(0), rest.pop(0), rest.pop(0)) if rope is not None else (None,) * 3
    (o_ref,) = rest

    xn = _rms(x_ref[...], g_ref[...]).astype(BF16)
    y = jnp.dot(xn, w_ref[...], preferred_element_type=F32)
    if scaled:
        y = y * cs_ref[...]
    if rope is None:
        o_ref[...] = y.astype(o_ref.dtype)
    else:
        c, s1, s2 = c_ref[...], s1_ref[...], s2_ref[...]
        for gi, on in enumerate(rope):
            sl = slice(gi * LANE, (gi + 1) * LANE)
            yg = y[:, sl]
            if on:
                yg = _rope_group(yg, c, s1, s2)
            o_ref[:, sl] = yg.astype(o_ref.dtype)


def _w_spec(w, layer, k, tn, col0=0):
    if w.ndim == 2:
        return pl.BlockSpec((k, tn), lambda i, j: (0, col0 + j))
    return pl.BlockSpec((None, k, tn), lambda i, j: (layer, 0, col0 + j))


def norm_mm(x, x_col, kx, gain, w, *, tn, out_dtype, name, n=None, layer=None, col0=0, tm=512, rope=None,
            tables=None, col_scale=None):
    s = x.shape[0]
    n = w.shape[-1] if n is None else n
    tm = min(tm, s)
    in_specs = [
        pl.BlockSpec((tm, kx), lambda i, j: (i, x_col)),
        pl.BlockSpec((1, kx), lambda i, j: (0, 0)),
        _w_spec(w, layer, kx, tn, col0),
    ]
    args = [x, gain.reshape(1, kx), w]
    if col_scale is not None:
        in_specs.append(pl.BlockSpec((1, tn), lambda i, j: (0, j)))
        args.append(col_scale.reshape(1, n))
    if rope is not None:
        assert len(rope) == tn // LANE
        in_specs += [pl.BlockSpec((tm, LANE), lambda i, j: (i, 0))] * 3
        args += list(tables)
    return pl.pallas_call(
        functools.partial(_norm_mm_kernel, rope=rope, scaled=col_scale is not None),
        grid=(s // tm, n // tn),
        in_specs=in_specs,
        out_specs=pl.BlockSpec((tm, tn), lambda i, j: (i, j)),
        out_shape=jax.ShapeDtypeStruct((s, n), out_dtype),
        compiler_params=_params(("parallel", "arbitrary")),
        name=name,
    )(*args)


def _mm_res_kernel(x_ref, w_ref, r_ref, o_ref):
    o_ref[...] = r_ref[...] + jnp.dot(x_ref[...], w_ref[...], preferred_element_type=F32)


def mm_residual(x, w, res, *, layer=None, tm=512, tn=512):
    s, k = x.shape
    n = w.shape[-1]
    tm = min(tm, s)
    return pl.pallas_call(
        _mm_res_kernel,
        name="mm_residual",
        grid=(s // tm, n // tn),
        in_specs=[
            pl.BlockSpec((tm, k), lambda i, j: (i, 0)),
            _w_spec(w, layer, k, tn),
            pl.BlockSpec((tm, tn), lambda i, j: (i, j)),
        ],
        out_specs=pl.BlockSpec((tm, tn), lambda i, j: (i, j)),
        out_shape=jax.ShapeDtypeStruct((s, n), F32),
        compiler_params=_params(("parallel", "arbitrary")),
    )(x, w, res)


def _ffn_in_kernel(x_ref, g_ref, wg_ref, wu_ref, o_ref, xn_ref):
    @pl.when(pl.program_id(1) == 0)
    def _():
        xn_ref[...] = _rms(x_ref[...], g_ref[...]).astype(BF16)

    xn = xn_ref[...]
    a = jnp.dot(xn, wg_ref[...], preferred_element_type=F32)
    u = jnp.dot(xn, wu_ref[...], preferred_element_type=F32)
    o_ref[...] = (a * jax.nn.sigmoid(a) * u).astype(o_ref.dtype)


def ffn_in(h, gain, w_in, *, layer=None, tm=1024, tn=512):
    s, d = h.shape
    f = w_in.shape[-1] // 2
    tm = min(tm, s)
    nj = f // tn
    return pl.pallas_call(
        _ffn_in_kernel,
        name="ffn_in",
        grid=(s // tm, nj),
        in_specs=[
            pl.BlockSpec((tm, d), lambda i, j: (i, 0)),
            pl.BlockSpec((1, d), lambda i, j: (0, 0)),
            _w_spec(w_in, layer, d, tn),
            _w_spec(w_in, layer, d, tn, nj),
        ],
        out_specs=pl.BlockSpec((tm, tn), lambda i, j: (i, j)),
        out_shape=jax.ShapeDtypeStruct((s, f), BF16),
        scratch_shapes=[pltpu.VMEM((tm, d), BF16)],
        compiler_params=_params(("parallel", "arbitrary")),
    )(h, gain.reshape(1, d), w_in, w_in)


def _ple_kernel(x_ref, g_ref, wg_ref, p_ref, wp_ref, *rest, out_norm):
    fg_ref, o_ref = rest if out_norm else (None,) + rest
    x = x_ref[...]
    xn = _rms(x, g_ref[...]).astype(BF16)
    gate = jax.nn.sigmoid(jnp.dot(xn, wg_ref[...], preferred_element_type=F32))
    proj = jnp.dot(p_ref[...].astype(BF16), wp_ref[...], preferred_element_type=F32)
    y = x + gate * proj
    o_ref[...] = _rms(y, fg_ref[...]) if out_norm else y


def ple(h, gain, w_gate, p, w_proj, *, layer, out_gain=None, tm=512):
    s, d = h.shape
    pd = p.shape[-1]
    tm = min(tm, s)
    row = pl.BlockSpec((1, d), lambda i, j: (0, 0))
    in_specs = [
        pl.BlockSpec((tm, d), lambda i, j: (i, 0)),
        row,
        _w_spec(w_gate, layer, d, d),
        pl.BlockSpec((None, None, tm, pd), lambda i, j: (layer, 0, i, 0)),
        _w_spec(w_proj, layer, pd, d),
    ]
    args = [h, gain.reshape(1, d), w_gate, p, w_proj]
    if out_gain is not None:
        in_specs.append(row)
        args.append(out_gain.reshape(1, d))
    return pl.pallas_call(
        functools.partial(_ple_kernel, out_norm=out_gain is not None),
        name="ple",
        grid=(s // tm, 1),
        in_specs=in_specs,
        out_specs=pl.BlockSpec((tm, d), lambda i, j: (i, 0)),
        out_shape=jax.ShapeDtypeStruct((s, d), F32),
        compiler_params=_params(("parallel", "arbitrary")),
    )(*args)


IDX_TQ = 256
IDX_KC = 256


def _orderable(x):
    b = pltpu.bitcast(x, jnp.int32)
    return jnp.where(b < 0, b ^ jnp.int32(0x7FFFFFFF), b)


def _indexer_kernel(qi_ref, ki_ref, wi_ref, mask_ref, keys_ref, *, topk, n_kblocks):
    tq, kc = IDX_TQ, IDX_KC
    i = pl.program_id(0)
    nkc = (i + 1) * (tq // kc)
    n_idx_bits = (n_kblocks * kc - 1).bit_length()
    w_t = (wi_ref[...] * (IDX_HEADS ** -0.5 * IDX_DIM ** -0.5)).T
    qchunk = (i * tq + lax.broadcasted_iota(jnp.int32, (kc, tq), 1)) >> CHUNK_SHIFT
    krow = lax.broadcasted_iota(jnp.int32, (kc, tq), 0)

    def valid_of(c):
        return ((c * kc + krow) >> CHUNK_SHIFT) <= qchunk

    def score_chunk(c):
        k0 = pl.multiple_of(c * kc, kc)
        kblk = ki_ref[pl.ds(k0, kc), :].astype(BF16)
        acc = jnp.zeros((kc, tq), F32)
        for h in range(IDX_HEADS):
            s = lax.dot_general(kblk, qi_ref[:, h * IDX_DIM:(h + 1) * IDX_DIM], NT_DIMS,
                                preferred_element_type=F32)
            acc = acc + jnp.maximum(s, 0.0) * w_t[h:h + 1, :]
        keys_ref[pl.ds(k0, kc), :] = jnp.where(valid_of(c), _orderable(acc), jnp.int32(INT_MIN))

    _for_tiles_in_pairs(0, nkc, score_chunk, per_trip=(4, 2, 1))

    @pl.when(nkc % 2 == 1)
    def _():
        keys_ref[pl.ds(pl.multiple_of(nkc * kc, kc), kc), :] = jnp.full((kc, tq), INT_MIN, jnp.int32)

    def count(pred_of):
        nacc = 4

        def body(cp, cnt):
            for c in (2 * cp, 2 * cp + 1):
                k0 = pl.multiple_of(c * kc, kc)
                hit = pred_of(c, keys_ref[pl.ds(k0, kc), :]).astype(jnp.int32)
                cnt = cnt + hit.reshape(kc // (8 * nacc), nacc, 8, tq).sum(axis=0)
            return cnt
        cnt = lax.fori_loop(0, (nkc + 1) // 2, body, jnp.zeros((nacc, 8, tq), jnp.int32))
        return cnt.sum(axis=0).sum(axis=0, keepdims=True)

    def bit_step(b, state):
        thr, cnt = state
        cand = thr + lax.shift_left(jnp.int32(1), 31 - b)
        c_cand = count(lambda c, keys: keys >= cand)
        take = c_cand >= topk
        return jnp.where(take, cand, thr), jnp.where(take, c_cand, cnt)

    thr, cnt = lax.fori_loop(
        0, 32, bit_step,
        (jnp.full((1, tq), INT_MIN, jnp.int32), jnp.full((1, tq), nkc * kc, jnp.int32)))

    def tie_limit():
        need = topk - count(lambda c, keys: keys > thr)
        def bit_body(b, x):
            cand = x + lax.shift_left(jnp.int32(1), n_idx_bits - 1 - b)
            below = count(lambda c, keys: (keys == thr) & ((c * kc + krow) < cand))
            return jnp.where(below < need, cand, x)
        return lax.fori_loop(0, n_idx_bits, bit_body, jnp.zeros((1, tq), jnp.int32))

    def write_masks(selected):
        def mask_body(c, carry):
            k0 = pl.multiple_of(c * kc, kc)
            sel = selected(c, keys_ref[pl.ds(k0, kc), :]) & valid_of(c)
            mask_ref[c] = jnp.where(sel, 0.0, -jnp.inf).astype(F32).T
            return carry
        lax.fori_loop(0, nkc, mask_body, 0)

    has_ties = jnp.max(jnp.where((cnt > topk) & (thr > INT_MIN), 1, 0)) > 0

    @pl.when(has_ties)
    def _():
        tie_last = tie_limit()
        write_masks(lambda c, keys: (keys > thr) | ((keys == thr) & ((c * kc + krow) <= tie_last)))

    @pl.when(jnp.logical_not(has_ties))
    def _():
        write_masks(lambda c, keys: keys >= thr)

    def fill_body(c, carry):
        mask_ref[c] = jnp.full((tq, kc), -jnp.inf, F32)
        return carry

    lax.fori_loop(nkc, n_kblocks, fill_body, 0)


def dsa_indexer(qi, kiwi, *, topk):
    s = qi.shape[0]
    tq, kc = IDX_TQ, IDX_KC
    nkb = s // kc
    assert s % tq == 0 and nkb % 2 == 0
    return pl.pallas_call(
        functools.partial(_indexer_kernel, topk=topk, n_kblocks=nkb),
        name="dsa_indexer",
        grid=(s // tq,),
        in_specs=[
            pl.BlockSpec((tq, IDX_HEADS * IDX_DIM), lambda i: (i, 0)),
            pl.BlockSpec((s, IDX_DIM), lambda i: (0, 0)),
            pl.BlockSpec((tq, LANE), lambda i: (i, 1)),
        ],
        out_specs=pl.BlockSpec((nkb, tq, kc), lambda i: (0, i, 0)),
        out_shape=jax.ShapeDtypeStruct((nkb, s, kc), F32),
        scratch_shapes=[pltpu.VMEM((s, tq), jnp.int32)],
        compiler_params=_params(("parallel",)),
    )(qi, kiwi, kiwi)


ATT_A_TQ = 256
ATT_A_KB = 2
ATT_A_NEAR = (T5_FAR + ATT_A_KB * IDX_KC - 2) // IDX_KC + 1


def _t5_bucket(rel):
    nb = T5_BUCKETS // 2
    max_exact = nb // 2
    ret = jnp.where(rel > 0, nb, 0)
    n = jnp.abs(rel)
    nf = jnp.maximum(n, 1).astype(F32)
    large = max_exact + (jnp.log(nf / max_exact) / math.log(T5_MAX_DISTANCE / max_exact)
                         * (nb - max_exact)).astype(jnp.int32)
    large = jnp.minimum(large, nb - 1)
    return ret + jnp.where(n < max_exact, n, large)


def _softmax_step(s, v, m_ref, l_ref, acc_ref, rows, tk, shift=None):
    d = v.shape[1]
    m_prev = m_ref[rows, :]
    mx = jnp.max(s, axis=1, keepdims=True)
    if shift is not None:
        mx = mx + shift
    m_next = jnp.maximum(m_prev, mx)
    pivot = m_next if shift is None else m_next - shift
    p = jnp.exp2(s - jnp.concatenate([pivot] * (tk // LANE), axis=1)).astype(BF16)
    alpha = jnp.exp2(m_prev - m_next)
    v_ones = jnp.concatenate([v, jnp.ones((tk, LANE), BF16)], axis=1)
    pv = jnp.dot(p, v_ones, preferred_element_type=F32)
    m_ref[rows, :] = m_next
    l_ref[rows, :] = alpha * l_ref[rows, :] + pv[:, d:]
    acc_ref[rows, :] = acc_ref[rows, :] * alpha + pv[:, :d]


def _for_tiles_in_pairs(lo, hi, tile_fn, per_trip=(2, 1)):
    assert per_trip[-1] == 1
    start = lo
    for size in per_trip:
        def group(kp, carry, size=size, start=start):
            for u in range(size):
                tile_fn(start + size * kp + u)
            return carry

        n = (hi - start) // size
        lax.fori_loop(0, n, group, 0)
        start = start + n * size


def _attn_a_kernel(qmin_ref, kmax_ref, t5s_ref, q_ref, k_ref, v_ref, mask_ref, posq_ref, posk_ref,
                   t5t_ref, o_ref, m_ref, l_ref, acc_ref, nbias_ref, *, consecutive):
    tq, kc, nb = ATT_A_TQ, IDX_KC, ATT_A_KB
    tk = nb * kc
    g = pl.program_id(0)
    i = pl.program_id(1)

    def head_bias(r, bucket):
        row = t5t_ref[pl.ds(g * A_GROUP + r, 1), :] * LOG2E
        tbl = jnp.broadcast_to(row, (tq, LANE))
        return jnp.concatenate([_lane_gather(tbl, bucket[:, c * LANE:(c + 1) * LANE])
                                for c in range(tk // LANE)], axis=1)

    if consecutive:
        @pl.when(i == 0)
        def _():
            rel0 = (lax.broadcasted_iota(jnp.int32, (tq, tk), 1)
                    - lax.broadcasted_iota(jnp.int32, (tq, tk), 0))
            for d in range(ATT_A_NEAR):
                bucket = _t5_bucket(rel0 + (d - ATT_A_NEAR + 1) * kc)
                for r in range(A_GROUP):
                    nbias_ref[d, r] = head_bias(r, bucket)

    m_ref[...] = jnp.full(m_ref.shape, NEG_BIG, F32)
    l_ref[...] = jnp.zeros(l_ref.shape, F32)
    acc_ref[...] = jnp.zeros(acc_ref.shape, F32)
    nkt = ((i + 1) * (tq // kc) + nb - 1) // nb
    qmin = qmin_ref[i]

    def is_far(kt):
        ktc = jnp.minimum(kt, nkt - 1) * nb
        kmax = kmax_ref[ktc]
        for b in range(1, nb):
            kmax = jnp.maximum(kmax, kmax_ref[ktc + b])
        return (kt < nkt) & ((qmin - kmax) >= T5_FAR)

    n_far = lax.while_loop(is_far, lambda kt: kt + 1, jnp.int32(0))

    def tile(kt, bias_of, shift_of):
        k0 = pl.multiple_of(kt * tk, tk)
        k_t = k_ref[pl.ds(k0, tk), :]
        v_t = v_ref[pl.ds(k0, tk), :]
        base = jnp.concatenate([mask_ref[kt * nb + b] for b in range(nb)], axis=1)
        for r in range(A_GROUP):
            rows = slice(r * tq, (r + 1) * tq)
            q_r = q_ref[:, r * A_HEAD_DIM:(r + 1) * A_HEAD_DIM]
            s = lax.dot_general(q_r, k_t, NT_DIMS, preferred_element_type=F32) + bias_of(r, base)
            _softmax_step(s, v_t, m_ref, l_ref, acc_ref, rows, tk, shift_of(r))

    def far_tile(kt):
        tile(kt, lambda r, base: base,
             lambda r: t5s_ref[T5_BUCKETS // 2 - 1, g * A_GROUP + r] * LOG2E)

    def near_tile(kt):
        if consecutive:
            d = kt * nb - i * (tq // kc) + (ATT_A_NEAR - 1)
            tile(kt, lambda r, base: base + nbias_ref[d, r], lambda r: None)
        else:
            pk = jnp.concatenate([posk_ref[kt * nb + b] for b in range(nb)], axis=1)
            bucket = _t5_bucket(pk - posq_ref[...])
            tile(kt, lambda r, base: base + head_bias(r, bucket), lambda r: None)

    _for_tiles_in_pairs(0, n_far, far_tile, per_trip=(4, 2, 1))
    _for_tiles_in_pairs(n_far, nkt, near_tile)
    for r in range(A_GROUP):
        rows = slice(r * tq, (r + 1) * tq)
        o_ref[:, r * A_HEAD_DIM:(r + 1) * A_HEAD_DIM] = (acc_ref[rows, :] / l_ref[rows, :]).astype(o_ref.dtype)


def dsa_attention(qkv, mask, pos, t5_table, *, consecutive):
    s = qkv.shape[0]
    tq, tk = ATT_A_TQ, IDX_KC
    assert (s // tk) % ATT_A_KB == 0
    gw = A_GROUP * A_HEAD_DIM
    qmin = pos.reshape(s // tq, tq).min(axis=1)
    kmax = pos.reshape(s // tk, tk).max(axis=1)
    t5t = jnp.zeros((A_HEADS, LANE), F32).at[:, :T5_BUCKETS].set(t5_table.T)
    kblk0 = (A_HEADS * A_HEAD_DIM) // A_HEAD_DIM
    smem = pl.BlockSpec(memory_space=pltpu.SMEM)
    nbias_shape = (ATT_A_NEAR, A_GROUP, tq, ATT_A_KB * tk) if consecutive else (1, 1, 8, LANE)
    return pl.pallas_call(
        functools.partial(_attn_a_kernel, consecutive=consecutive),
        name="dsa_attention",
        grid=(A_KV_HEADS, s // tq),
        in_specs=[
            smem, smem, smem,
            pl.BlockSpec((tq, gw), lambda g, i: (i, g)),
            pl.BlockSpec((s, A_HEAD_DIM), lambda g, i: (0, kblk0 + g)),
            pl.BlockSpec((s, A_HEAD_DIM), lambda g, i: (0, kblk0 + A_KV_HEADS + g)),
            pl.BlockSpec((s // tk, tq, tk), lambda g, i: (0, i, 0)),
            pl.BlockSpec((tq, 1), lambda g, i: (i, 0)),
            pl.BlockSpec((s // tk, 1, tk), lambda g, i: (0, 0, 0)),
            pl.BlockSpec((A_HEADS, LANE), lambda g, i: (0, 0)),
        ],
        out_specs=pl.BlockSpec((tq, gw), lambda g, i: (i, g)),
        out_shape=jax.ShapeDtypeStruct((s, A_HEADS * A_HEAD_DIM), BF16),
        scratch_shapes=[
            pltpu.VMEM((A_GROUP * tq, LANE), F32),
            pltpu.VMEM((A_GROUP * tq, LANE), F32),
            pltpu.VMEM((A_GROUP * tq, A_HEAD_DIM), F32),
            pltpu.VMEM(nbias_shape, F32),
        ],
        compiler_params=_params(("parallel", "arbitrary")),
    )(qmin, kmax, t5_table, qkv, qkv, qkv, mask, pos.reshape(s, 1), pos.reshape(s // tk, 1, tk), t5t)


ATT_B_TQ = 256
ATT_B_NKB = B_PREV_CHUNKS * CHUNK // ATT_B_TQ + 1
ATT_B_LG = 4


def _attn_b_kernel(t256_ref, q_ref, k0_ref, k1_ref, k2_ref, v0_ref, v1_ref, v2_ref, posq_ref, posk_ref,
                   relt_ref, o_ref, bias_ref, *, consecutive):
    tq = ATT_B_TQ
    nh = LANE // B_HEAD_DIM
    gp = pl.program_id(0)
    i = pl.program_id(1)
    k_refs = (k0_ref, k1_ref, k2_ref)
    v_refs = (v0_ref, v1_ref, v2_ref)

    def build_bias():
        row_i = lax.broadcasted_iota(jnp.int32, (tq, tq), 0)
        col_i = lax.broadcasted_iota(jnp.int32, (tq, tq), 1)
        for j in range(ATT_B_NKB):
            back = (ATT_B_NKB - 1 - j) * tq
            if consecutive:
                rel = row_i - (col_i - back)
            else:
                rel = posq_ref[...] - posk_ref[jnp.maximum(i - (ATT_B_NKB - 1) + j, 0)]
            r = jnp.clip(rel, -B_REL_CLIP, B_REL_CLIP) + B_REL_CLIP
            dchunk = (row_i >> CHUNK_SHIFT) - ((col_i - back) >> CHUNK_SHIFT)
            band = (dchunk >= 0) & (dchunk <= B_PREV_CHUNKS)
            for hl in range(ATT_B_LG * nh):
                h = gp * (ATT_B_LG * nh) + hl
                row = relt_ref[pl.ds(h, 1), :] * LOG2E
                seg0 = jnp.broadcast_to(row[:, :LANE], (tq, LANE))
                seg1 = jnp.broadcast_to(row[:, LANE:2 * LANE], (tq, LANE))
                t256 = t256_ref[h] * LOG2E
                for c in range(tq // LANE):
                    cs = slice(c * LANE, (c + 1) * LANE)
                    rc = r[:, cs]
                    lo = rc & (LANE - 1)
                    bias = jnp.where(rc < LANE, _lane_gather(seg0, lo),
                                     jnp.where(rc < 2 * LANE, _lane_gather(seg1, lo), t256))
                    bias_ref[hl, :, j * tq + c * LANE:j * tq + (c + 1) * LANE] = jnp.where(band[:, cs], bias, -jnp.inf)

    if consecutive:
        pl.when(i == 0)(build_bias)
    else:
        build_bias()

    lane_head = lax.broadcasted_iota(jnp.int32, (1, LANE), 1) // B_HEAD_DIM
    for lg in range(ATT_B_LG):
        lanes = slice(lg * LANE, (lg + 1) * LANE)
        qp = q_ref[:, lanes]
        out = jnp.zeros((tq, LANE), F32)
        for hh in range(nh):
            mine = lane_head == hh
            qm = jnp.where(mine, qp, jnp.zeros_like(qp))
            parts = []
            for j in range(ATT_B_NKB):
                sj = lax.dot_general(qm, k_refs[j][:, lanes], NT_DIMS, preferred_element_type=F32)
                sj = sj + bias_ref[lg * nh + hh, :, j * tq:(j + 1) * tq]
                if j < ATT_B_NKB - 1:
                    sj = sj + jnp.where(i - (ATT_B_NKB - 1) + j >= 0, 0.0, -jnp.inf)
                parts.append(sj)
            s = jnp.concatenate(parts, axis=1)
            p = jnp.exp2(s - jnp.max(s, axis=1, keepdims=True)).astype(BF16)
            acc = jnp.zeros((tq, LANE), F32)
            for j in range(ATT_B_NKB):
                vj = v_refs[j][:, lanes]
                vm = jnp.where(mine, vj, jnp.ones_like(vj))
                acc = acc + jnp.dot(p[:, j * tq:(j + 1) * tq], vm, preferred_element_type=F32)
            rowsum = pltpu.roll(acc, B_HEAD_DIM, 1)
            out = out + jnp.where(mine, acc / rowsum, 0.0)
        o_ref[:, lanes] = out.astype(o_ref.dtype)


def band_attention(qkv, pos, rel_table, *, consecutive):
    s = qkv.shape[0]
    tq = ATT_B_TQ
    hw = B_HEADS * B_HEAD_DIM
    bw = ATT_B_LG * LANE
    ngrp = hw // bw
    assert LANE // B_HEAD_DIM == 2 and ATT_B_NKB == 3
    nrel = 2 * B_REL_CLIP + 1
    relt = jnp.zeros((B_HEADS, 3 * LANE), F32).at[:, :nrel].set(rel_table.T)
    t256 = rel_table[nrel - 1]

    def kv_spec(j, base):
        return pl.BlockSpec((tq, bw), lambda gp, i: (jnp.maximum(i - (ATT_B_NKB - 1) + j, 0), base + gp))

    return pl.pallas_call(
        functools.partial(_attn_b_kernel, consecutive=consecutive),
        name="band_attention",
        grid=(ngrp, s // tq),
        in_specs=[
            pl.BlockSpec(memory_space=pltpu.SMEM),
            pl.BlockSpec((tq, bw), lambda gp, i: (i, gp)),
            kv_spec(0, ngrp), kv_spec(1, ngrp), kv_spec(2, ngrp),
            kv_spec(0, 2 * ngrp), kv_spec(1, 2 * ngrp), kv_spec(2, 2 * ngrp),
            pl.BlockSpec((tq, 1), lambda gp, i: (i, 0)),
            pl.BlockSpec((s // tq, 1, tq), lambda gp, i: (0, 0, 0)),
            pl.BlockSpec((B_HEADS, 3 * LANE), lambda gp, i: (0, 0)),
        ],
        out_specs=pl.BlockSpec((tq, bw), lambda gp, i: (i, gp)),
        out_shape=jax.ShapeDtypeStruct((s, hw), BF16),
        scratch_shapes=[pltpu.VMEM((ATT_B_LG * (LANE // B_HEAD_DIM), tq, ATT_B_NKB * tq), F32)],
        compiler_params=_params(("parallel", "arbitrary")),
    )(t256, qkv, qkv, qkv, qkv, qkv, qkv, qkv, pos.reshape(s, 1), pos.reshape(s // tq, 1, tq), relt)


ATT_C_T = 512
ATT_C_HEADS = 4


def _attn_c_kernel(q_ref, kn_ref, kr_ref, v_ref, o_ref, m_ref, l_ref, acc_ref):
    t, nh = ATT_C_T, ATT_C_HEADS
    iq = pl.program_id(1)
    m_ref[...] = jnp.full(m_ref.shape, NEG_BIG, F32)
    l_ref[...] = jnp.zeros(l_ref.shape, F32)
    acc_ref[...] = jnp.zeros(acc_ref.shape, F32)

    def tile(kt, diag):
        k0 = pl.multiple_of(kt * t, t)
        kr_t = kr_ref[pl.ds(k0, t), :]
        for hh in range(nh):
            kcat = jnp.concatenate([kn_ref[pl.ds(k0, t), hh * C_NOPE:(hh + 1) * C_NOPE], kr_t], axis=1)
            s = lax.dot_general(q_ref[:, hh * 2 * LANE:(hh + 1) * 2 * LANE], kcat, NT_DIMS,
                                preferred_element_type=F32)
            if diag:
                qc = lax.broadcasted_iota(jnp.int32, (t, t), 0) >> CHUNK_SHIFT
                kc = lax.broadcasted_iota(jnp.int32, (t, t), 1) >> CHUNK_SHIFT
                s = jnp.where(kc <= qc, s, -jnp.inf)
            _softmax_step(s, v_ref[pl.ds(k0, t), hh * C_V:(hh + 1) * C_V], m_ref, l_ref, acc_ref,
                          slice(hh * t, (hh + 1) * t), t)

    _for_tiles_in_pairs(0, iq, lambda kt: tile(kt, False), per_trip=(4, 2, 1))
    tile(iq, True)
    for hh in range(nh):
        rows = slice(hh * t, (hh + 1) * t)
        o_ref[:, hh * C_V:(hh + 1) * C_V] = (acc_ref[rows, :] / l_ref[rows, :]).astype(o_ref.dtype)


def mla_attention(qcat, kv, kr):
    s = qcat.shape[0]
    t, nh = ATT_C_T, ATT_C_HEADS
    assert s % t == 0
    return pl.pallas_call(
        _attn_c_kernel,
        name="mla_attention",
        grid=(C_HEADS // nh, s // t),
        in_specs=[
            pl.BlockSpec((t, nh * 2 * LANE), lambda h, i: (i, h)),
            pl.BlockSpec((s, nh * C_NOPE), lambda h, i: (0, h)),
            pl.BlockSpec((s, LANE), lambda h, i: (0, 0)),
            pl.BlockSpec((s, nh * C_V), lambda h, i: (0, C_HEADS // nh + h)),
        ],
        out_specs=pl.BlockSpec((t, nh * C_V), lambda h, i: (i, h)),
        out_shape=jax.ShapeDtypeStruct((s, C_HEADS * C_V), BF16),
        scratch_shapes=[pltpu.VMEM((nh * t, LANE), F32), pltpu.VMEM((nh * t, LANE), F32),
                        pltpu.VMEM((nh * t, C_V), F32)],
        compiler_params=_params(("parallel", "arbitrary")),
    )(qcat, kv, kr, kv)


def _rope_tables(pos):
    half = ROPE_DIM // 2
    inv = ROPE_BASE ** (-jnp.arange(half, dtype=F32) * 2.0 / ROPE_DIM)
    ang = pos.astype(F32)[:, None] * inv
    cos, sin = jnp.cos(ang), jnp.sin(ang)
    s = pos.shape[0]
    z = jnp.zeros((s, half), F32)
    c = jnp.concatenate([cos, cos, jnp.ones((s, LANE - ROPE_DIM), F32)], axis=1)
    s1 = jnp.concatenate([-sin, z, z, z], axis=1)
    s2 = jnp.concatenate([z, sin, z, z], axis=1)
    return c, s1, s2


def _query_scale(n_query, n_total, scale):
    return jnp.concatenate([jnp.full((n_query,), scale * LOG2E, F32), jnp.ones((n_total - n_query,), F32)])


def _by_position_layout(consecutive, fn, *args):
    return lax.cond(consecutive, functools.partial(fn, consecutive=True),
                    functools.partial(fn, consecutive=False), *args)


def _mixer_a(h, gain, pos, consecutive, tables, w_in, w_out, layer, t5_table):
    s = h.shape[0]
    d = h.shape[1]
    nq = A_HEADS * A_HEAD_DIM
    nkv = A_KV_HEADS * A_HEAD_DIM
    nqi = IDX_HEADS * IDX_DIM
    o_qkv = nq + 2 * nkv
    o_ki = o_qkv + nqi
    tn = 1024
    assert o_qkv % tn == 0 and nqi % tn == 0
    qkv = norm_mm(h, 0, d, gain, w_in, layer=layer, n=o_qkv, tn=o_qkv // 2, out_dtype=BF16, name="a_qkv_proj",
                  col_scale=_query_scale(nq, o_qkv, A_HEAD_DIM ** -0.5))
    qi = norm_mm(h, 0, d, gain, w_in, layer=layer, n=nqi, col0=o_qkv // tn, tn=tn, out_dtype=BF16,
                 rope=(True,) * (tn // LANE), tables=tables, name="a_idxq_proj")
    w_kw = jnp.zeros((d, 2 * LANE), BF16).at[:, :IDX_DIM + IDX_HEADS].set(w_in[layer, :, o_ki:].astype(BF16))
    kiwi = norm_mm(h, 0, d, gain, w_kw, tn=2 * LANE, out_dtype=F32, rope=(True, False), tables=tables,
                   name="a_idxk_proj")
    mask = dsa_indexer(qi, kiwi, topk=min(IDX_TOPK_MAX, s // 4))
    o = _by_position_layout(consecutive, dsa_attention, qkv, mask, pos, t5_table)
    return mm_residual(o, w_out, h, layer=layer, tn=w_out.shape[-1])


def _mixer_b(h, gain, pos, consecutive, w_in, rel_table, w_out, layer):
    d = h.shape[1]
    n = w_in.shape[-1]
    qkv = norm_mm(h, 0, d, gain, w_in, layer=layer, tn=2048, out_dtype=BF16, name="b_qkv_proj",
                  col_scale=_query_scale(n // 3, n, B_HEAD_DIM ** -0.5))
    o = _by_position_layout(consecutive, band_attention, qkv, pos, rel_table)
    return mm_residual(o, w_out, h, layer=layer, tn=w_out.shape[-1])


def _mixer_c(h, gain, tables, w_down, g_q, g_kv, w_uq, w_ukv, w_out, layer):
    d = h.shape[1]
    lq = g_q.shape[0]
    lkv = g_kv.shape[0]
    n_down = lq + lkv + LANE
    wd = jnp.zeros((d, n_down), BF16).at[:, :w_down.shape[1]].set(w_down.astype(BF16))
    down = norm_mm(h, 0, d, gain, wd, tn=n_down, out_dtype=F32,
                   rope=(False,) * ((lq + lkv) // LANE) + (True,), tables=tables, name="c_down_proj")
    wq = w_uq.astype(BF16).reshape(lq, C_HEADS, C_NOPE + C_ROPE)
    wq = jnp.pad(wq, ((0, 0), (0, 0), (0, 2 * LANE - C_NOPE - C_ROPE))).reshape(lq, C_HEADS * 2 * LANE)
    nqc = wq.shape[1]
    qcat = norm_mm(down, 0, lq, g_q, wq, tn=nqc, out_dtype=BF16, rope=(False, True) * C_HEADS, tables=tables,
                   name="c_uq_proj", col_scale=_query_scale(nqc, nqc, (C_NOPE + C_ROPE) ** -0.5))
    wkv = w_ukv.astype(BF16).reshape(lkv, C_HEADS, 2, C_NOPE).transpose(0, 2, 1, 3).reshape(lkv, -1)
    assert lq == lkv
    kv = norm_mm(down, 1, lkv, g_kv, wkv, tn=wkv.shape[1], out_dtype=BF16, name="c_ukv_proj")
    kr = down[:, lq + lkv:].astype(BF16)
    o = mla_attention(qcat, kv, kr)
    return mm_residual(o, w_out, h, layer=layer, tn=w_out.shape[-1])


def kernel(x, p, positions, t5_table, a_w_in, a_w_out, b_w_in, b_rel_table, b_w_out, c_w_down, c_q_norm,
           c_kv_norm, c_w_uq, c_w_ukv, c_w_out, attn_norm, ffn_norm, ffn_w_in, ffn_w_out, ple_norm,
           ple_w_gate, ple_w_proj, final_norm):
    assert x.shape[0] == 1
    depth = attn_norm.shape[0]
    h = x[0]
    pos = positions[0]
    tables = _rope_tables(pos)
    consecutive = jnp.all(pos[1:] - pos[:-1] == 1)
    a_w_in, a_w_out, b_w_in, b_w_out, c_w_out, ffn_w_in, ffn_w_out, ple_w_gate, ple_w_proj = (
        w.astype(BF16) for w in (a_w_in, a_w_out, b_w_in, b_w_out, c_w_out, ffn_w_in, ffn_w_out, ple_w_gate,
                                 ple_w_proj))
    for i in range(depth):
        j, kind = divmod(i, 3)
        if kind == 0:
            h = _mixer_a(h, attn_norm[i], pos, consecutive, tables, a_w_in, a_w_out, j, t5_table)
        elif kind == 1:
            h = _mixer_b(h, attn_norm[i], pos, consecutive, b_w_in, b_rel_table[j], b_w_out, j)
        else:
            h = _mixer_c(h, attn_norm[i], tables, c_w_down[j], c_q_norm[j], c_kv_norm[j], c_w_uq[j],
                         c_w_ukv[j], c_w_out, j)
        act = ffn_in(h, ffn_norm[i], ffn_w_in, layer=i)
        h = mm_residual(act, ffn_w_out, h, layer=i, tn=1024)
        h = ple(h, ple_norm[i], ple_w_gate, p, ple_w_proj, layer=i,
                out_gain=final_norm if i == depth - 1 else None)
    return h[None]
```
